```python
import math
import jax, jax.numpy as jnp
from jax import lax
import numpy as np


D_MODEL = 1024
BATCH = 8
SEQ = 8192
DEPTH = 4

GRID_W = 64
CTX_LEN = 256
N_MIXERS = 3
N_A = (DEPTH + 2) // 3
N_B = (DEPTH + 1) // 3
N_C = DEPTH // 3
Q_BLOCK = 128
RMS_EPS = 1e-6
ROPE_BASE = 10000.0

DA_HEADS = 8
DA_HEAD_DIM = 64
DA_V_DIM = 2 * DA_HEAD_DIM
DA_KW = DA_HEADS * 2 * DA_HEAD_DIM
DA_VW = DA_HEADS * DA_V_DIM
DA_QW = DA_KW
DA_IN = DA_KW + DA_VW + DA_QW

RET_HEADS = 4
RET_KDIM = D_MODEL // RET_HEADS
RET_VDIM = 2 * RET_KDIM
RET_CHUNK = 128
RET_KW = RET_HEADS * RET_KDIM
RET_VW = RET_HEADS * RET_VDIM
RET_IN = 2 * RET_KW + 2 * RET_VW

MLA_HEADS = 8
MLA_NOPE = 128
MLA_ROPE = 64
MLA_V = 128
MLA_Q_RANK = 256
MLA_KV_RANK = 128
MLA_IN = MLA_KV_RANK + MLA_ROPE + MLA_Q_RANK

N_EXPERTS = 32
TOP_K = 4
D_EXPERT = 1024
SWIGLU_LIMIT = 7.0
SWIGLU_ALPHA = 1.702
MOE_BLOCK = 512

kernel_name = 'hybrid_diffattn_retention_mla_moe_dit'


def rms_norm(x, gain=None):
    xf = x.astype(jnp.float32)
    y = xf * lax.rsqrt(jnp.mean(xf * xf, axis=-1, keepdims=True) + RMS_EPS)
    if gain is not None:
        y = y * gain.astype(jnp.float32)
    return y.astype(x.dtype)


def modulate(h, shift, scale):
    return h * (1.0 + scale) + shift


def adaln(cond, w, b, n_chunks):
    width = n_chunks * D_MODEL
    m = jax.nn.silu(cond) @ w[:, :width] + b[:width]
    return jnp.split(m, n_chunks, axis=-1)


def rope_angles(pos, dim):
    inv_freq = ROPE_BASE ** (-jnp.arange(0, dim, 2, dtype=jnp.float32) / dim)
    return pos.astype(jnp.float32)[:, None] * inv_freq[None, :]


def apply_rotary(x, ang):
    xf = x.astype(jnp.float32)
    x1, x2 = jnp.split(xf, 2, axis=-1)
    cos = jnp.cos(ang)[None, :, None, :]
    sin = jnp.sin(ang)[None, :, None, :]
    return jnp.concatenate([x1 * cos - x2 * sin, x1 * sin + x2 * cos], axis=-1).astype(x.dtype)


def axial_rope(x, rows, cols):
    half = x.shape[-1] // 2
    return jnp.concatenate([apply_rotary(x[..., :half], rope_angles(rows, half)),
                            apply_rotary(x[..., half:], rope_angles(cols, half))], axis=-1)


def to_query_blocks(a):
    b, t = a.shape[:2]
    return jnp.swapaxes(a.reshape(b, t // Q_BLOCK, Q_BLOCK, *a.shape[2:]), 0, 1)


def from_query_blocks(a):
    a = jnp.swapaxes(a, 0, 1)
    return a.reshape(a.shape[0], -1, *a.shape[3:])


def softmax_attend(q, k, v, scale):
    s = jnp.einsum('bqhd,bkhd->bhqk', q, k).astype(jnp.float32) * scale
    p = jax.nn.softmax(s, axis=-1).astype(v.dtype)
    return jnp.einsum('bhqk,bkhv->bqhv', p, v)


def diff_attention(h_lat, h_ctx, w_in, w_out, lam_vecs, subln_g, lam_init, rows, cols, need_ctx):
    def split_kv(p):
        b, t, _ = p.shape
        k = p[..., :DA_KW].reshape(b, t, DA_HEADS, 2, DA_HEAD_DIM)
        v = p[..., DA_KW:DA_KW + DA_VW].reshape(b, t, DA_HEADS, DA_V_DIM)
        return k, v

    def as_q(p):
        b, t, _ = p.shape
        return p.reshape(b, t, DA_HEADS, 2, DA_HEAD_DIM)

    def rope(a):
        b, t = a.shape[:2]
        return axial_rope(a.reshape(b, t, 2 * DA_HEADS, DA_HEAD_DIM), rows, cols).reshape(a.shape)

    lf = lam_vecs.astype(jnp.float32)
    lam = jnp.exp(jnp.sum(lf[0] * lf[1])) - jnp.exp(jnp.sum(lf[2] * lf[3])) + lam_init
    scale = DA_HEAD_DIM ** -0.5

    def attend(q, k, v):
        s = jnp.einsum('bqhmd,bkhmd->bhmqk', q, k).astype(jnp.float32) * scale
        p = jax.nn.softmax(s, axis=-1)
        a = (p[:, :, 0] - lam * p[:, :, 1]).astype(v.dtype)
        return jnp.einsum('bhqk,bkhv->bqhv', a, v)

    def finish(o):
        b, t = o.shape[:2]
        o = rms_norm(o, subln_g) * (1.0 - lam_init)
        return o.reshape(b, t, DA_VW) @ w_out

    p_lat = h_lat @ w_in
    k_lat, v_lat = split_kv(p_lat)
    k_lat = rope(k_lat)
    q_lat = rope(as_q(p_lat[..., DA_KW + DA_VW:]))
    k_ctx, v_ctx = split_kv(h_ctx @ w_in[:, :DA_KW + DA_VW])
    k_all = jnp.concatenate([k_ctx, k_lat], axis=1)
    v_all = jnp.concatenate([v_ctx, v_lat], axis=1)
    o_lat = from_query_blocks(lax.map(lambda qb: attend(qb, k_all, v_all), to_query_blocks(q_lat)))
    y_lat = finish(o_lat)
    y_ctx = None
    if need_ctx:
        q_ctx = as_q(h_ctx @ w_in[:, DA_KW + DA_VW:])
        y_ctx = finish(attend(q_ctx, k_ctx, v_ctx))
    return y_lat, y_ctx


def retention_chunkwise(q, k, v, log_gamma, state0):
    b, h, t, _ = q.shape
    n = t // RET_CHUNK
    pos = jnp.arange(RET_CHUNK, dtype=jnp.float32)
    diff = pos[:, None] - pos[None, :]
    lower = diff >= 0
    intra = jnp.where(lower[None], jnp.exp(jnp.where(lower, diff, 0.0)[None] * log_gamma[:, None, None]), 0.0)
    q_dec = jnp.exp((pos[None, :] + 1.0) * log_gamma[:, None])
    k_dec = jnp.exp((RET_CHUNK - 1.0 - pos[None, :]) * log_gamma[:, None])
    chunk_dec = jnp.exp(RET_CHUNK * log_gamma)[None, :, None, None]

    def chunks(a):
        return jnp.moveaxis(a.reshape(b, h, n, RET_CHUNK, a.shape[-1]), 2, 0)

    def step(state, blk):
        qb, kb, vb = blk
        s = jnp.einsum('bhid,bhjd->bhij', qb, kb) * intra
        o = jnp.einsum('bhij,bhjv->bhiv', s, vb) + jnp.einsum('bhid,bhdv->bhiv', qb * q_dec[..., None], state)
        state = chunk_dec * state + jnp.einsum('bhjd,bhjv->bhdv', kb * k_dec[..., None], vb)
        return state, o

    _, o = lax.scan(step, state0, (chunks(q), chunks(k), chunks(v)))
    return jnp.moveaxis(o, 0, 2).reshape(b, h, t, -1)


def retention(h_lat, h_ctx, w_in, w_out, decay_logit, need_ctx):
    log_gamma = jax.nn.log_sigmoid(decay_logit.astype(jnp.float32))
    k_scale = RET_KDIM ** -0.5

    def split_heads(a, dim):
        b, t, _ = a.shape
        return a.reshape(b, t, RET_HEADS, dim)

    def to_bhtd(a):
        return jnp.swapaxes(a, 1, 2).astype(jnp.float32)

    def rev(a):
        return jnp.flip(a, axis=2)

    def bidirectional(q, k, v, s_fwd, s_bwd):
        o_f = retention_chunkwise(q, k, v, log_gamma[0], s_fwd)
        o_b = rev(retention_chunkwise(rev(q), rev(k), rev(v), log_gamma[1], s_bwd))
        return o_f + o_b

    def finish(o, g):
        o = rms_norm(jnp.swapaxes(o, 1, 2))
        b, t = o.shape[:2]
        return (jax.nn.silu(g) * o.reshape(b, t, RET_VW).astype(g.dtype)) @ w_out

    p_ctx = h_ctx @ (w_in if need_ctx else w_in[:, :RET_KW + RET_VW])
    k_c = to_bhtd(split_heads(p_ctx[..., :RET_KW], RET_KDIM)) * k_scale
    v_c = to_bhtd(split_heads(p_ctx[..., RET_KW:RET_KW + RET_VW], RET_VDIM))
    pos_c = jnp.arange(k_c.shape[2], dtype=jnp.float32)
    w_fwd = jnp.exp((pos_c[-1] - pos_c)[None, :] * log_gamma[0][:, None])
    w_bwd = jnp.exp(pos_c[None, :] * log_gamma[1][:, None])
    s_fwd = jnp.einsum('bhtd,bhtv->bhdv', k_c * w_fwd[None, :, :, None], v_c)
    s_bwd = jnp.einsum('bhtd,bhtv->bhdv', k_c * w_bwd[None, :, :, None], v_c)

    p_lat = h_lat @ w_in
    ang = rope_angles(jnp.arange(p_lat.shape[1]), RET_KDIM)
    k_l = to_bhtd(apply_rotary(split_heads(p_lat[..., :RET_KW], RET_KDIM), ang)) * k_scale
    v_l = to_bhtd(split_heads(p_lat[..., RET_KW:RET_KW + RET_VW], RET_VDIM))
    q_l = to_bhtd(apply_rotary(split_heads(p_lat[..., RET_KW + RET_VW:2 * RET_KW + RET_VW], RET_KDIM), ang))
    y_lat = finish(bidirectional(q_l, k_l, v_l, s_fwd, s_bwd), p_lat[..., 2 * RET_KW + RET_VW:])
    y_ctx = None
    if need_ctx:
        q_c = to_bhtd(split_heads(p_ctx[..., RET_KW + RET_VW:2 * RET_KW + RET_VW], RET_KDIM))
        zero = jnp.zeros_like(s_fwd)
        y_ctx = finish(bidirectional(q_c, k_c, v_c, zero, zero), p_ctx[..., 2 * RET_KW + RET_VW:])
    return y_lat, y_ctx


def mla(h_lat, h_ctx, w_in, q_norm_g, w_q_up, kv_norm_g, w_kv_up, w_out, rows, cols, need_ctx):
    kvw = MLA_KV_RANK + MLA_ROPE
    scale = (MLA_NOPE + MLA_ROPE) ** -0.5

    def keys_values(p, positioned):
        b, t, _ = p.shape
        c_kv = rms_norm(p[..., :MLA_KV_RANK], kv_norm_g)
        kv = (c_kv @ w_kv_up).reshape(b, t, MLA_HEADS, MLA_NOPE + MLA_V)
        k_rope = p[..., MLA_KV_RANK:kvw].reshape(b, t, 1, MLA_ROPE)
        if positioned:
            k_rope = axial_rope(k_rope, rows, cols)
        k = jnp.concatenate([kv[..., :MLA_NOPE], jnp.broadcast_to(k_rope, (b, t, MLA_HEADS, MLA_ROPE))], axis=-1)
        return k, kv[..., MLA_NOPE:]

    def queries(p, positioned):
        b, t, _ = p.shape
        q = (rms_norm(p, q_norm_g) @ w_q_up).reshape(b, t, MLA_HEADS, MLA_NOPE + MLA_ROPE)
        if positioned:
            q = jnp.concatenate([q[..., :MLA_NOPE], axial_rope(q[..., MLA_NOPE:], rows, cols)], axis=-1)
        return q

    def finish(o):
        b, t = o.shape[:2]
        return o.reshape(b, t, MLA_HEADS * MLA_V) @ w_out

    p_lat = h_lat @ w_in
    k_lat, v_lat = keys_values(p_lat, True)
    q_lat = queries(p_lat[..., kvw:], True)
    p_ctx = h_ctx @ (w_in if need_ctx else w_in[:, :kvw])
    k_ctx, v_ctx = keys_values(p_ctx, False)
    k_all = jnp.concatenate([k_ctx, k_lat], axis=1)
    v_all = jnp.concatenate([v_ctx, v_lat], axis=1)
    o_lat = from_query_blocks(lax.map(lambda qb: softmax_attend(qb, k_all, v_all, scale), to_query_blocks(q_lat)))
    y_lat = finish(o_lat)
    y_ctx = None
    if need_ctx:
        y_ctx = finish(softmax_attend(queries(p_ctx[..., kvw:], False), k_ctx, v_ctx, scale))
    return y_lat, y_ctx


def moe_ffn(h, w_router, b_router, w_gate_up, b_gate_up, w_down, b_down):
    n_tok = h.shape[0]
    logits = (h @ w_router).astype(jnp.float32) + b_router.astype(jnp.float32)
    top_logit, top_idx = lax.top_k(logits, TOP_K)
    gates = jax.nn.softmax(top_logit, axis=-1)
    n_assign = n_tok * TOP_K
    e_flat = top_idx.reshape(-1)
    tok_flat = jnp.arange(n_assign, dtype=jnp.int32) // TOP_K
    order = jnp.argsort(e_flat)
    e_sorted = e_flat[order]
    counts = jnp.bincount(e_flat, length=N_EXPERTS)
    starts = jnp.cumsum(counts) - counts
    padded = (counts + MOE_BLOCK - 1) // MOE_BLOCK * MOE_BLOCK
    pad_end = jnp.cumsum(padded)
    pad_start = pad_end - padded
    slot = pad_start[e_sorted] + jnp.arange(n_assign) - starts[e_sorted]
    n_blocks = n_assign // MOE_BLOCK + N_EXPERTS
    n_slots = n_blocks * MOE_BLOCK
    slot_tok = jnp.zeros((n_slots,), jnp.int32).at[slot].set(tok_flat[order])
    slot_gate = jnp.zeros((n_slots,), jnp.float32).at[slot].set(gates.reshape(-1)[order])
    block_expert = jnp.minimum(jnp.searchsorted(pad_end, jnp.arange(n_blocks) * MOE_BLOCK, side='right'), N_EXPERTS - 1)

    def expert_block(args):
        toks, e = args
        gu = h[toks] @ w_gate_up[e] + b_gate_up[e]
        glu = jnp.minimum(gu[:, :D_EXPERT], SWIGLU_LIMIT)
        lin = jnp.clip(gu[:, D_EXPERT:], -SWIGLU_LIMIT, SWIGLU_LIMIT)
        act = glu * jax.nn.sigmoid(SWIGLU_ALPHA * glu) * (lin + 1.0)
        return act @ w_down[e] + b_down[e]

    y = lax.map(expert_block, (slot_tok.reshape(n_blocks, MOE_BLOCK), block_expert))
    y = y.reshape(n_slots, -1) * slot_gate[:, None].astype(y.dtype)
    return jax.ops.segment_sum(y, slot_tok, num_segments=n_tok)


def setup_inputs(seed: int = 0) -> dict:
    key = jax.random.key(seed)
    ks = jax.random.split(key, 27)
    f32 = jnp.float32
    D = D_MODEL

    def nrm(k, shape, scale):
        return jax.random.normal(k, shape, f32) * scale

    ret_logit0 = jnp.log(2.0 ** (5.0 + jnp.arange(RET_HEADS, dtype=f32)) - 1.0)
    return {
        'x': nrm(ks[0], (BATCH, SEQ, D), 1.0),
        'c': nrm(ks[1], (BATCH, D), 1.0),
        'ctx': nrm(ks[2], (BATCH, CTX_LEN, D), 1.0),
        'c_ctx': nrm(ks[3], (D,), 1.0),
        'ada_w': nrm(ks[4], (DEPTH, D, 6 * D), 0.5 * D ** -0.5),
        'ada_b': nrm(ks[5], (DEPTH, 6 * D), 0.01),
        'norm_g': 1.0 + nrm(ks[6], (DEPTH, 2, D), 0.1),
        'final_g': 1.0 + nrm(ks[7], (D,), 0.1),
        'da_w_in': nrm(ks[8], (N_A, D, DA_IN), D ** -0.5),
        'da_w_out': nrm(ks[9], (N_A, DA_VW, D), DA_VW ** -0.5),
        'da_lambda': nrm(ks[10], (N_A, 4, DA_HEAD_DIM), 0.1),
        'da_subln_g': 1.0 + nrm(ks[11], (N_A, DA_V_DIM), 0.1),
        'ret_w_in': nrm(ks[12], (N_B, D, RET_IN), D ** -0.5),
        'ret_decay_logit': ret_logit0[None, None, :] + nrm(ks[13], (N_B, 2, RET_HEADS), 0.1),
        'ret_w_out': nrm(ks[14], (N_B, RET_VW, D), RET_VW ** -0.5),
        'mla_w_in': nrm(ks[15], (N_C, D, MLA_IN), D ** -0.5),
        'mla_q_norm_g': 1.0 + nrm(ks[16], (N_C, MLA_Q_RANK), 0.1),
        'mla_w_q_up': nrm(ks[17], (N_C, MLA_Q_RANK, MLA_HEADS * (MLA_NOPE + MLA_ROPE)), MLA_Q_RANK ** -0.5),
        'mla_kv_norm_g': 1.0 + nrm(ks[18], (N_C, MLA_KV_RANK), 0.1),
        'mla_w_kv_up': nrm(ks[19], (N_C, MLA_KV_RANK, MLA_HEADS * (MLA_NOPE + MLA_V)), MLA_KV_RANK ** -0.5),
        'mla_w_out': nrm(ks[20], (N_C, MLA_HEADS * MLA_V, D), (MLA_HEADS * MLA_V) ** -0.5),
        'moe_w_router': nrm(ks[21], (DEPTH, D, N_EXPERTS), D ** -0.5),
        'moe_b_router': nrm(ks[22], (DEPTH, N_EXPERTS), 0.01),
        'moe_w_gate_up': nrm(ks[23], (DEPTH, N_EXPERTS, D, 2 * D_EXPERT), D ** -0.5),
        'moe_b_gate_up': nrm(ks[24], (DEPTH, N_EXPERTS, 2 * D_EXPERT), 0.01),
        'moe_w_down': nrm(ks[25], (DEPTH, N_EXPERTS, D_EXPERT, D), D_EXPERT ** -0.5),
        'moe_b_down': nrm(ks[26], (DEPTH, N_EXPERTS, D), 0.01),
    }


def reference(x, c, ctx, c_ctx, ada_w, ada_b, norm_g, final_g, da_w_in, da_w_out, da_lambda, da_subln_g,
              ret_w_in, ret_decay_logit, ret_w_out, mla_w_in, mla_q_norm_g, mla_w_q_up, mla_kv_norm_g,
              mla_w_kv_up, mla_w_out, moe_w_router, moe_b_router, moe_w_gate_up, moe_b_gate_up,
              moe_w_down, moe_b_down):
    b, seq, _ = x.shape
    n_rows = seq // GRID_W
    rows = jnp.repeat(jnp.arange(n_rows), GRID_W)
    cols = jnp.tile(jnp.arange(GRID_W), n_rows)
    x_ctx = ctx
    for i in range(DEPTH):
        last = i == DEPTH - 1
        kind = i % N_MIXERS
        j = i // N_MIXERS
        mod_lat = [m[:, None, :] for m in adaln(c, ada_w[i], ada_b[i], 6)]
        mod_ctx = adaln(c_ctx, ada_w[i], ada_b[i], 2 if last else 6)
        h_lat = modulate(rms_norm(x, norm_g[i, 0]), mod_lat[0], mod_lat[1])
        h_ctx = modulate(rms_norm(x_ctx, norm_g[i, 0]), mod_ctx[0], mod_ctx[1])
        if kind == 0:
            lam_init = 0.8 - 0.6 * math.exp(-0.3 * i)
            y_lat, y_ctx = diff_attention(h_lat, h_ctx, da_w_in[j], da_w_out[j], da_lambda[j], da_subln_g[j],
                                          lam_init, rows, cols, not last)
        elif kind == 1:
            y_lat, y_ctx = retention(h_lat, h_ctx, ret_w_in[j], ret_w_out[j], ret_decay_logit[j], not last)
        else:
            y_lat, y_ctx = mla(h_lat, h_ctx, mla_w_in[j], mla_q_norm_g[j], mla_w_q_up[j], mla_kv_norm_g[j],
                               mla_w_kv_up[j], mla_w_out[j], rows, cols, not last)
        x = x + mod_lat[2] * y_lat
        h_lat = modulate(rms_norm(x, norm_g[i, 1]), mod_lat[3], mod_lat[4]).reshape(b * seq, D_MODEL)
        if last:
            f_lat = moe_ffn(h_lat, moe_w_router[i], moe_b_router[i], moe_w_gate_up[i], moe_b_gate_up[i],
                            moe_w_down[i], moe_b_down[i])
        else:
            x_ctx = x_ctx + mod_ctx[2] * y_ctx
            h_ctx = modulate(rms_norm(x_ctx, norm_g[i, 1]), mod_ctx[3], mod_ctx[4])
            n_ctx = b * h_ctx.shape[1]
            f = moe_ffn(jnp.concatenate([h_ctx.reshape(n_ctx, D_MODEL), h_lat], axis=0), moe_w_router[i],
                        moe_b_router[i], moe_w_gate_up[i], moe_b_gate_up[i], moe_w_down[i], moe_b_down[i])
            x_ctx = x_ctx + mod_ctx[5] * f[:n_ctx].reshape(x_ctx.shape)
            f_lat = f[n_ctx:]
        x = x + mod_lat[5] * f_lat.reshape(b, seq, D_MODEL)
    return rms_norm(x, final_g)
```

```python
import functools
import math

import jax
import jax.numpy as jnp
from jax import lax
from jax.experimental import pallas as pl
from jax.experimental.pallas import tpu as pltpu

F32 = jnp.float32
BF16 = jnp.bfloat16

D_MODEL = 1024
GRID_W = 64
RMS_EPS = 1e-6
ROPE_BASE = 10000.0
N_MIXERS = 3

DA_HEADS = 8
DA_HEAD_DIM = 64
DA_V_DIM = 128

RET_HEADS = 4
RET_KDIM = 256
RET_VDIM = 512
RET_CHUNK = 128

MLA_HEADS = 8
MLA_NOPE = 128
MLA_ROPE = 64
MLA_V = 128
MLA_Q_RANK = 256
MLA_KV_RANK = 128

N_EXPERTS = 32
TOP_K = 4
D_EXPERT = 1024
SWIGLU_LIMIT = 7.0
SWIGLU_ALPHA = 1.702

LANES = 128
MOD_ROWS = 16
MOE_BLOCK = 512
COL_CHUNK = 1024
VMEM_LIMIT = 56 * 1024 * 1024
NEG_BIG = -1e30


def _params(n_axes):
    return pltpu.CompilerParams(dimension_semantics=("arbitrary",) * n_axes,
                                vmem_limit_bytes=VMEM_LIMIT)


def _pick_tile(cap, *dims):
    t = cap
    while any(d % t for d in dims):
        t //= 2
    return t


def _rms(x):
    return x * lax.rsqrt(jnp.mean(x * x, axis=-1, keepdims=True) + RMS_EPS)


def _norm_mod(x, g, shift, scale):
    return (_rms(x) * g) * (1.0 + scale) + shift


def _dot(a, b):
    return jnp.dot(a, b, preferred_element_type=F32)


def _dot_nt(a, b):
    return lax.dot_general(a, b, (((1,), (1,)), ((), ())), preferred_element_type=F32)


def _dot_tn(a, b):
    return lax.dot_general(a, b, (((0,), (0,)), ((), ())), preferred_element_type=F32)


def _adaln_kernel(c_ref, w_ref, b_ref, o_ref):
    c = c_ref[...]
    a = c * (1.0 / (1.0 + jnp.exp(-c)))
    o_ref[...] = jnp.dot(a, w_ref[...], precision=lax.Precision.HIGHEST,
                         preferred_element_type=F32) + b_ref[...]


def _adaln(cond, ada_w, ada_b):
    depth, d, n = ada_w.shape
    tn = COL_CHUNK
    return pl.pallas_call(
        _adaln_kernel,
        grid=(depth, n // tn),
        in_specs=[pl.BlockSpec((MOD_ROWS, d), lambda l, j: (0, 0)),
                  pl.BlockSpec((None, d, tn), lambda l, j: (l, 0, j)),
                  pl.BlockSpec((None, 1, tn), lambda l, j: (l, 0, j))],
        out_specs=pl.BlockSpec((None, MOD_ROWS, tn), lambda l, j: (l, 0, j)),
        out_shape=jax.ShapeDtypeStruct((depth, MOD_ROWS, n), F32),
        compiler_params=_params(2),
        name="adaln",
    )(cond, ada_w, ada_b.reshape(depth, 1, n))


def _rope_pairs16(x, cos, sin):
    lane = lax.broadcasted_iota(jnp.int32, x.shape, 1)
    first = (lane % 32) < 16
    partner = jnp.where(first, pltpu.roll(x, LANES - 16, 1), pltpu.roll(x, 16, 1))
    return x * cos + partner * sin


def _proj_kernel(x_ref, g_ref, mod_ref, w_ref, cos_ref, sin_ref, o_ref, h_scr, *, kinds, ret_scale):
    j = pl.program_id(1)

    @pl.when(j == 0)
    def _():
        h = _norm_mod(x_ref[...], g_ref[...], mod_ref[0:1, :], mod_ref[1:2, :])
        h_scr[...] = h.astype(BF16)

    def chunks_of(kind):
        cs = [c for c, k in enumerate(kinds) if k == kind]
        pred = None
        for c in cs:
            pred = (j == c) if pred is None else jnp.logical_or(pred, j == c)
        return pred

    def acc():
        return _dot(h_scr[...], w_ref[...])

    if "plain" in kinds:
        @pl.when(chunks_of("plain"))
        def _():
            o_ref[...] = acc().astype(BF16)

    if "rope16" in kinds:
        @pl.when(chunks_of("rope16"))
        def _():
            a = acc()
            cos = cos_ref[...]
            sin = sin_ref[...]
            for g in range(COL_CHUNK // LANES):
                sl = slice(g * LANES, (g + 1) * LANES)
                o_ref[:, sl] = _rope_pairs16(a[:, sl], cos, sin).astype(BF16)

    for kind, scale in (("ret_k", ret_scale), ("ret_q", 1.0)):
        if kind in kinds:
            @pl.when(chunks_of(kind))
            def _(scale=scale):
                a = acc()
                cos = cos_ref[...]
                sin = sin_ref[...]
                for hh in range(COL_CHUNK // RET_KDIM):
                    lo = slice(hh * RET_KDIM, hh * RET_KDIM + LANES)
                    hi = slice(hh * RET_KDIM + LANES, (hh + 1) * RET_KDIM)
                    x1 = a[:, lo]
                    x2 = a[:, hi]
                    o_ref[:, lo] = ((x1 * cos - x2 * sin) * scale).astype(BF16)
                    o_ref[:, hi] = ((x1 * sin + x2 * cos) * scale).astype(BF16)


def _proj(xs, g, mod_l, w, cos_t, sin_t, kinds, geo, ret_scale=1.0):
    nt, d = xs.shape
    n = w.shape[1]
    tm, per, nlt, nb = geo["tm"], geo["per"], geo["nlt"], geo["b"]
    return pl.pallas_call(
        functools.partial(_proj_kernel, kinds=tuple(kinds), ret_scale=ret_scale),
        grid=(nt // tm, n // COL_CHUNK),
        in_specs=[pl.BlockSpec((tm, d), lambda i, j: (i, 0)),
                  pl.BlockSpec((1, d), lambda i, j: (0, 0)),
                  pl.BlockSpec((None, 6, d), lambda i, j: (jnp.minimum(i // per, nb), 0, 0)),
                  pl.BlockSpec((d, COL_CHUNK), lambda i, j: (0, j)),
                  pl.BlockSpec((tm, LANES), lambda i, j: (jnp.where(i < nlt, i % per, per), 0)),
                  pl.BlockSpec((tm, LANES), lambda i, j: (jnp.where(i < nlt, i % per, per), 0))],
        out_specs=pl.BlockSpec((tm, COL_CHUNK), lambda i, j: (i, j)),
        out_shape=jax.ShapeDtypeStruct((nt, n), BF16),
        scratch_shapes=[pltpu.VMEM((tm, d), BF16)],
        compiler_params=_params(2),
        name="proj",
    )(xs, g, mod_l, w, cos_t, sin_t)


def _mla_proj_kernel(x_ref, g_ref, mod_ref, win_ref, qg_ref, kvg_ref, wq_ref, wkv_ref, cos_ref, sin_ref,
                     q_ref, kn_ref, v_ref, kr_ref):
    h = _norm_mod(x_ref[...], g_ref[...], mod_ref[0:1, :], mod_ref[1:2, :]).astype(BF16)
    p = _dot(h, win_ref[...])
    cos = cos_ref[...]
    sin = sin_ref[...]
    ckv = (_rms(p[:, :MLA_KV_RANK]) * kvg_ref[...]).astype(BF16)
    kv = _dot(ckv, wkv_ref[...])
    nk = MLA_HEADS * MLA_NOPE
    kn_ref[...] = kv[:, :nk].astype(BF16)
    v_ref[...] = kv[:, nk:].astype(BF16)
    kr_ref[...] = _rope_pairs16(p[:, MLA_KV_RANK + MLA_Q_RANK:], cos, sin).astype(BF16)
    cq = (_rms(p[:, MLA_KV_RANK:MLA_KV_RANK + MLA_Q_RANK]) * qg_ref[...]).astype(BF16)
    q = _dot(cq, wq_ref[...])
    for hh in range(MLA_HEADS):
        lo = slice(hh * 2 * LANES, hh * 2 * LANES + LANES)
        hi = slice(hh * 2 * LANES + LANES, (hh + 1) * 2 * LANES)
        q_ref[:, lo] = q[:, lo].astype(BF16)
        q_ref[:, hi] = _rope_pairs16(q[:, hi], cos, sin).astype(BF16)


def _mla_proj(xs, g, mod_l, w_in, qg, kvg, wq, wkv, cos_t, sin_t, geo):
    nt, d = xs.shape
    tm, per, nlt, nb = geo["tm"], geo["per"], geo["nlt"], geo["b"]
    full = lambda a: pl.BlockSpec(a.shape, lambda i: (0,) * a.ndim)
    tab = pl.BlockSpec((tm, LANES), lambda i: (jnp.where(i < nlt, i % per, per), 0))
    nq = MLA_HEADS * 2 * LANES
    nk = MLA_HEADS * MLA_NOPE
    return pl.pallas_call(
        _mla_proj_kernel,
        grid=(nt // tm,),
        in_specs=[pl.BlockSpec((tm, d), lambda i: (i, 0)), full(g),
                  pl.BlockSpec((None, 6, d), lambda i: (jnp.minimum(i // per, nb), 0, 0)),
                  full(w_in), full(qg), full(kvg), full(wq), full(wkv), tab, tab],
        out_specs=[pl.BlockSpec((tm, nq), lambda i: (i, 0)),
                   pl.BlockSpec((tm, nk), lambda i: (i, 0)),
                   pl.BlockSpec((tm, nk), lambda i: (i, 0)),
                   pl.BlockSpec((tm, LANES), lambda i: (i, 0))],
        out_shape=[jax.ShapeDtypeStruct((nt, nq), BF16), jax.ShapeDtypeStruct((nt, nk), BF16),
                   jax.ShapeDtypeStruct((nt, nk), BF16), jax.ShapeDtypeStruct((nt, LANES), BF16)],
        compiler_params=_params(1),
        name="mla_proj",
    )(xs, g, mod_l, w_in, qg, kvg, wq, wkv, cos_t, sin_t)


def _flash_init(m_ref, l_ref, a_ref):
    m_ref[...] = jnp.full(m_ref.shape, -jnp.inf, F32)
    l_ref[...] = jnp.zeros(l_ref.shape, F32)
    a_ref[...] = jnp.zeros(a_ref.shape, F32)


def _flash_update(s, v, m_ref, l_ref, a_ref):
    m_prev = m_ref[...]
    m_new = jnp.maximum(m_prev, jnp.max(s, axis=1, keepdims=True))
    alpha = jnp.exp(m_prev - m_new)
    p = jnp.exp(s - m_new)
    l_ref[...] = alpha * l_ref[...] + jnp.sum(p, axis=1, keepdims=True)
    a_ref[...] = alpha * a_ref[...] + _dot(p.astype(BF16), v)
    m_ref[...] = m_new


def _da_attn_kernel(*refs, n_chunks, tk, lam_init, scale):
    if n_chunks:
        (lam_ref, sg_ref, q_ref, kc_ref, vc_ref, kl_ref, vl_ref, o_ref, m0, l0, a0, m1, l1, a1) = refs
    else:
        (lam_ref, sg_ref, q_ref, kc_ref, vc_ref, o_ref, m0, l0, a0, m1, l1, a1) = refs
    q = q_ref[...]
    lane = lax.broadcasted_iota(jnp.int32, q.shape, 1)
    zero = jnp.zeros_like(q)
    q_lo = jnp.where(lane < DA_HEAD_DIM, q, zero)
    q_hi = jnp.where(lane >= DA_HEAD_DIM, q, zero)
    _flash_init(m0, l0, a0)
    _flash_init(m1, l1, a1)

    def step(k, v):
        _flash_update(_dot_nt(q_lo, k) * scale, v, m0, l0, a0)
        _flash_update(_dot_nt(q_hi, k) * scale, v, m1, l1, a1)

    step(kc_ref[...], vc_ref[...])
    if n_chunks:
        def body(j, carry):
            r = pl.multiple_of(j * tk, tk)
            step(kl_ref[pl.ds(r, tk), :], vl_ref[pl.ds(r, tk), :])
            return carry
        lax.fori_loop(0, n_chunks, body, 0)

    lf = lam_ref[...]
    lam = (jnp.exp(jnp.sum(lf[0:1] * lf[1:2], axis=1, keepdims=True))
           - jnp.exp(jnp.sum(lf[2:3] * lf[3:4], axis=1, keepdims=True)) + lam_init)
    o = a0[...] / l0[...] - lam * (a1[...] / l1[...])
    o = _rms(o) * sg_ref[...] * (1.0 - lam_init)
    o_ref[...] = o.astype(BF16)


def _attn_scratch(tq, vdim, n_maps):
    s = []
    for _ in range(n_maps):
        s += [pltpu.VMEM((tq, 1), F32), pltpu.VMEM((tq, 1), F32), pltpu.VMEM((tq, vdim), F32)]
    return s


def _da_attention(p, lam_vecs, subg, lam_init, geo):
    nt = p.shape[0]
    nb, seq, ctx, nlat = geo["b"], geo["seq"], geo["ctx"], geo["nlat"]
    hh = DA_HEADS
    tq = _pick_tile(512, seq)
    tk = _pick_tile(512, seq)
    nqt = seq // tq
    cb = nlat // ctx
    scale = DA_HEAD_DIM ** -0.5
    small = [pl.BlockSpec(lam_vecs.shape, lambda *a: (0, 0)), pl.BlockSpec(subg.shape, lambda *a: (0, 0))]
    o_lat = pl.pallas_call(
        functools.partial(_da_attn_kernel, n_chunks=seq // tk, tk=tk, lam_init=lam_init, scale=scale),
        grid=(nb, hh, nqt),
        in_specs=small + [
            pl.BlockSpec((tq, LANES), lambda b, h, i: (b * nqt + i, 2 * hh + h)),
            pl.BlockSpec((ctx, LANES), lambda b, h, i: (cb + b, h)),
            pl.BlockSpec((ctx, LANES), lambda b, h, i: (cb + b, hh + h)),
            pl.BlockSpec((seq, LANES), lambda b, h, i: (b, h)),
            pl.BlockSpec((seq, LANES), lambda b, h, i: (b, hh + h))],
        out_specs=pl.BlockSpec((tq, LANES), lambda b, h, i: (b * nqt + i, h)),
        out_shape=jax.ShapeDtypeStruct((nlat, hh * DA_V_DIM), BF16),
        scratch_shapes=_attn_scratch(tq, DA_V_DIM, 2),
        compiler_params=_params(3),
        name="da_attn_lat",
    )(lam_vecs, subg, p, p, p, p, p)
    o_ctx = pl.pallas_call(
        functools.partial(_da_attn_kernel, n_chunks=0, tk=tk, lam_init=lam_init, scale=scale),
        grid=(nb, hh),
        in_specs=small + [
            pl.BlockSpec((ctx, LANES), lambda b, h: (cb + b, 2 * hh + h)),
            pl.BlockSpec((ctx, LANES), lambda b, h: (cb + b, h)),
            pl.BlockSpec((ctx, LANES), lambda b, h: (cb + b, hh + h))],
        out_specs=pl.BlockSpec((ctx, LANES), lambda b, h: (b, h)),
        out_shape=jax.ShapeDtypeStruct((nt - nlat, hh * DA_V_DIM), BF16),
        scratch_shapes=_attn_scratch(ctx, DA_V_DIM, 2),
        compiler_params=_params(2),
        name="da_attn_ctx",
    )(lam_vecs, subg, p, p, p)
    return o_lat, o_ctx


def _mla_attn_kernel(*refs, n_chunks, tk, scale):
    if n_chunks:
        (q_ref, knc_ref, krc_ref, vc_ref, knl_ref, krl_ref, vl_ref, o_ref, m0, l0, a0) = refs
    else:
        (q_ref, knc_ref, krc_ref, vc_ref, o_ref, m0, l0, a0) = refs
    q = q_ref[...]
    _flash_init(m0, l0, a0)

    def step(kn, kr, v):
        k = jnp.concatenate([kn, kr], axis=1)
        _flash_update(_dot_nt(q, k) * scale, v, m0, l0, a0)

    step(knc_ref[...], krc_ref[...], vc_ref[...])
    if n_chunks:
        def body(j, carry):
            r = pl.multiple_of(j * tk, tk)
            step(knl_ref[pl.ds(r, tk), :], krl_ref[pl.ds(r, tk), :], vl_ref[pl.ds(r, tk), :])
            return carry
        lax.fori_loop(0, n_chunks, body, 0)
    o_ref[...] = (a0[...] / l0[...]).astype(BF16)


def _mla_attention(q, kn, v, kr, geo):
    nt = q.shape[0]
    nb, seq, ctx, nlat = geo["b"], geo["seq"], geo["ctx"], geo["nlat"]
    hh = MLA_HEADS
    tq = _pick_tile(512, seq)
    tk = _pick_tile(512, seq)
    nqt = seq // tq
    cb = nlat // ctx
    scale = (MLA_NOPE + MLA_ROPE) ** -0.5
    o_lat = pl.pallas_call(
        functools.partial(_mla_attn_kernel, n_chunks=seq // tk, tk=tk, scale=scale),
        grid=(nb, hh, nqt),
        in_specs=[
            pl.BlockSpec((tq, 2 * LANES), lambda b, h, i: (b * nqt + i, h)),
            pl.BlockSpec((ctx, LANES), lambda b, h, i: (cb + b, h)),
            pl.BlockSpec((ctx, LANES), lambda b, h, i: (cb + b, 0)),
            pl.BlockSpec((ctx, LANES), lambda b, h, i: (cb + b, h)),
            pl.BlockSpec((seq, LANES), lambda b, h, i: (b, h)),
            pl.BlockSpec((seq, LANES), lambda b, h, i: (b, 0)),
            pl.BlockSpec((seq, LANES), lambda b, h, i: (b, h))],
        out_specs=pl.BlockSpec((tq, LANES), lambda b, h, i: (b * nqt + i, h)),
        out_shape=jax.ShapeDtypeStruct((nlat, hh * MLA_V), BF16),
        scratch_shapes=_attn_scratch(tq, MLA_V, 1),
        compiler_params=_params(3),
        name="mla_attn_lat",
    )(q, kn, kr, v, kn, kr, v)
    o_ctx = pl.pallas_call(
        functools.partial(_mla_attn_kernel, n_chunks=0, tk=tk, scale=scale),
        grid=(nb, hh),
        in_specs=[
            pl.BlockSpec((ctx, 2 * LANES), lambda b, h: (cb + b, h)),
            pl.BlockSpec((ctx, LANES), lambda b, h: (cb + b, h)),
            pl.BlockSpec((ctx, LANES), lambda b, h: (cb + b, 0)),
            pl.BlockSpec((ctx, LANES), lambda b, h: (cb + b, h))],
        out_specs=pl.BlockSpec((ctx, LANES), lambda b, h: (b, h)),
        out_shape=jax.ShapeDtypeStruct((nt - nlat, hh * MLA_V), BF16),
        scratch_shapes=_attn_scratch(ctx, MLA_V, 1),
        compiler_params=_params(2),
        name="mla_attn_ctx",
    )(q, kn, kr, v)
    return o_lat, o_ctx


def _ret_kernel(*refs, reverse):
    if reverse:
        dl_ref, q_ref, k_ref, v_ref, of_ref, g_ref, o_ref, st = refs
    else:
        dl_ref, q_ref, k_ref, v_ref, o_ref, st = refs
    hd = pl.program_id(1)
    s = pl.program_id(2)

    @pl.when(s == 0)
    def _():
        st[...] = jnp.zeros(st.shape, F32)

    dl = dl_ref[...]
    lsig = jnp.minimum(dl, 0.0) - jnp.log(1.0 + jnp.exp(-jnp.abs(dl)))
    row = lax.broadcasted_iota(jnp.int32, dl.shape, 0)
    col = lax.broadcasted_iota(jnp.int32, dl.shape, 1)
    pick = jnp.logical_and(row == (1 if reverse else 0), col == hd)
    lg = jnp.sum(jnp.sum(jnp.where(pick, lsig, 0.0), axis=1, keepdims=True), axis=0, keepdims=True)

    c = RET_CHUNK
    i2 = lax.broadcasted_iota(jnp.int32, (c, c), 0)
    j2 = lax.broadcasted_iota(jnp.int32, (c, c), 1)
    pos = lax.broadcasted_iota(jnp.int32, (c, 1), 0).astype(F32)
    if reverse:
        dist = (j2 - i2).astype(F32)
        q_dec = jnp.exp((c - pos) * lg)
        k_dec = jnp.exp(pos * lg)
    else:
        dist = (i2 - j2).astype(F32)
        q_dec = jnp.exp((pos + 1.0) * lg)
        k_dec = jnp.exp((c - 1.0 - pos) * lg)
    keep = dist >= 0.0
    intra = jnp.where(keep, jnp.exp(jnp.where(keep, dist, 0.0) * lg), 0.0)
    chunk_dec = jnp.exp(c * lg)

    qb = q_ref[...]
    kb = k_ref[...]
    vb = v_ref[...]
    state = st[...]
    sc = _dot_nt(qb, kb) * intra
    o = _dot(sc.astype(BF16), vb) + _dot((qb.astype(F32) * q_dec).astype(BF16), state.astype(BF16))
    st[...] = chunk_dec * state + _dot_tn((kb.astype(F32) * k_dec).astype(BF16), vb)
    if reverse:
        tot = _rms(of_ref[...] + o)
        g = g_ref[...].astype(F32)
        o_ref[...] = (g * (1.0 / (1.0 + jnp.exp(-g))) * tot).astype(BF16)
    else:
        o_ref[...] = o


def _retention(p, decay_logit, geo):
    nt = p.shape[0]
    nb, seq, ctx, nlat = geo["b"], geo["seq"], geo["ctx"], geo["nlat"]
    c = RET_CHUNK
    ncc, ncl = ctx // c, seq // c
    hh = RET_HEADS
    dl = jnp.zeros((8, LANES), F32).at[:2, :hh].set(decay_logit.astype(F32))

    def rows_fwd(b, s):
        return jnp.where(s < ncc, nlat // c + b * ncc + s, b * ncl + (s - ncc))

    def rows_bwd(b, s):
        return jnp.where(s < ncc, nlat // c + b * ncc + (ncc - 1 - s), b * ncl + (ncl - 1 - (s - ncc)))

    kq = RET_KDIM
    kv = RET_VDIM
    q_off = (hh * kq + hh * kv) // kq
    v_off = (hh * kq) // kv
    g_off = (2 * hh * kq + hh * kv) // kv

    def specs(rows):
        return [pl.BlockSpec((8, LANES), lambda b, h, s: (0, 0)),
                pl.BlockSpec((c, kq), lambda b, h, s: (rows(b, s), q_off + h)),
                pl.BlockSpec((c, kq), lambda b, h, s: (rows(b, s), h)),
                pl.BlockSpec((c, kv), lambda b, h, s: (rows(b, s), v_off + h))]

    grid = (nb, hh, ncc + ncl)
    o_f = pl.pallas_call(
        functools.partial(_ret_kernel, reverse=False),
        grid=grid,
        in_specs=specs(rows_fwd),
        out_specs=pl.BlockSpec((c, kv), lambda b, h, s: (rows_fwd(b, s), h)),
        out_shape=jax.ShapeDtypeStruct((nt, hh * kv), F32),
        scratch_shapes=[pltpu.VMEM((kq, kv), F32)],
        compiler_params=_params(3),
        name="ret_fwd",
    )(dl, p, p, p)
    return pl.pallas_call(
        functools.partial(_ret_kernel, reverse=True),
        grid=grid,
        in_specs=specs(rows_bwd) + [
            pl.BlockSpec((c, kv), lambda b, h, s: (rows_bwd(b, s), h)),
            pl.BlockSpec((c, kv), lambda b, h, s: (rows_bwd(b, s), g_off + h))],
        out_specs=pl.BlockSpec((c, kv), lambda b, h, s: (rows_bwd(b, s), h)),
        out_shape=jax.ShapeDtypeStruct((nt, hh * kv), BF16),
        scratch_shapes=[pltpu.VMEM((kq, kv), F32)],
        compiler_params=_params(3),
        name="ret_bwd",
    )(dl, p, p, p, o_f, p)


def _post_kernel(ol_ref, oc_ref, w_ref, x_ref, mod_ref, g2_ref, wr_ref, br_ref, tri_ref,
                 xo_ref, h2_ref, idx_ref, gate_ref, rank_ref, cnt_ref, cnt_scr, *, nlt):
    t = pl.program_id(0)

    @pl.when(t == 0)
    def _():
        cnt_scr[...] = jnp.zeros(cnt_scr.shape, F32)

    o = jnp.where(t < nlt, ol_ref[...], oc_ref[...])
    y = _dot(o, w_ref[...])
    x = x_ref[...] + mod_ref[2:3, :] * y
    xo_ref[...] = x
    h2 = _norm_mod(x, g2_ref[...], mod_ref[3:4, :], mod_ref[4:5, :])
    h2_ref[...] = h2
    logits = jnp.dot(h2, wr_ref[...], precision=lax.Precision.HIGHEST,
                     preferred_element_type=F32) + br_ref[...]
    lane = lax.broadcasted_iota(jnp.int32, logits.shape, 1).astype(F32)
    work = logits
    vals, idxs = [], []
    for _ in range(TOP_K):
        mx = jnp.max(work, axis=1, keepdims=True)
        ix = jnp.min(jnp.where(work == mx, lane, float(LANES)), axis=1, keepdims=True)
        vals.append(mx)
        idxs.append(ix)
        work = jnp.where(lane == ix, -jnp.inf, work)
    es = [jnp.exp(v - vals[0]) for v in vals]
    den = es[0] + es[1] + es[2] + es[3]
    onehots = [lane == ix for ix in idxs]
    oh = jnp.zeros(logits.shape, F32)
    for m in onehots:
        oh = oh + jnp.where(m, 1.0, 0.0)
    before = _dot(tri_ref[...], oh.astype(BF16)) + cnt_scr[0:1, :]
    idx_out = jnp.zeros(logits.shape, F32)
    gate_out = jnp.zeros(logits.shape, F32)
    rank_out = jnp.zeros(logits.shape, F32)
    for r in range(TOP_K):
        sel = lane == float(r)
        rk = jnp.sum(jnp.where(onehots[r], before, 0.0), axis=1, keepdims=True)
        idx_out = jnp.where(sel, idxs[r], idx_out)
        gate_out = jnp.where(sel, es[r] / den, gate_out)
        rank_out = jnp.where(sel, rk, rank_out)
    idx_ref[...] = idx_out.astype(jnp.int32)
    gate_ref[...] = gate_out
    rank_ref[...] = rank_out.astype(jnp.int32)
    cnt_scr[0:1, :] = cnt_scr[0:1, :] + jnp.sum(oh, axis=0, keepdims=True)
    cnt_ref[...] = cnt_scr[...]


def _post(o_lat, o_ctx, ctx_off, w_out, xs, mod_l, g2, w_router, b_router, geo):
    nt, d = xs.shape
    ko = o_lat.shape[1]
    tm, per, nb, nlt = geo["tm"], geo["per"], geo["b"], geo["nlt"]
    ii = lax.broadcasted_iota(jnp.int32, (tm, tm), 0)
    jj = lax.broadcasted_iota(jnp.int32, (tm, tm), 1)
    tri = (jj < ii).astype(BF16)
    wr = jnp.zeros((d, LANES), F32).at[:, :N_EXPERTS].set(w_router.astype(F32))
    br = jnp.full((1, LANES), NEG_BIG, F32).at[0, :N_EXPERTS].set(b_router.astype(F32))
    row = lambda w: pl.BlockSpec((tm, w), lambda i: (i, 0))
    full = lambda a: pl.BlockSpec(a.shape, lambda i: (0,) * a.ndim)
    return pl.pallas_call(
        functools.partial(_post_kernel, nlt=nlt),
        grid=(nt // tm,),
        in_specs=[pl.BlockSpec((tm, ko), lambda i: (jnp.minimum(i, nlt - 1), 0)),
                  pl.BlockSpec((tm, ko), lambda i: (jnp.maximum(i - nlt, 0) + ctx_off, 0)),
                  full(w_out), row(d),
                  pl.BlockSpec((None, 6, d), lambda i: (jnp.minimum(i // per, nb), 0, 0)),
                  full(g2), full(wr), full(br), full(tri)],
        out_specs=[row(d), row(d), row(LANES), row(LANES), row(LANES),
                   pl.BlockSpec((8, LANES), lambda i: (0, 0))],
        out_shape=[jax.ShapeDtypeStruct((nt, d), F32), jax.ShapeDtypeStruct((nt, d), F32),
                   jax.ShapeDtypeStruct((nt, LANES), jnp.int32), jax.ShapeDtypeStruct((nt, LANES), F32),
                   jax.ShapeDtypeStruct((nt, LANES), jnp.int32), jax.ShapeDtypeStruct((8, LANES), F32)],
        scratch_shapes=[pltpu.VMEM((8, LANES), F32)],
        input_output_aliases={3: 0},
        compiler_params=_params(1),
        name="post",
    )(o_lat, o_ctx, w_out, xs, mod_l, g2, wr, br, tri)


def _row_copy(src, dst, sem):
    return pltpu.make_async_copy(src, dst, sem)


def _dispatch_kernel(zs_ref, zc_ref, slot_ref, h_ref, xs_ref, zero_scr, sem, *, tm):
    t = pl.program_id(0)

    @pl.when(t == 0)
    def _():
        zero_scr[...] = jnp.zeros(zero_scr.shape, F32)

        def per_expert(e, carry):
            start = zs_ref[e]
            n = zc_ref[e]

            def issue(r, c2):
                _row_copy(zero_scr.at[pl.ds(0, 1)], xs_ref.at[pl.ds(start + r, 1)], sem).start()
                return c2

            lax.fori_loop(0, n, issue, 0)

            def drain(r, c2):
                _row_copy(zero_scr.at[pl.ds(0, 1)], xs_ref.at[pl.ds(0, 1)], sem).wait()
                return c2

            lax.fori_loop(0, n, drain, 0)
            return carry

        lax.fori_loop(0, N_EXPERTS, per_expert, 0)

    def issue(r, carry):
        for k in range(TOP_K):
            s = slot_ref[r * TOP_K + k]
            _row_copy(h_ref.at[pl.ds(r, 1)], xs_ref.at[pl.ds(s, 1)], sem).start()
        return carry

    lax.fori_loop(0, tm, issue, 0)

    def drain(r, carry):
        for k in range(TOP_K):
            _row_copy(h_ref.at[pl.ds(0, 1)], xs_ref.at[pl.ds(0, 1)], sem).wait()
        return carry

    lax.fori_loop(0, tm, drain, 0)


def _dispatch(h2, slot_flat, zstart, zcount, n_slots, geo):
    nt, d = h2.shape
    tm = geo["tm"]
    return pl.pallas_call(
        functools.partial(_dispatch_kernel, tm=tm),
        grid_spec=pltpu.PrefetchScalarGridSpec(
            num_scalar_prefetch=2,
            grid=(nt // tm,),
            in_specs=[pl.BlockSpec((tm * TOP_K,), lambda i, zs, zc: (i,), memory_space=pltpu.SMEM),
                      pl.BlockSpec((tm, d), lambda i, zs, zc: (i, 0))],
            out_specs=pl.BlockSpec(memory_space=pl.ANY),
            scratch_shapes=[pltpu.VMEM((8, d), F32), pltpu.SemaphoreType.DMA(())]),
        out_shape=jax.ShapeDtypeStruct((n_slots, d), F32),
        compiler_params=_params(1),
        name="moe_dispatch",
    )(zstart, zcount, slot_flat, h2)


def _expert_kernel(be_ref, nu_ref, x_ref, wgu_ref, bgu_ref, wd_ref, bd_ref, y_ref):
    i = pl.program_id(0)

    @pl.when(i < nu_ref[0])
    def _():
        x = x_ref[...].astype(BF16)
        gu = _dot(x, wgu_ref[...]) + bgu_ref[...]
        glu = jnp.minimum(gu[:, :D_EXPERT], SWIGLU_LIMIT)
        lin = jnp.clip(gu[:, D_EXPERT:], -SWIGLU_LIMIT, SWIGLU_LIMIT)
        act = glu * (1.0 / (1.0 + jnp.exp(-SWIGLU_ALPHA * glu))) * (lin + 1.0)
        y_ref[...] = _dot(act.astype(BF16), wd_ref[...]) + bd_ref[...]


def _experts(xs, block_expert, n_used, wgu, bgu, wd, bd):
    ns, d = xs.shape
    nblk = ns // MOE_BLOCK
    ne, _, f2 = wgu.shape
    f = wd.shape[1]
    blk = lambda i, be, nu: (jnp.minimum(i, nu[0] - 1), 0)
    return pl.pallas_call(
        _expert_kernel,
        grid_spec=pltpu.PrefetchScalarGridSpec(
            num_scalar_prefetch=2,
            grid=(nblk,),
            in_specs=[pl.BlockSpec((MOE_BLOCK, d), blk),
                      pl.BlockSpec((None, d, f2), lambda i, be, nu: (be[i], 0, 0)),
                      pl.BlockSpec((None, 1, f2), lambda i, be, nu: (be[i], 0, 0)),
                      pl.BlockSpec((None, f, d), lambda i, be, nu: (be[i], 0, 0)),
                      pl.BlockSpec((None, 1, d), lambda i, be, nu: (be[i], 0, 0))],
            out_specs=pl.BlockSpec((MOE_BLOCK, d), blk)),
        out_shape=jax.ShapeDtypeStruct((ns, d), F32),
        compiler_params=_params(1),
        name="moe_experts",
    )(block_expert, n_used, xs, wgu, bgu.reshape(ne, 1, f2), wd, bd.reshape(ne, 1, d))


def _combine_kernel(slot_ref, gate_ref, x_ref, mod_ref, y_ref, xo_ref, ybuf, sem, *, tm):
    def issue(r, carry):
        for k in range(TOP_K):
            s = slot_ref[r * TOP_K + k]
            _row_copy(y_ref.at[pl.ds(s, 1)], ybuf.at[k, pl.ds(r, 1)], sem).start()
        return carry

    lax.fori_loop(0, tm, issue, 0)

    def drain(r, carry):
        for k in range(TOP_K):
            _row_copy(y_ref.at[pl.ds(0, 1)], ybuf.at[k, pl.ds(0, 1)], sem).wait()
        return carry

    lax.fori_loop(0, tm, drain, 0)
    gates = gate_ref[...]
    f = gates[:, 0:1] * ybuf[0]
    for k in range(1, TOP_K):
        f = f + gates[:, k:k + 1] * ybuf[k]
    xo_ref[...] = x_ref[...] + mod_ref[5:6, :] * f


def _combine(y, slot_flat, gates, xs, mod_l, geo):
    nt, d = xs.shape
    tm, per, nb = geo["tm"], geo["per"], geo["b"]
    return pl.pallas_call(
        functools.partial(_combine_kernel, tm=tm),
        grid=(nt // tm,),
        in_specs=[pl.BlockSpec((tm * TOP_K,), lambda i: (i,), memory_space=pltpu.SMEM),
                  pl.BlockSpec((tm, LANES), lambda i: (i, 0)),
                  pl.BlockSpec((tm, d), lambda i: (i, 0)),
                  pl.BlockSpec((None, 6, d), lambda i: (jnp.minimum(i // per, nb), 0, 0)),
                  pl.BlockSpec(memory_space=pl.ANY)],
        out_specs=pl.BlockSpec((tm, d), lambda i: (i, 0)),
        out_shape=jax.ShapeDtypeStruct((nt, d), F32),
        scratch_shapes=[pltpu.VMEM((TOP_K, tm, d), F32), pltpu.SemaphoreType.DMA(())],
        input_output_aliases={2: 0},
        compiler_params=_params(1),
        name="moe_combine",
    )(slot_flat, gates, xs, mod_l, y)


def _moe(h2, top_idx, gates, rank, counts, xs, mod_l, wgu, bgu, wd, bd, geo):
    nt = h2.shape[0]
    n_assign = nt * TOP_K
    nblk = n_assign // MOE_BLOCK + N_EXPERTS
    n_slots = nblk * MOE_BLOCK
    counts = counts.astype(jnp.int32)
    padded = (counts + MOE_BLOCK - 1) // MOE_BLOCK * MOE_BLOCK
    pad_end = jnp.cumsum(padded)
    pad_start = pad_end - padded
    experts = jnp.arange(N_EXPERTS, dtype=jnp.int32)
    idx4 = top_idx[:, :TOP_K]
    base = jnp.sum(jnp.where(idx4[:, :, None] == experts[None, None, :], pad_start[None, None, :], 0), axis=-1)
    slot_flat = (base + rank[:, :TOP_K]).reshape(n_assign).astype(jnp.int32)
    n_used = (pad_end[-1] // MOE_BLOCK).astype(jnp.int32)
    blk_id = jnp.minimum(jnp.arange(nblk, dtype=jnp.int32), n_used - 1)
    block_expert = jnp.minimum(
        jnp.sum((pad_end[None, :] <= (blk_id * MOE_BLOCK)[:, None]).astype(jnp.int32), axis=1), N_EXPERTS - 1)
    xs_slots = _dispatch(h2, slot_flat, (pad_start + counts).astype(jnp.int32),
                         (padded - counts).astype(jnp.int32), n_slots, geo)
    y = _experts(xs_slots, block_expert.astype(jnp.int32), n_used.reshape(1), wgu, bgu, wd, bd)
    return _combine(y, slot_flat, gates, xs, mod_l, geo)


def _final_kernel(x_ref, g_ref, o_ref):
    o_ref[...] = _rms(x_ref[...]) * g_ref[...]


def _final_norm(xs, g, nlat, tm):
    d = xs.shape[1]
    return pl.pallas_call(
        _final_kernel,
        grid=(nlat // tm,),
        in_specs=[pl.BlockSpec((tm, d), lambda i: (i, 0)), pl.BlockSpec((1, d), lambda i: (0, 0))],
        out_specs=pl.BlockSpec((tm, d), lambda i: (i, 0)),
        out_shape=jax.ShapeDtypeStruct((nlat, d), F32),
        compiler_params=_params(1),
        name="final_norm",
    )(xs, g)


def _with_identity_rows(cos, sin, tm):
    ones = jnp.ones((tm, cos.shape[1]), F32)
    return jnp.concatenate([cos, ones], axis=0), jnp.concatenate([sin, 0.0 * ones], axis=0)


def _axial_tables(seq, tm, pad_to_lanes):
    pos = jnp.arange(seq)
    rows = (pos // GRID_W).astype(F32)
    cols = (pos % GRID_W).astype(F32)
    half = DA_HEAD_DIM // 2
    inv = ROPE_BASE ** (-jnp.arange(0, half, 2, dtype=F32) / half)
    ar = rows[:, None] * inv[None, :]
    ac = cols[:, None] * inv[None, :]
    cos64 = jnp.concatenate([jnp.cos(ar), jnp.cos(ar), jnp.cos(ac), jnp.cos(ac)], axis=1)
    sin64 = jnp.concatenate([-jnp.sin(ar), jnp.sin(ar), -jnp.sin(ac), jnp.sin(ac)], axis=1)
    if pad_to_lanes:
        cos = jnp.concatenate([cos64, jnp.ones_like(cos64)], axis=1)
        sin = jnp.concatenate([sin64, jnp.zeros_like(sin64)], axis=1)
    else:
        cos = jnp.concatenate([cos64, cos64], axis=1)
        sin = jnp.concatenate([sin64, sin64], axis=1)
    return _with_identity_rows(cos, sin, tm)


def _ret_tables(seq, tm):
    inv = ROPE_BASE ** (-jnp.arange(0, RET_KDIM, 2, dtype=F32) / RET_KDIM)
    ang = jnp.arange(seq).astype(F32)[:, None] * inv[None, :]
    return _with_identity_rows(jnp.cos(ang), jnp.sin(ang), tm)


def _mla_weights(w_in, w_q_up, w_kv_up):
    d = w_in.shape[0]
    kvw = MLA_KV_RANK + MLA_ROPE
    w_in_r = jnp.concatenate([w_in[:, :MLA_KV_RANK], w_in[:, kvw:], w_in[:, MLA_KV_RANK:kvw],
                              jnp.zeros((d, LANES - MLA_ROPE), w_in.dtype)], axis=1)
    wq = w_q_up.reshape(MLA_Q_RANK, MLA_HEADS, MLA_NOPE + MLA_ROPE)
    wq = jnp.concatenate([wq, jnp.zeros((MLA_Q_RANK, MLA_HEADS, LANES - MLA_ROPE), wq.dtype)], axis=2)
    wq = wq.reshape(MLA_Q_RANK, MLA_HEADS * 2 * LANES)
    wkv = w_kv_up.reshape(MLA_KV_RANK, MLA_HEADS, MLA_NOPE + MLA_V)
    wkv = jnp.concatenate([wkv[:, :, :MLA_NOPE].reshape(MLA_KV_RANK, -1),
                           wkv[:, :, MLA_NOPE:].reshape(MLA_KV_RANK, -1)], axis=1)
    return w_in_r.astype(BF16), wq.astype(BF16), wkv.astype(BF16)


def kernel(x, c, ctx, c_ctx, ada_w, ada_b, norm_g, final_g, da_w_in, da_w_out, da_lambda, da_subln_g,
           ret_w_in, ret_decay_logit, ret_w_out, mla_w_in, mla_q_norm_g, mla_w_q_up, mla_kv_norm_g,
           mla_w_kv_up, mla_w_out, moe_w_router, moe_b_router, moe_w_gate_up, moe_b_gate_up,
           moe_w_down, moe_b_down):
    nb, seq, d = x.shape
    nctx_per = ctx.shape[1]
    depth = ada_w.shape[0]
    nlat = nb * seq
    tm = _pick_tile(512, seq, nb * nctx_per)
    geo = dict(b=nb, seq=seq, ctx=nctx_per, nlat=nlat, tm=tm, per=seq // tm, nlt=nlat // tm)
    assert nb < MOD_ROWS and nlat % nctx_per == 0 and nctx_per % RET_CHUNK == 0

    xs = jnp.concatenate([x.reshape(nlat, d), ctx.reshape(nb * nctx_per, d)], axis=0).astype(F32)
    cond = jnp.zeros((MOD_ROWS, d), F32).at[:nb].set(c).at[nb].set(c_ctx)
    mod = _adaln(cond, ada_w, ada_b).reshape(depth, MOD_ROWS, 6, d)

    cos_a, sin_a = _axial_tables(seq, tm, pad_to_lanes=False)
    cos_m, sin_m = _axial_tables(seq, tm, pad_to_lanes=True)
    cos_r, sin_r = _ret_tables(seq, tm)

    for i in range(depth):
        kind = i % N_MIXERS
        j = i // N_MIXERS
        mod_l = mod[i]
        g1 = norm_g[i, 0].reshape(1, d)
        g2 = norm_g[i, 1].reshape(1, d)
        if kind == 0:
            lam_init = 0.8 - 0.6 * math.exp(-0.3 * i)
            p = _proj(xs, g1, mod_l, da_w_in[j].astype(BF16), cos_a, sin_a,
                      ["rope16", "plain", "rope16"], geo)
            o_lat, o_ctx = _da_attention(p, da_lambda[j].astype(F32),
                                         da_subln_g[j].reshape(1, DA_V_DIM).astype(F32), lam_init, geo)
            ctx_off = 0
            w_out = da_w_out[j]
        elif kind == 1:
            p = _proj(xs, g1, mod_l, ret_w_in[j].astype(BF16), cos_r, sin_r,
                      ["ret_k", "plain", "plain", "ret_q", "plain", "plain"], geo,
                      ret_scale=RET_KDIM ** -0.5)
            o_lat = o_ctx = _retention(p, ret_decay_logit[j], geo)
            ctx_off = geo["nlt"]
            w_out = ret_w_out[j]
        else:
            w_in_r, wq, wkv = _mla_weights(mla_w_in[j], mla_w_q_up[j], mla_w_kv_up[j])
            q, kn, v, kr = _mla_proj(xs, g1, mod_l, w_in_r, mla_q_norm_g[j].reshape(1, -1).astype(F32),
                                     mla_kv_norm_g[j].reshape(1, -1).astype(F32), wq, wkv, cos_m, sin_m, geo)
            o_lat, o_ctx = _mla_attention(q, kn, v, kr, geo)
            ctx_off = 0
            w_out = mla_w_out[j]
        xs, h2, top_idx, gates, rank, counts = _post(o_lat, o_ctx, ctx_off, w_out.astype(BF16), xs, mod_l, g2,
                                                     moe_w_router[i], moe_b_router[i], geo)
        xs = _moe(h2, top_idx, gates, rank, counts[0, :N_EXPERTS], xs, mod_l,
                  moe_w_gate_up[i].astype(BF16), moe_b_gate_up[i].astype(F32),
                  moe_w_down[i].astype(BF16), moe_b_down[i].astype(F32), geo)
    return _final_norm(xs, final_g.reshape(1, d).astype(F32), nlat, tm).reshape(nb, seq, d)
```

```python
import functools
import math

import jax
import jax.numpy as jnp
from jax import lax
from jax.experimental import pallas as pl
from jax.experimental.pallas import tpu as pltpu

F32 = jnp.float32
BF16 = jnp.bfloat16

D_MODEL = 1024
GRID_W = 64
RMS_EPS = 1e-6
ROPE_BASE = 10000.0
N_MIXERS = 3

DA_HEADS = 8
DA_HEAD_DIM = 64
DA_V_DIM = 128

RET_HEADS = 4
RET_KDIM = 256
RET_VDIM = 512
RET_CHUNK = 128

MLA_HEADS = 8
MLA_NOPE = 128
MLA_ROPE = 64
MLA_V = 128
MLA_Q_RANK = 256
MLA_KV_RANK = 128

N_EXPERTS = 32
TOP_K = 4
D_EXPERT = 1024
SWIGLU_LIMIT = 7.0
SWIGLU_ALPHA = 1.702

LANES = 128
MOD_ROWS = 16
MOE_BLOCK = 512
COL_CHUNK = 1024
ATTN_TQ = 1024
ATTN_TK = 1024
VMEM_LIMIT = 56 * 1024 * 1024
NEG_BIG = -1e30


def _params(n_axes):
    return pltpu.CompilerParams(dimension_semantics=("arbitrary",) * n_axes,
                                vmem_limit_bytes=VMEM_LIMIT)


def _pick_tile(cap, *dims):
    t = cap
    while any(d % t for d in dims):
        t //= 2
    return t


def _rms(x):
    return x * lax.rsqrt(jnp.mean(x * x, axis=-1, keepdims=True) + RMS_EPS)


def _norm_mod(x, g, shift, scale):
    return (_rms(x) * g) * (1.0 + scale) + shift


def _dot(a, b):
    return jnp.dot(a, b, preferred_element_type=F32)


def _dot_nt(a, b):
    return lax.dot_general(a, b, (((1,), (1,)), ((), ())), preferred_element_type=F32)


def _dot_tn(a, b):
    return lax.dot_general(a, b, (((0,), (0,)), ((), ())), preferred_element_type=F32)


def _adaln_kernel(c_ref, w_ref, b_ref, o_ref):
    c = c_ref[...]
    a = c * (1.0 / (1.0 + jnp.exp(-c)))
    o_ref[...] = jnp.dot(a, w_ref[...], precision=lax.Precision.HIGHEST,
                         preferred_element_type=F32) + b_ref[...]


def _adaln(cond, ada_w, ada_b):
    depth, d, n = ada_w.shape
    tn = COL_CHUNK
    return pl.pallas_call(
        _adaln_kernel,
        grid=(depth, n // tn),
        in_specs=[pl.BlockSpec((MOD_ROWS, d), lambda l, j: (0, 0)),
                  pl.BlockSpec((None, d, tn), lambda l, j: (l, 0, j)),
                  pl.BlockSpec((None, 1, tn), lambda l, j: (l, 0, j))],
        out_specs=pl.BlockSpec((None, MOD_ROWS, tn), lambda l, j: (l, 0, j)),
        out_shape=jax.ShapeDtypeStruct((depth, MOD_ROWS, n), F32),
        compiler_params=_params(2),
        name="adaln",
    )(cond, ada_w, ada_b.reshape(depth, 1, n))


def _rope_pairs16(x, cos, sin):
    lane = lax.broadcasted_iota(jnp.int32, x.shape, 1)
    first = (lane % 32) < 16
    partner = jnp.where(first, pltpu.roll(x, LANES - 16, 1), pltpu.roll(x, 16, 1))
    return x * cos + partner * sin


def _proj_kernel(x_ref, g_ref, mod_ref, w_ref, cos_ref, sin_ref, o_ref, h_scr, *, kinds, ret_scale):
    j = pl.program_id(1)

    @pl.when(j == 0)
    def _():
        h = _norm_mod(x_ref[...], g_ref[...], mod_ref[0:1, :], mod_ref[1:2, :])
        h_scr[...] = h.astype(BF16)

    def chunks_of(kind):
        cs = [c for c, k in enumerate(kinds) if k == kind]
        pred = None
        for c in cs:
            pred = (j == c) if pred is None else jnp.logical_or(pred, j == c)
        return pred

    def acc():
        return _dot(h_scr[...], w_ref[...])

    if "plain" in kinds:
        @pl.when(chunks_of("plain"))
        def _():
            o_ref[...] = acc().astype(BF16)

    if "rope16" in kinds:
        @pl.when(chunks_of("rope16"))
        def _():
            a = acc()
            cos = cos_ref[...]
            sin = sin_ref[...]
            for g in range(COL_CHUNK // LANES):
                sl = slice(g * LANES, (g + 1) * LANES)
                o_ref[:, sl] = _rope_pairs16(a[:, sl], cos, sin).astype(BF16)

    for kind, scale in (("ret_k", ret_scale), ("ret_q", 1.0)):
        if kind in kinds:
            @pl.when(chunks_of(kind))
            def _(scale=scale):
                a = acc()
                cos = cos_ref[...]
                sin = sin_ref[...]
                for hh in range(COL_CHUNK // RET_KDIM):
                    lo = slice(hh * RET_KDIM, hh * RET_KDIM + LANES)
                    hi = slice(hh * RET_KDIM + LANES, (hh + 1) * RET_KDIM)
                    x1 = a[:, lo]
                    x2 = a[:, hi]
                    o_ref[:, lo] = ((x1 * cos - x2 * sin) * scale).astype(BF16)
                    o_ref[:, hi] = ((x1 * sin + x2 * cos) * scale).astype(BF16)


def _proj(xs, g, mod_l, w, cos_t, sin_t, kinds, geo, ret_scale=1.0):
    nt, d = xs.shape
    n = w.shape[1]
    tm, per, nlt, nb = geo["tm"], geo["per"], geo["nlt"], geo["b"]
    return pl.pallas_call(
        functools.partial(_proj_kernel, kinds=tuple(kinds), ret_scale=ret_scale),
        grid=(nt // tm, n // COL_CHUNK),
        in_specs=[pl.BlockSpec((tm, d), lambda i, j: (i, 0)),
                  pl.BlockSpec((1, d), lambda i, j: (0, 0)),
                  pl.BlockSpec((None, 6, d), lambda i, j: (jnp.minimum(i // per, nb), 0, 0)),
                  pl.BlockSpec((d, COL_CHUNK), lambda i, j: (0, j)),
                  pl.BlockSpec((tm, LANES), lambda i, j: (jnp.where(i < nlt, i % per, per), 0)),
                  pl.BlockSpec((tm, LANES), lambda i, j: (jnp.where(i < nlt, i % per, per), 0))],
        out_specs=pl.BlockSpec((tm, COL_CHUNK), lambda i, j: (i, j)),
        out_shape=jax.ShapeDtypeStruct((nt, n), BF16),
        scratch_shapes=[pltpu.VMEM((tm, d), BF16)],
        compiler_params=_params(2),
        name="proj",
    )(xs, g, mod_l, w, cos_t, sin_t)


def _mla_proj_kernel(x_ref, g_ref, mod_ref, win_ref, qg_ref, kvg_ref, wq_ref, wkv_ref, cos_ref, sin_ref,
                     q_ref, kn_ref, v_ref, kr_ref):
    h = _norm_mod(x_ref[...], g_ref[...], mod_ref[0:1, :], mod_ref[1:2, :]).astype(BF16)
    p = _dot(h, win_ref[...])
    cos = cos_ref[...]
    sin = sin_ref[...]
    ckv = (_rms(p[:, :MLA_KV_RANK]) * kvg_ref[...]).astype(BF16)
    kv = _dot(ckv, wkv_ref[...])
    nk = MLA_HEADS * MLA_NOPE
    kn_ref[...] = kv[:, :nk].astype(BF16)
    v_ref[...] = kv[:, nk:].astype(BF16)
    kr_ref[...] = _rope_pairs16(p[:, MLA_KV_RANK + MLA_Q_RANK:], cos, sin).astype(BF16)
    cq = (_rms(p[:, MLA_KV_RANK:MLA_KV_RANK + MLA_Q_RANK]) * qg_ref[...]).astype(BF16)
    q = _dot(cq, wq_ref[...])
    for hh in range(MLA_HEADS):
        lo = slice(hh * 2 * LANES, hh * 2 * LANES + LANES)
        hi = slice(hh * 2 * LANES + LANES, (hh + 1) * 2 * LANES)
        q_ref[:, lo] = q[:, lo].astype(BF16)
        q_ref[:, hi] = _rope_pairs16(q[:, hi], cos, sin).astype(BF16)


def _mla_proj(xs, g, mod_l, w_in, qg, kvg, wq, wkv, cos_t, sin_t, geo):
    nt, d = xs.shape
    tm, per, nlt, nb = geo["tm"], geo["per"], geo["nlt"], geo["b"]
    full = lambda a: pl.BlockSpec(a.shape, lambda i: (0,) * a.ndim)
    tab = pl.BlockSpec((tm, LANES), lambda i: (jnp.where(i < nlt, i % per, per), 0))
    nq = MLA_HEADS * 2 * LANES
    nk = MLA_HEADS * MLA_NOPE
    return pl.pallas_call(
        _mla_proj_kernel,
        grid=(nt // tm,),
        in_specs=[pl.BlockSpec((tm, d), lambda i: (i, 0)), full(g),
                  pl.BlockSpec((None, 6, d), lambda i: (jnp.minimum(i // per, nb), 0, 0)),
                  full(w_in), full(qg), full(kvg), full(wq), full(wkv), tab, tab],
        out_specs=[pl.BlockSpec((tm, nq), lambda i: (i, 0)),
                   pl.BlockSpec((tm, nk), lambda i: (i, 0)),
                   pl.BlockSpec((tm, nk), lambda i: (i, 0)),
                   pl.BlockSpec((tm, LANES), lambda i: (i, 0))],
        out_shape=[jax.ShapeDtypeStruct((nt, nq), BF16), jax.ShapeDtypeStruct((nt, nk), BF16),
                   jax.ShapeDtypeStruct((nt, nk), BF16), jax.ShapeDtypeStruct((nt, LANES), BF16)],
        compiler_params=_params(1),
        name="mla_proj",
    )(xs, g, mod_l, w_in, qg, kvg, wq, wkv, cos_t, sin_t)


LOG2E = 1.4426950408889634


def _flash_init(m_ref, a_ref):
    m_ref[...] = jnp.full(m_ref.shape, -jnp.inf, F32)
    a_ref[...] = jnp.zeros(a_ref.shape, F32)


def _ones_column(rows):
    lane = lax.broadcasted_iota(jnp.int32, (rows, LANES), 1)
    return jnp.where(lane == 0, 1.0, 0.0).astype(BF16)


def _chunk_rows(j, tk):
    if isinstance(j, int):
        return pl.ds(j * tk, tk)
    return pl.ds(pl.multiple_of(j * tk, tk), tk)


def _softmax_pv(parts, stats):
    for mp, (m_ref, a_ref) in enumerate(stats):
        tiles = [scores[mp] for scores, _ in parts]
        m_prev = m_ref[...]
        m_new = m_prev
        for s2 in tiles:
            m_new = jnp.maximum(m_new, jnp.max(s2, axis=1, keepdims=True))
        acc = jnp.exp2(m_prev - m_new) * a_ref[...]
        for s2, (_, v_aug) in zip(tiles, parts):
            acc = acc + _dot(jnp.exp2(s2 - m_new).astype(BF16), v_aug)
        a_ref[...] = acc
        m_ref[...] = m_new


def _flash_lat(qk_fns, stats, k_ctx, v_ctx, k_at, v_at, s_a, s_b, s_c, n_chunks):
    def qk_into(dst, k):
        for mp, f in enumerate(qk_fns):
            dst[mp] = f(k)

    qk_into(s_c, k_ctx)
    qk_into(s_a, k_at(0))
    qk_into(s_b, k_at(1))
    _softmax_pv([(s_c, v_ctx), (s_a, v_at(0))], stats)

    def body(t, carry):
        j = 2 * t + 1
        qk_into(s_a, k_at(j + 1))
        _softmax_pv([(s_b, v_at(j))], stats)
        qk_into(s_b, k_at(j + 2))
        _softmax_pv([(s_a, v_at(j + 1))], stats)
        return carry

    lax.fori_loop(0, (n_chunks - 2) // 2, body, 0)
    _softmax_pv([(s_b, v_at(n_chunks - 1))], stats)


def _flash_result(a_ref, vdim):
    a = a_ref[...]
    return a[:, :vdim] / a[:, vdim:vdim + 1]


def _da_attn_kernel(*refs, n_chunks, tk, lam_init, scale):
    if n_chunks:
        (lam_ref, sg_ref, q_ref, kc_ref, vc_ref, kl_ref, vl_ref, o_ref, m0, a0, m1, a1, s_a, s_b, s_c) = refs
    else:
        (lam_ref, sg_ref, q_ref, kc_ref, vc_ref, o_ref, m0, a0, m1, a1) = refs
    q = q_ref[...]
    lane = lax.broadcasted_iota(jnp.int32, q.shape, 1)
    zero = jnp.zeros_like(q)
    q_lo = jnp.where(lane < DA_HEAD_DIM, q, zero)
    q_hi = jnp.where(lane >= DA_HEAD_DIM, q, zero)
    _flash_init(m0, a0)
    _flash_init(m1, a1)
    c2 = scale * LOG2E
    stats = [(m0, a0), (m1, a1)]
    qk_fns = [lambda k: _dot_nt(q_lo, k) * c2, lambda k: _dot_nt(q_hi, k) * c2]
    v_ctx = jnp.concatenate([vc_ref[...], _ones_column(vc_ref.shape[0])], axis=1)
    if n_chunks:
        ones = _ones_column(tk)

        def k_at(j):
            return kl_ref[_chunk_rows(j, tk), :]

        def v_at(j):
            return jnp.concatenate([vl_ref[_chunk_rows(j, tk), :], ones], axis=1)

        _flash_lat(qk_fns, stats, kc_ref[...], v_ctx, k_at, v_at, s_a, s_b, s_c, n_chunks)
    else:
        _softmax_pv([([f(kc_ref[...]) for f in qk_fns], v_ctx)], stats)

    lf = lam_ref[...]
    lam = (jnp.exp(jnp.sum(lf[0:1] * lf[1:2], axis=1, keepdims=True))
           - jnp.exp(jnp.sum(lf[2:3] * lf[3:4], axis=1, keepdims=True)) + lam_init)
    o = _flash_result(a0, DA_V_DIM) - lam * _flash_result(a1, DA_V_DIM)
    o = _rms(o) * sg_ref[...] * (1.0 - lam_init)
    o_ref[...] = o.astype(BF16)


def _attn_scratch(tq, vdim, n_maps, tk=0, ctx=0):
    s = []
    for _ in range(n_maps):
        s += [pltpu.VMEM((tq, 1), F32), pltpu.VMEM((tq, vdim + LANES), F32)]
    if tk:
        s += [pltpu.VMEM((n_maps, tq, tk), F32), pltpu.VMEM((n_maps, tq, tk), F32),
              pltpu.VMEM((n_maps, tq, ctx), F32)]
    return s


def _da_attention(p, lam_vecs, subg, lam_init, geo):
    nt = p.shape[0]
    nb, seq, ctx, nlat = geo["b"], geo["seq"], geo["ctx"], geo["nlat"]
    hh = DA_HEADS
    tq = _pick_tile(ATTN_TQ, seq)
    tk = _pick_tile(min(ATTN_TK, seq // 2), seq)
    assert (seq // tk) % 2 == 0
    nqt = seq // tq
    cb = nlat // ctx
    scale = DA_HEAD_DIM ** -0.5
    small = [pl.BlockSpec(lam_vecs.shape, lambda *a: (0, 0)), pl.BlockSpec(subg.shape, lambda *a: (0, 0))]
    o_lat = pl.pallas_call(
        functools.partial(_da_attn_kernel, n_chunks=seq // tk, tk=tk, lam_init=lam_init, scale=scale),
        grid=(nb, hh, nqt),
        in_specs=small + [
            pl.BlockSpec((tq, LANES), lambda b, h, i: (b * nqt + i, 2 * hh + h)),
            pl.BlockSpec((ctx, LANES), lambda b, h, i: (cb + b, h)),
            pl.BlockSpec((ctx, LANES), lambda b, h, i: (cb + b, hh + h)),
            pl.BlockSpec((seq, LANES), lambda b, h, i: (b, h)),
            pl.BlockSpec((seq, LANES), lambda b, h, i: (b, hh + h))],
        out_specs=pl.BlockSpec((tq, LANES), lambda b, h, i: (b * nqt + i, h)),
        out_shape=jax.ShapeDtypeStruct((nlat, hh * DA_V_DIM), BF16),
        scratch_shapes=_attn_scratch(tq, DA_V_DIM, 2, tk, ctx),
        compiler_params=_params(3),
        name="da_attn_lat",
    )(lam_vecs, subg, p, p, p, p, p)
    o_ctx = pl.pallas_call(
        functools.partial(_da_attn_kernel, n_chunks=0, tk=tk, lam_init=lam_init, scale=scale),
        grid=(nb, hh),
        in_specs=small + [
            pl.BlockSpec((ctx, LANES), lambda b, h: (cb + b, 2 * hh + h)),
            pl.BlockSpec((ctx, LANES), lambda b, h: (cb + b, h)),
            pl.BlockSpec((ctx, LANES), lambda b, h: (cb + b, hh + h))],
        out_specs=pl.BlockSpec((ctx, LANES), lambda b, h: (b, h)),
        out_shape=jax.ShapeDtypeStruct((nt - nlat, hh * DA_V_DIM), BF16),
        scratch_shapes=_attn_scratch(ctx, DA_V_DIM, 2),
        compiler_params=_params(2),
        name="da_attn_ctx",
    )(lam_vecs, subg, p, p, p)
    return o_lat, o_ctx


def _mla_attn_kernel(*refs, n_chunks, tk, scale):
    if n_chunks:
        (q_ref, knc_ref, krc_ref, vc_ref, knl_ref, krl_ref, vl_ref, o_ref, m0, a0, s_a, s_b, s_c) = refs
    else:
        (q_ref, knc_ref, krc_ref, vc_ref, o_ref, m0, a0) = refs
    q = q_ref[...]
    _flash_init(m0, a0)
    c2 = scale * LOG2E
    stats = [(m0, a0)]
    qk_fns = [lambda k: _dot_nt(q, k) * c2]
    k_ctx = jnp.concatenate([knc_ref[...], krc_ref[...]], axis=1)
    v_ctx = jnp.concatenate([vc_ref[...], _ones_column(vc_ref.shape[0])], axis=1)
    if n_chunks:
        ones = _ones_column(tk)

        def k_at(j):
            rows = _chunk_rows(j, tk)
            return jnp.concatenate([knl_ref[rows, :], krl_ref[rows, :]], axis=1)

        def v_at(j):
            return jnp.concatenate([vl_ref[_chunk_rows(j, tk), :], ones], axis=1)

        _flash_lat(qk_fns, stats, k_ctx, v_ctx, k_at, v_at, s_a, s_b, s_c, n_chunks)
    else:
        _softmax_pv([([qk_fns[0](k_ctx)], v_ctx)], stats)
    o_ref[...] = _flash_result(a0, MLA_V).astype(BF16)


def _mla_attention(q, kn, v, kr, geo):
    nt = q.shape[0]
    nb, seq, ctx, nlat = geo["b"], geo["seq"], geo["ctx"], geo["nlat"]
    hh = MLA_HEADS
    tq = _pick_tile(ATTN_TQ, seq)
    tk = _pick_tile(min(ATTN_TK, seq // 2), seq)
    assert (seq // tk) % 2 == 0
    nqt = seq // tq
    cb = nlat // ctx
    scale = (MLA_NOPE + MLA_ROPE) ** -0.5
    o_lat = pl.pallas_call(
        functools.partial(_mla_attn_kernel, n_chunks=seq // tk, tk=tk, scale=scale),
        grid=(nb, hh, nqt),
        in_specs=[
            pl.BlockSpec((tq, 2 * LANES), lambda b, h, i: (b * nqt + i, h)),
            pl.BlockSpec((ctx, LANES), lambda b, h, i: (cb + b, h)),
            pl.BlockSpec((ctx, LANES), lambda b, h, i: (cb + b, 0)),
            pl.BlockSpec((ctx, LANES), lambda b, h, i: (cb + b, h)),
            pl.BlockSpec((seq, LANES), lambda b, h, i: (b, h)),
            pl.BlockSpec((seq, LANES), lambda b, h, i: (b, 0)),
            pl.BlockSpec((seq, LANES), lambda b, h, i: (b, h))],
        out_specs=pl.BlockSpec((tq, LANES), lambda b, h, i: (b * nqt + i, h)),
        out_shape=jax.ShapeDtypeStruct((nlat, hh * MLA_V), BF16),
        scratch_shapes=_attn_scratch(tq, MLA_V, 1, tk, ctx),
        compiler_params=_params(3),
        name="mla_attn_lat",
    )(q, kn, kr, v, kn, kr, v)
    o_ctx = pl.pallas_call(
        functools.partial(_mla_attn_kernel, n_chunks=0, tk=tk, scale=scale),
        grid=(nb, hh),
        in_specs=[
            pl.BlockSpec((ctx, 2 * LANES), lambda b, h: (cb + b, h)),
            pl.BlockSpec((ctx, LANES), lambda b, h: (cb + b, h)),
            pl.BlockSpec((ctx, LANES), lambda b, h: (cb + b, 0)),
            pl.BlockSpec((ctx, LANES), lambda b, h: (cb + b, h))],
        out_specs=pl.BlockSpec((ctx, LANES), lambda b, h: (b, h)),
        out_shape=jax.ShapeDtypeStruct((nt - nlat, hh * MLA_V), BF16),
        scratch_shapes=_attn_scratch(ctx, MLA_V, 1),
        compiler_params=_params(2),
        name="mla_attn_ctx",
    )(q, kn, kr, v)
    return o_lat, o_ctx


def _ret_kernel(*refs, reverse):
    if reverse:
        dl_ref, q_ref, k_ref, v_ref, of_ref, g_ref, o_ref, st = refs
    else:
        dl_ref, q_ref, k_ref, v_ref, o_ref, st = refs
    hd = pl.program_id(1)
    s = pl.program_id(2)

    @pl.when(s == 0)
    def _():
        st[...] = jnp.zeros(st.shape, F32)

    dl = dl_ref[...]
    lsig = jnp.minimum(dl, 0.0) - jnp.log(1.0 + jnp.exp(-jnp.abs(dl)))
    row = lax.broadcasted_iota(jnp.int32, dl.shape, 0)
    col = lax.broadcasted_iota(jnp.int32, dl.shape, 1)
    pick = jnp.logical_and(row == (1 if reverse else 0), col == hd)
    lg = jnp.sum(jnp.sum(jnp.where(pick, lsig, 0.0), axis=1, keepdims=True), axis=0, keepdims=True)

    c = RET_CHUNK
    i2 = lax.broadcasted_iota(jnp.int32, (c, c), 0)
    j2 = lax.broadcasted_iota(jnp.int32, (c, c), 1)
    pos = lax.broadcasted_iota(jnp.int32, (c, 1), 0).astype(F32)
    if reverse:
        dist = (j2 - i2).astype(F32)
        q_dec = jnp.exp((c - pos) * lg)
        k_dec = jnp.exp(pos * lg)
    else:
        dist = (i2 - j2).astype(F32)
        q_dec = jnp.exp((pos + 1.0) * lg)
        k_dec = jnp.exp((c - 1.0 - pos) * lg)
    keep = dist >= 0.0
    intra = jnp.where(keep, jnp.exp(jnp.where(keep, dist, 0.0) * lg), 0.0)
    chunk_dec = jnp.exp(c * lg)

    qb = q_ref[...]
    kb = k_ref[...]
    vb = v_ref[...]
    state = st[...]
    sc = _dot_nt(qb, kb) * intra
    o = _dot(sc.astype(BF16), vb) + _dot((qb.astype(F32) * q_dec).astype(BF16), state.astype(BF16))
    st[...] = chunk_dec * state + _dot_tn((kb.astype(F32) * k_dec).astype(BF16), vb)
    if reverse:
        tot = _rms(of_ref[...] + o)
        g = g_ref[...].astype(F32)
        o_ref[...] = (g * (1.0 / (1.0 + jnp.exp(-g))) * tot).astype(BF16)
    else:
        o_ref[...] = o


def _retention(p, decay_logit, geo):
    nt = p.shape[0]
    nb, seq, ctx, nlat = geo["b"], geo["seq"], geo["ctx"], geo["nlat"]
    c = RET_CHUNK
    ncc, ncl = ctx // c, seq // c
    hh = RET_HEADS
    dl = jnp.zeros((8, LANES), F32).at[:2, :hh].set(decay_logit.astype(F32))

    def rows_fwd(b, s):
        return jnp.where(s < ncc, nlat // c + b * ncc + s, b * ncl + (s - ncc))

    def rows_bwd(b, s):
        return jnp.where(s < ncc, nlat // c + b * ncc + (ncc - 1 - s), b * ncl + (ncl - 1 - (s - ncc)))

    kq = RET_KDIM
    kv = RET_VDIM
    q_off = (hh * kq + hh * kv) // kq
    v_off = (hh * kq) // kv
    g_off = (2 * hh * kq + hh * kv) // kv

    def specs(rows):
        return [pl.BlockSpec((8, LANES), lambda b, h, s: (0, 0)),
                pl.BlockSpec((c, kq), lambda b, h, s: (rows(b, s), q_off + h)),
                pl.BlockSpec((c, kq), lambda b, h, s: (rows(b, s), h)),
                pl.BlockSpec((c, kv), lambda b, h, s: (rows(b, s), v_off + h))]

    grid = (nb, hh, ncc + ncl)
    o_f = pl.pallas_call(
        functools.partial(_ret_kernel, reverse=False),
        grid=grid,
        in_specs=specs(rows_fwd),
        out_specs=pl.BlockSpec((c, kv), lambda b, h, s: (rows_fwd(b, s), h)),
        out_shape=jax.ShapeDtypeStruct((nt, hh * kv), F32),
        scratch_shapes=[pltpu.VMEM((kq, kv), F32)],
        compiler_params=_params(3),
        name="ret_fwd",
    )(dl, p, p, p)
    return pl.pallas_call(
        functools.partial(_ret_kernel, reverse=True),
        grid=grid,
        in_specs=specs(rows_bwd) + [
            pl.BlockSpec((c, kv), lambda b, h, s: (rows_bwd(b, s), h)),
            pl.BlockSpec((c, kv), lambda b, h, s: (rows_bwd(b, s), g_off + h))],
        out_specs=pl.BlockSpec((c, kv), lambda b, h, s: (rows_bwd(b, s), h)),
        out_shape=jax.ShapeDtypeStruct((nt, hh * kv), BF16),
        scratch_shapes=[pltpu.VMEM((kq, kv), F32)],
        compiler_params=_params(3),
        name="ret_bwd",
    )(dl, p, p, p, o_f, p)


def _post_kernel(ol_ref, oc_ref, w_ref, x_ref, mod_ref, g2_ref, wr_ref, br_ref, tri_ref,
                 xo_ref, h2_ref, idx_ref, gate_ref, rank_ref, cnt_ref, cnt_scr, *, nlt):
    t = pl.program_id(0)

    @pl.when(t == 0)
    def _():
        cnt_scr[...] = jnp.zeros(cnt_scr.shape, F32)

    o = jnp.where(t < nlt, ol_ref[...], oc_ref[...])
    y = _dot(o, w_ref[...])
    x = x_ref[...] + mod_ref[2:3, :] * y
    xo_ref[...] = x
    h2 = _norm_mod(x, g2_ref[...], mod_ref[3:4, :], mod_ref[4:5, :])
    h2_ref[...] = h2
    logits = jnp.dot(h2, wr_ref[...], precision=lax.Precision.HIGHEST,
                     preferred_element_type=F32) + br_ref[...]
    lane = lax.broadcasted_iota(jnp.int32, logits.shape, 1).astype(F32)
    work = logits
    vals, idxs = [], []
    for _ in range(TOP_K):
        mx = jnp.max(work, axis=1, keepdims=True)
        ix = jnp.min(jnp.where(work == mx, lane, float(LANES)), axis=1, keepdims=True)
        vals.append(mx)
        idxs.append(ix)
        work = jnp.where(lane == ix, -jnp.inf, work)
    es = [jnp.exp(v - vals[0]) for v in vals]
    den = es[0] + es[1] + es[2] + es[3]
    onehots = [lane == ix for ix in idxs]
    oh = jnp.zeros(logits.shape, F32)
    for m in onehots:
        oh = oh + jnp.where(m, 1.0, 0.0)
    before = _dot(tri_ref[...], oh.astype(BF16)) + cnt_scr[0:1, :]
    idx_out = jnp.zeros(logits.shape, F32)
    gate_out = jnp.zeros(logits.shape, F32)
    rank_out = jnp.zeros(logits.shape, F32)
    for r in range(TOP_K):
        sel = lane == float(r)
        rk = jnp.sum(jnp.where(onehots[r], before, 0.0), axis=1, keepdims=True)
        idx_out = jnp.where(sel, idxs[r], idx_out)
        gate_out = jnp.where(sel, es[r] / den, gate_out)
        rank_out = jnp.where(sel, rk, rank_out)
    idx_ref[...] = idx_out.astype(jnp.int32)
    gate_ref[...] = gate_out
    rank_ref[...] = rank_out.astype(jnp.int32)
    cnt_scr[0:1, :] = cnt_scr[0:1, :] + jnp.sum(oh, axis=0, keepdims=True)
    cnt_ref[...] = cnt_scr[...]


def _post(o_lat, o_ctx, ctx_off, w_out, xs, mod_l, g2, w_router, b_router, geo):
    nt, d = xs.shape
    ko = o_lat.shape[1]
    tm, per, nb, nlt = geo["tm"], geo["per"], geo["b"], geo["nlt"]
    ii = lax.broadcasted_iota(jnp.int32, (tm, tm), 0)
    jj = lax.broadcasted_iota(jnp.int32, (tm, tm), 1)
    tri = (jj < ii).astype(BF16)
    wr = jnp.zeros((d, LANES), F32).at[:, :N_EXPERTS].set(w_router.astype(F32))
    br = jnp.full((1, LANES), NEG_BIG, F32).at[0, :N_EXPERTS].set(b_router.astype(F32))
    row = lambda w: pl.BlockSpec((tm, w), lambda i: (i, 0))
    full = lambda a: pl.BlockSpec(a.shape, lambda i: (0,) * a.ndim)
    return pl.pallas_call(
        functools.partial(_post_kernel, nlt=nlt),
        grid=(nt // tm,),
        in_specs=[pl.BlockSpec((tm, ko), lambda i: (jnp.minimum(i, nlt - 1), 0)),
                  pl.BlockSpec((tm, ko), lambda i: (jnp.maximum(i - nlt, 0) + ctx_off, 0)),
                  full(w_out), row(d),
                  pl.BlockSpec((None, 6, d), lambda i: (jnp.minimum(i // per, nb), 0, 0)),
                  full(g2), full(wr), full(br), full(tri)],
        out_specs=[row(d), row(d), row(LANES), row(LANES), row(LANES),
                   pl.BlockSpec((8, LANES), lambda i: (0, 0))],
        out_shape=[jax.ShapeDtypeStruct((nt, d), F32), jax.ShapeDtypeStruct((nt, d), F32),
                   jax.ShapeDtypeStruct((nt, LANES), jnp.int32), jax.ShapeDtypeStruct((nt, LANES), F32),
                   jax.ShapeDtypeStruct((nt, LANES), jnp.int32), jax.ShapeDtypeStruct((8, LANES), F32)],
        scratch_shapes=[pltpu.VMEM((8, LANES), F32)],
        input_output_aliases={3: 0},
        compiler_params=_params(1),
        name="post",
    )(o_lat, o_ctx, w_out, xs, mod_l, g2, wr, br, tri)


def _row_copy(src, dst, sem):
    return pltpu.make_async_copy(src, dst, sem)


def _dispatch_kernel(zs_ref, zc_ref, slot_ref, h_ref, xs_ref, zero_scr, sem, *, tm):
    t = pl.program_id(0)

    @pl.when(t == 0)
    def _():
        zero_scr[...] = jnp.zeros(zero_scr.shape, F32)

        def per_expert(e, carry):
            start = zs_ref[e]
            n = zc_ref[e]

            def issue(r, c2):
                _row_copy(zero_scr.at[pl.ds(0, 1)], xs_ref.at[pl.ds(start + r, 1)], sem).start()
                return c2

            lax.fori_loop(0, n, issue, 0)

            def drain(r, c2):
                _row_copy(zero_scr.at[pl.ds(0, 1)], xs_ref.at[pl.ds(0, 1)], sem).wait()
                return c2

            lax.fori_loop(0, n, drain, 0)
            return carry

        lax.fori_loop(0, N_EXPERTS, per_expert, 0)

    def issue(r, carry):
        for k in range(TOP_K):
            s = slot_ref[r * TOP_K + k]
            _row_copy(h_ref.at[pl.ds(r, 1)], xs_ref.at[pl.ds(s, 1)], sem).start()
        return carry

    lax.fori_loop(0, tm, issue, 0)

    def drain(r, carry):
        for k in range(TOP_K):
            _row_copy(h_ref.at[pl.ds(0, 1)], xs_ref.at[pl.ds(0, 1)], sem).wait()
        return carry

    lax.fori_loop(0, tm, drain, 0)


def _dispatch(h2, slot_flat, zstart, zcount, n_slots, geo):
    nt, d = h2.shape
    tm = geo["tm"]
    return pl.pallas_call(
        functools.partial(_dispatch_kernel, tm=tm),
        grid_spec=pltpu.PrefetchScalarGridSpec(
            num_scalar_prefetch=2,
            grid=(nt // tm,),
            in_specs=[pl.BlockSpec((tm * TOP_K,), lambda i, zs, zc: (i,), memory_space=pltpu.SMEM),
                      pl.BlockSpec((tm, d), lambda i, zs, zc: (i, 0))],
            out_specs=pl.BlockSpec(memory_space=pl.ANY),
            scratch_shapes=[pltpu.VMEM((8, d), F32), pltpu.SemaphoreType.DMA(())]),
        out_shape=jax.ShapeDtypeStruct((n_slots, d), F32),
        compiler_params=_params(1),
        name="moe_dispatch",
    )(zstart, zcount, slot_flat, h2)


def _expert_kernel(be_ref, nu_ref, x_ref, wgu_ref, bgu_ref, wd_ref, bd_ref, y_ref):
    i = pl.program_id(0)

    @pl.when(i < nu_ref[0])
    def _():
        x = x_ref[...].astype(BF16)
        gu = _dot(x, wgu_ref[...]) + bgu_ref[...]
        glu = jnp.minimum(gu[:, :D_EXPERT], SWIGLU_LIMIT)
        lin = jnp.clip(gu[:, D_EXPERT:], -SWIGLU_LIMIT, SWIGLU_LIMIT)
        act = glu * (1.0 / (1.0 + jnp.exp(-SWIGLU_ALPHA * glu))) * (lin + 1.0)
        y_ref[...] = _dot(act.astype(BF16), wd_ref[...]) + bd_ref[...]


def _experts(xs, block_expert, n_used, wgu, bgu, wd, bd):
    ns, d = xs.shape
    nblk = ns // MOE_BLOCK
    ne, _, f2 = wgu.shape
    f = wd.shape[1]
    blk = lambda i, be, nu: (jnp.minimum(i, nu[0] - 1), 0)
    return pl.pallas_call(
        _expert_kernel,
        grid_spec=pltpu.PrefetchScalarGridSpec(
            num_scalar_prefetch=2,
            grid=(nblk,),
            in_specs=[pl.BlockSpec((MOE_BLOCK, d), blk),
                      pl.BlockSpec((None, d, f2), lambda i, be, nu: (be[i], 0, 0)),
                      pl.BlockSpec((None, 1, f2), lambda i, be, nu: (be[i], 0, 0)),
                      pl.BlockSpec((None, f, d), lambda i, be, nu: (be[i], 0, 0)),
                      pl.BlockSpec((None, 1, d), lambda i, be, nu: (be[i], 0, 0))],
            out_specs=pl.BlockSpec((MOE_BLOCK, d), blk)),
        out_shape=jax.ShapeDtypeStruct((ns, d), F32),
        compiler_params=_params(1),
        name="moe_experts",
    )(block_expert, n_used, xs, wgu, bgu.reshape(ne, 1, f2), wd, bd.reshape(ne, 1, d))


def _combine_kernel(slot_ref, gate_ref, x_ref, mod_ref, y_ref, xo_ref, ybuf, sem, *, tm):
    def issue(r, carry):
        for k in range(TOP_K):
            s = slot_ref[r * TOP_K + k]
            _row_copy(y_ref.at[pl.ds(s, 1)], ybuf.at[k, pl.ds(r, 1)], sem).start()
        return carry

    lax.fori_loop(0, tm, issue, 0)

    def drain(r, carry):
        for k in range(TOP_K):
            _row_copy(y_ref.at[pl.ds(0, 1)], ybuf.at[k, pl.ds(0, 1)], sem).wait()
        return carry

    lax.fori_loop(0, tm, drain, 0)
    gates = gate_ref[...]
    f = gates[:, 0:1] * ybuf[0]
    for k in range(1, TOP_K):
        f = f + gates[:, k:k + 1] * ybuf[k]
    xo_ref[...] = x_ref[...] + mod_ref[5:6, :] * f


def _combine(y, slot_flat, gates, xs, mod_l, geo):
    nt, d = xs.shape
    tm, per, nb = geo["tm"], geo["per"], geo["b"]
    return pl.pallas_call(
        functools.partial(_combine_kernel, tm=tm),
        grid=(nt // tm,),
        in_specs=[pl.BlockSpec((tm * TOP_K,), lambda i: (i,), memory_space=pltpu.SMEM),
                  pl.BlockSpec((tm, LANES), lambda i: (i, 0)),
                  pl.BlockSpec((tm, d), lambda i: (i, 0)),
                  pl.BlockSpec((None, 6, d), lambda i: (jnp.minimum(i // per, nb), 0, 0)),
                  pl.BlockSpec(memory_space=pl.ANY)],
        out_specs=pl.BlockSpec((tm, d), lambda i: (i, 0)),
        out_shape=jax.ShapeDtypeStruct((nt, d), F32),
        scratch_shapes=[pltpu.VMEM((TOP_K, tm, d), F32), pltpu.SemaphoreType.DMA(())],
        input_output_aliases={2: 0},
        compiler_params=_params(1),
        name="moe_combine",
    )(slot_flat, gates, xs, mod_l, y)


def _moe(h2, top_idx, gates, rank, counts, xs, mod_l, wgu, bgu, wd, bd, geo):
    nt = h2.shape[0]
    n_assign = nt * TOP_K
    nblk = n_assign // MOE_BLOCK + N_EXPERTS
    n_slots = nblk * MOE_BLOCK
    counts = counts.astype(jnp.int32)
    padded = (counts + MOE_BLOCK - 1) // MOE_BLOCK * MOE_BLOCK
    pad_end = jnp.cumsum(padded)
    pad_start = pad_end - padded
    experts = jnp.arange(N_EXPERTS, dtype=jnp.int32)
    idx4 = top_idx[:, :TOP_K]
    base = jnp.sum(jnp.where(idx4[:, :, None] == experts[None, None, :], pad_start[None, None, :], 0), axis=-1)
    slot_flat = (base + rank[:, :TOP_K]).reshape(n_assign).astype(jnp.int32)
    n_used = (pad_end[-1] // MOE_BLOCK).astype(jnp.int32)
    blk_id = jnp.minimum(jnp.arange(nblk, dtype=jnp.int32), n_used - 1)
    block_expert = jnp.minimum(
        jnp.sum((pad_end[None, :] <= (blk_id * MOE_BLOCK)[:, None]).astype(jnp.int32), axis=1), N_EXPERTS - 1)
    xs_slots = _dispatch(h2, slot_flat, (pad_start + counts).astype(jnp.int32),
                         (padded - counts).astype(jnp.int32), n_slots, geo)
    y = _experts(xs_slots, block_expert.astype(jnp.int32), n_used.reshape(1), wgu, bgu, wd, bd)
    return _combine(y, slot_flat, gates, xs, mod_l, geo)


def _final_kernel(x_ref, g_ref, o_ref):
    o_ref[...] = _rms(x_ref[...]) * g_ref[...]


def _final_norm(xs, g, nlat, tm):
    d = xs.shape[1]
    return pl.pallas_call(
        _final_kernel,
        grid=(nlat // tm,),
        in_specs=[pl.BlockSpec((tm, d), lambda i: (i, 0)), pl.BlockSpec((1, d), lambda i: (0, 0))],
        out_specs=pl.BlockSpec((tm, d), lambda i: (i, 0)),
        out_shape=jax.ShapeDtypeStruct((nlat, d), F32),
        compiler_params=_params(1),
        name="final_norm",
    )(xs, g)


def _with_identity_rows(cos, sin, tm):
    ones = jnp.ones((tm, cos.shape[1]), F32)
    return jnp.concatenate([cos, ones], axis=0), jnp.concatenate([sin, 0.0 * ones], axis=0)


def _axial_tables(seq, tm, pad_to_lanes):
    pos = jnp.arange(seq)
    rows = (pos // GRID_W).astype(F32)
    cols = (pos % GRID_W).astype(F32)
    half = DA_HEAD_DIM // 2
    inv = ROPE_BASE ** (-jnp.arange(0, half, 2, dtype=F32) / half)
    ar = rows[:, None] * inv[None, :]
    ac = cols[:, None] * inv[None, :]
    cos64 = jnp.concatenate([jnp.cos(ar), jnp.cos(ar), jnp.cos(ac), jnp.cos(ac)], axis=1)
    sin64 = jnp.concatenate([-jnp.sin(ar), jnp.sin(ar), -jnp.sin(ac), jnp.sin(ac)], axis=1)
    if pad_to_lanes:
        cos = jnp.concatenate([cos64, jnp.ones_like(cos64)], axis=1)
        sin = jnp.concatenate([sin64, jnp.zeros_like(sin64)], axis=1)
    else:
        cos = jnp.concatenate([cos64, cos64], axis=1)
        sin = jnp.concatenate([sin64, sin64], axis=1)
    return _with_identity_rows(cos, sin, tm)


def _ret_tables(seq, tm):
    inv = ROPE_BASE ** (-jnp.arange(0, RET_KDIM, 2, dtype=F32) / RET_KDIM)
    ang = jnp.arange(seq).astype(F32)[:, None] * inv[None, :]
    return _with_identity_rows(jnp.cos(ang), jnp.sin(ang), tm)


def _mla_weights(w_in, w_q_up, w_kv_up):
    d = w_in.shape[0]
    kvw = MLA_KV_RANK + MLA_ROPE
    w_in_r = jnp.concatenate([w_in[:, :MLA_KV_RANK], w_in[:, kvw:], w_in[:, MLA_KV_RANK:kvw],
                              jnp.zeros((d, LANES - MLA_ROPE), w_in.dtype)], axis=1)
    wq = w_q_up.reshape(MLA_Q_RANK, MLA_HEADS, MLA_NOPE + MLA_ROPE)
    wq = jnp.concatenate([wq, jnp.zeros((MLA_Q_RANK, MLA_HEADS, LANES - MLA_ROPE), wq.dtype)], axis=2)
    wq = wq.reshape(MLA_Q_RANK, MLA_HEADS * 2 * LANES)
    wkv = w_kv_up.reshape(MLA_KV_RANK, MLA_HEADS, MLA_NOPE + MLA_V)
    wkv = jnp.concatenate([wkv[:, :, :MLA_NOPE].reshape(MLA_KV_RANK, -1),
                           wkv[:, :, MLA_NOPE:].reshape(MLA_KV_RANK, -1)], axis=1)
    return w_in_r.astype(BF16), wq.astype(BF16), wkv.astype(BF16)


def kernel(x, c, ctx, c_ctx, ada_w, ada_b, norm_g, final_g, da_w_in, da_w_out, da_lambda, da_subln_g,
           ret_w_in, ret_decay_logit, ret_w_out, mla_w_in, mla_q_norm_g, mla_w_q_up, mla_kv_norm_g,
           mla_w_kv_up, mla_w_out, moe_w_router, moe_b_router, moe_w_gate_up, moe_b_gate_up,
           moe_w_down, moe_b_down):
    nb, seq, d = x.shape
    nctx_per = ctx.shape[1]
    depth = ada_w.shape[0]
    nlat = nb * seq
    tm = _pick_tile(512, seq, nb * nctx_per)
    geo = dict(b=nb, seq=seq, ctx=nctx_per, nlat=nlat, tm=tm, per=seq // tm, nlt=nlat // tm)
    assert nb < MOD_ROWS and nlat % nctx_per == 0 and nctx_per % RET_CHUNK == 0

    xs = jnp.concatenate([x.reshape(nlat, d), ctx.reshape(nb * nctx_per, d)], axis=0).astype(F32)
    cond = jnp.zeros((MOD_ROWS, d), F32).at[:nb].set(c).at[nb].set(c_ctx)
    mod = _adaln(cond, ada_w, ada_b).reshape(depth, MOD_ROWS, 6, d)

    cos_a, sin_a = _axial_tables(seq, tm, pad_to_lanes=False)
    cos_m, sin_m = _axial_tables(seq, tm, pad_to_lanes=True)
    cos_r, sin_r = _ret_tables(seq, tm)

    for i in range(depth):
        kind = i % N_MIXERS
        j = i // N_MIXERS
        mod_l = mod[i]
        g1 = norm_g[i, 0].reshape(1, d)
        g2 = norm_g[i, 1].reshape(1, d)
        if kind == 0:
            lam_init = 0.8 - 0.6 * math.exp(-0.3 * i)
            p = _proj(xs, g1, mod_l, da_w_in[j].astype(BF16), cos_a, sin_a,
                      ["rope16", "plain", "rope16"], geo)
            o_lat, o_ctx = _da_attention(p, da_lambda[j].astype(F32),
                                         da_subln_g[j].reshape(1, DA_V_DIM).astype(F32), lam_init, geo)
            ctx_off = 0
            w_out = da_w_out[j]
        elif kind == 1:
            p = _proj(xs, g1, mod_l, ret_w_in[j].astype(BF16), cos_r, sin_r,
                      ["ret_k", "plain", "plain", "ret_q", "plain", "plain"], geo,
                      ret_scale=RET_KDIM ** -0.5)
            o_lat = o_ctx = _retention(p, ret_decay_logit[j], geo)
            ctx_off = geo["nlt"]
            w_out = ret_w_out[j]
        else:
            w_in_r, wq, wkv = _mla_weights(mla_w_in[j], mla_w_q_up[j], mla_w_kv_up[j])
            q, kn, v, kr = _mla_proj(xs, g1, mod_l, w_in_r, mla_q_norm_g[j].reshape(1, -1).astype(F32),
                                     mla_kv_norm_g[j].reshape(1, -1).astype(F32), wq, wkv, cos_m, sin_m, geo)
            o_lat, o_ctx = _mla_attention(q, kn, v, kr, geo)
            ctx_off = 0
            w_out = mla_w_out[j]
        xs, h2, top_idx, gates, rank, counts = _post(o_lat, o_ctx, ctx_off, w_out.astype(BF16), xs, mod_l, g2,
                                                     moe_w_router[i], moe_b_router[i], geo)
        xs = _moe(h2, top_idx, gates, rank, counts[0, :N_EXPERTS], xs, mod_l,
                  moe_w_gate_up[i].astype(BF16), moe_b_gate_up[i].astype(F32),
                  moe_w_down[i].astype(BF16), moe_b_down[i].astype(F32), geo)
    return _final_norm(xs, final_g.reshape(1, d).astype(F32), nlat, tm).reshape(nb, seq, d)
```

```python
import functools
import math

import jax
import jax.numpy as jnp
from jax import lax
from jax.experimental import pallas as pl
from jax.experimental.pallas import tpu as pltpu

F32 = jnp.float32
BF16 = jnp.bfloat16

D_MODEL = 1024
GRID_W = 64
RMS_EPS = 1e-6
ROPE_BASE = 10000.0
N_MIXERS = 3

DA_HEADS = 8
DA_HEAD_DIM = 64
DA_V_DIM = 128

RET_HEADS = 4
RET_KDIM = 256
RET_VDIM = 512
RET_CHUNK = 128

MLA_HEADS = 8
MLA_NOPE = 128
MLA_ROPE = 64
MLA_V = 128
MLA_Q_RANK = 256
MLA_KV_RANK = 128

N_EXPERTS = 32
TOP_K = 4
D_EXPERT = 1024
SWIGLU_LIMIT = 7.0
SWIGLU_ALPHA = 1.702

LANES = 128
MOD_ROWS = 16
MOE_BLOCK = 512
COL_CHUNK = 1024
ATTN_TQ = 1024
ATTN_TK = 1024
VMEM_LIMIT = 56 * 1024 * 1024
NEG_BIG = -1e30


def _params(n_axes):
    return pltpu.CompilerParams(dimension_semantics=("arbitrary",) * n_axes,
                                vmem_limit_bytes=VMEM_LIMIT)


def _pick_tile(cap, *dims):
    t = cap
    while any(d % t for d in dims):
        t //= 2
    return t


def _rms(x):
    return x * lax.rsqrt(jnp.mean(x * x, axis=-1, keepdims=True) + RMS_EPS)


def _norm_mod(x, g, shift, scale):
    return (_rms(x) * g) * (1.0 + scale) + shift


def _dot(a, b):
    return jnp.dot(a, b, preferred_element_type=F32)


def _dot_nt(a, b):
    return lax.dot_general(a, b, (((1,), (1,)), ((), ())), preferred_element_type=F32)


def _dot_tn(a, b):
    return lax.dot_general(a, b, (((0,), (0,)), ((), ())), preferred_element_type=F32)


def _adaln_kernel(c_ref, w_ref, b_ref, o_ref):
    c = c_ref[...]
    a = c * (1.0 / (1.0 + jnp.exp(-c)))
    o_ref[...] = jnp.dot(a, w_ref[...], precision=lax.Precision.HIGHEST,
                         preferred_element_type=F32) + b_ref[...]


def _adaln(cond, ada_w, ada_b):
    depth, d, n = ada_w.shape
    tn = COL_CHUNK
    return pl.pallas_call(
        _adaln_kernel,
        grid=(depth, n // tn),
        in_specs=[pl.BlockSpec((MOD_ROWS, d), lambda l, j: (0, 0)),
                  pl.BlockSpec((None, d, tn), lambda l, j: (l, 0, j)),
                  pl.BlockSpec((None, 1, tn), lambda l, j: (l, 0, j))],
        out_specs=pl.BlockSpec((None, MOD_ROWS, tn), lambda l, j: (l, 0, j)),
        out_shape=jax.ShapeDtypeStruct((depth, MOD_ROWS, n), F32),
        compiler_params=_params(2),
        name="adaln",
    )(cond, ada_w, ada_b.reshape(depth, 1, n))


def _rope_pairs16(x, cos, sin):
    lane = lax.broadcasted_iota(jnp.int32, x.shape, 1)
    first = (lane % 32) < 16
    partner = jnp.where(first, pltpu.roll(x, LANES - 16, 1), pltpu.roll(x, 16, 1))
    return x * cos + partner * sin


def _proj_kernel(x_ref, g_ref, mod_ref, w_ref, cos_ref, sin_ref, o_ref, h_scr, *, kinds, ret_scale):
    j = pl.program_id(1)

    @pl.when(j == 0)
    def _():
        h = _norm_mod(x_ref[...], g_ref[...], mod_ref[0:1, :], mod_ref[1:2, :])
        h_scr[...] = h.astype(BF16)

    def chunks_of(kind):
        cs = [c for c, k in enumerate(kinds) if k == kind]
        pred = None
        for c in cs:
            pred = (j == c) if pred is None else jnp.logical_or(pred, j == c)
        return pred

    def acc():
        return _dot(h_scr[...], w_ref[...])

    if "plain" in kinds:
        @pl.when(chunks_of("plain"))
        def _():
            o_ref[...] = acc().astype(BF16)

    if "rope16" in kinds:
        @pl.when(chunks_of("rope16"))
        def _():
            a = acc()
            cos = cos_ref[...]
            sin = sin_ref[...]
            for g in range(COL_CHUNK // LANES):
                sl = slice(g * LANES, (g + 1) * LANES)
                o_ref[:, sl] = _rope_pairs16(a[:, sl], cos, sin).astype(BF16)

    for kind, scale in (("ret_k", ret_scale), ("ret_q", 1.0)):
        if kind in kinds:
            @pl.when(chunks_of(kind))
            def _(scale=scale):
                a = acc()
                cos = cos_ref[...]
                sin = sin_ref[...]
                for hh in range(COL_CHUNK // RET_KDIM):
                    lo = slice(hh * RET_KDIM, hh * RET_KDIM + LANES)
                    hi = slice(hh * RET_KDIM + LANES, (hh + 1) * RET_KDIM)
                    x1 = a[:, lo]
                    x2 = a[:, hi]
                    o_ref[:, lo] = ((x1 * cos - x2 * sin) * scale).astype(BF16)
                    o_ref[:, hi] = ((x1 * sin + x2 * cos) * scale).astype(BF16)


def _proj(xs, g, mod_l, w, cos_t, sin_t, kinds, geo, ret_scale=1.0):
    nt, d = xs.shape
    n = w.shape[1]
    tm, per, nlt, nb = geo["tm"], geo["per"], geo["nlt"], geo["b"]
    return pl.pallas_call(
        functools.partial(_proj_kernel, kinds=tuple(kinds), ret_scale=ret_scale),
        grid=(nt // tm, n // COL_CHUNK),
        in_specs=[pl.BlockSpec((tm, d), lambda i, j: (i, 0)),
                  pl.BlockSpec((1, d), lambda i, j: (0, 0)),
                  pl.BlockSpec((None, 6, d), lambda i, j: (jnp.minimum(i // per, nb), 0, 0)),
                  pl.BlockSpec((d, COL_CHUNK), lambda i, j: (0, j)),
                  pl.BlockSpec((tm, LANES), lambda i, j: (jnp.where(i < nlt, i % per, per), 0)),
                  pl.BlockSpec((tm, LANES), lambda i, j: (jnp.where(i < nlt, i % per, per), 0))],
        out_specs=pl.BlockSpec((tm, COL_CHUNK), lambda i, j: (i, j)),
        out_shape=jax.ShapeDtypeStruct((nt, n), BF16),
        scratch_shapes=[pltpu.VMEM((tm, d), BF16)],
        compiler_params=_params(2),
        name="proj",
    )(xs, g, mod_l, w, cos_t, sin_t)


def _mla_proj_kernel(x_ref, g_ref, mod_ref, win_ref, qg_ref, kvg_ref, wq_ref, wkv_ref, cos_ref, sin_ref,
                     q_ref, kn_ref, v_ref, kr_ref):
    h = _norm_mod(x_ref[...], g_ref[...], mod_ref[0:1, :], mod_ref[1:2, :]).astype(BF16)
    p = _dot(h, win_ref[...])
    cos = cos_ref[...]
    sin = sin_ref[...]
    ckv = (_rms(p[:, :MLA_KV_RANK]) * kvg_ref[...]).astype(BF16)
    kv = _dot(ckv, wkv_ref[...])
    nk = MLA_HEADS * MLA_NOPE
    kn_ref[...] = kv[:, :nk].astype(BF16)
    v_ref[...] = kv[:, nk:].astype(BF16)
    kr_ref[...] = _rope_pairs16(p[:, MLA_KV_RANK + MLA_Q_RANK:], cos, sin).astype(BF16)
    cq = (_rms(p[:, MLA_KV_RANK:MLA_KV_RANK + MLA_Q_RANK]) * qg_ref[...]).astype(BF16)
    q = _dot(cq, wq_ref[...])
    for hh in range(MLA_HEADS):
        lo = slice(hh * 2 * LANES, hh * 2 * LANES + LANES)
        hi = slice(hh * 2 * LANES + LANES, (hh + 1) * 2 * LANES)
        q_ref[:, lo] = q[:, lo].astype(BF16)
        q_ref[:, hi] = _rope_pairs16(q[:, hi], cos, sin).astype(BF16)


def _mla_proj(xs, g, mod_l, w_in, qg, kvg, wq, wkv, cos_t, sin_t, geo):
    nt, d = xs.shape
    tm, per, nlt, nb = geo["tm"], geo["per"], geo["nlt"], geo["b"]
    full = lambda a: pl.BlockSpec(a.shape, lambda i: (0,) * a.ndim)
    tab = pl.BlockSpec((tm, LANES), lambda i: (jnp.where(i < nlt, i % per, per), 0))
    nq = MLA_HEADS * 2 * LANES
    nk = MLA_HEADS * MLA_NOPE
    return pl.pallas_call(
        _mla_proj_kernel,
        grid=(nt // tm,),
        in_specs=[pl.BlockSpec((tm, d), lambda i: (i, 0)), full(g),
                  pl.BlockSpec((None, 6, d), lambda i: (jnp.minimum(i // per, nb), 0, 0)),
                  full(w_in), full(qg), full(kvg), full(wq), full(wkv), tab, tab],
        out_specs=[pl.BlockSpec((tm, nq), lambda i: (i, 0)),
                   pl.BlockSpec((tm, nk), lambda i: (i, 0)),
                   pl.BlockSpec((tm, nk), lambda i: (i, 0)),
                   pl.BlockSpec((tm, LANES), lambda i: (i, 0))],
        out_shape=[jax.ShapeDtypeStruct((nt, nq), BF16), jax.ShapeDtypeStruct((nt, nk), BF16),
                   jax.ShapeDtypeStruct((nt, nk), BF16), jax.ShapeDtypeStruct((nt, LANES), BF16)],
        compiler_params=_params(1),
        name="mla_proj",
    )(xs, g, mod_l, w_in, qg, kvg, wq, wkv, cos_t, sin_t)


LOG2E = 1.4426950408889634


def _flash_init(m_ref, a_ref):
    m_ref[...] = jnp.full(m_ref.shape, -jnp.inf, F32)
    a_ref[...] = jnp.zeros(a_ref.shape, F32)


def _ones_column(rows):
    lane = lax.broadcasted_iota(jnp.int32, (rows, LANES), 1)
    return jnp.where(lane == 0, 1.0, 0.0).astype(BF16)


def _chunk_rows(j, tk):
    if isinstance(j, int):
        return pl.ds(j * tk, tk)
    return pl.ds(pl.multiple_of(j * tk, tk), tk)


def _softmax_pv(parts, stats):
    for mp, (m_ref, a_ref) in enumerate(stats):
        tiles = [scores[mp] for scores, _ in parts]
        m_prev = m_ref[...]
        m_new = m_prev
        for s2 in tiles:
            m_new = jnp.maximum(m_new, jnp.max(s2, axis=1, keepdims=True))
        acc = jnp.exp2(m_prev - m_new) * a_ref[...]
        for s2, (_, v_aug) in zip(tiles, parts):
            acc = acc + _dot(jnp.exp2(s2 - m_new).astype(BF16), v_aug)
        a_ref[...] = acc
        m_ref[...] = m_new


def _flash_lat(qk_fns, stats, k_ctx, v_ctx, k_at, v_at, s_a, s_b, s_c, n_chunks):
    def qk_into(dst, k):
        for mp, f in enumerate(qk_fns):
            dst[mp] = f(k)

    qk_into(s_c, k_ctx)
    qk_into(s_a, k_at(0))
    qk_into(s_b, k_at(1))
    _softmax_pv([(s_c, v_ctx), (s_a, v_at(0))], stats)

    def body(t, carry):
        j = 2 * t + 1
        qk_into(s_a, k_at(j + 1))
        _softmax_pv([(s_b, v_at(j))], stats)
        qk_into(s_b, k_at(j + 2))
        _softmax_pv([(s_a, v_at(j + 1))], stats)
        return carry

    lax.fori_loop(0, (n_chunks - 2) // 2, body, 0)
    _softmax_pv([(s_b, v_at(n_chunks - 1))], stats)


def _flash_result(a_ref, vdim):
    a = a_ref[...]
    return a[:, :vdim] / a[:, vdim:vdim + 1]


def _da_attn_kernel(*refs, n_chunks, tk, lam_init, scale):
    if n_chunks:
        (lam_ref, sg_ref, q_ref, kc_ref, vc_ref, kl_ref, vl_ref, o_ref, m0, a0, m1, a1, s_a, s_b, s_c) = refs
    else:
        (lam_ref, sg_ref, q_ref, kc_ref, vc_ref, o_ref, m0, a0, m1, a1) = refs
    q = q_ref[...]
    lane = lax.broadcasted_iota(jnp.int32, q.shape, 1)
    zero = jnp.zeros_like(q)
    q_lo = jnp.where(lane < DA_HEAD_DIM, q, zero)
    q_hi = jnp.where(lane >= DA_HEAD_DIM, q, zero)
    _flash_init(m0, a0)
    _flash_init(m1, a1)
    c2 = scale * LOG2E
    stats = [(m0, a0), (m1, a1)]
    qk_fns = [lambda k: _dot_nt(q_lo, k) * c2, lambda k: _dot_nt(q_hi, k) * c2]
    v_ctx = jnp.concatenate([vc_ref[...], _ones_column(vc_ref.shape[0])], axis=1)
    if n_chunks:
        ones = _ones_column(tk)

        def k_at(j):
            return kl_ref[_chunk_rows(j, tk), :]

        def v_at(j):
            return jnp.concatenate([vl_ref[_chunk_rows(j, tk), :], ones], axis=1)

        _flash_lat(qk_fns, stats, kc_ref[...], v_ctx, k_at, v_at, s_a, s_b, s_c, n_chunks)
    else:
        _softmax_pv([([f(kc_ref[...]) for f in qk_fns], v_ctx)], stats)

    lf = lam_ref[...]
    lam = (jnp.exp(jnp.sum(lf[0:1] * lf[1:2], axis=1, keepdims=True))
           - jnp.exp(jnp.sum(lf[2:3] * lf[3:4], axis=1, keepdims=True)) + lam_init)
    o = _flash_result(a0, DA_V_DIM) - lam * _flash_result(a1, DA_V_DIM)
    o = _rms(o) * sg_ref[...] * (1.0 - lam_init)
    o_ref[...] = o.astype(BF16)


def _attn_scratch(tq, vdim, n_maps, tk=0, ctx=0):
    s = []
    for _ in range(n_maps):
        s += [pltpu.VMEM((tq, 1), F32), pltpu.VMEM((tq, vdim + LANES), F32)]
    if tk:
        s += [pltpu.VMEM((n_maps, tq, tk), F32), pltpu.VMEM((n_maps, tq, tk), F32),
              pltpu.VMEM((n_maps, tq, ctx), F32)]
    return s


def _da_attention(p, lam_vecs, subg, lam_init, geo):
    nt = p.shape[0]
    nb, seq, ctx, nlat = geo["b"], geo["seq"], geo["ctx"], geo["nlat"]
    hh = DA_HEADS
    tq = _pick_tile(ATTN_TQ, seq)
    tk = _pick_tile(min(ATTN_TK, seq // 2), seq)
    assert (seq // tk) % 2 == 0
    nqt = seq // tq
    cb = nlat // ctx
    scale = DA_HEAD_DIM ** -0.5
    small = [pl.BlockSpec(lam_vecs.shape, lambda *a: (0, 0)), pl.BlockSpec(subg.shape, lambda *a: (0, 0))]
    o_lat = pl.pallas_call(
        functools.partial(_da_attn_kernel, n_chunks=seq // tk, tk=tk, lam_init=lam_init, scale=scale),
        grid=(nb, hh, nqt),
        in_specs=small + [
            pl.BlockSpec((tq, LANES), lambda b, h, i: (b * nqt + i, 2 * hh + h)),
            pl.BlockSpec((ctx, LANES), lambda b, h, i: (cb + b, h)),
            pl.BlockSpec((ctx, LANES), lambda b, h, i: (cb + b, hh + h)),
            pl.BlockSpec((seq, LANES), lambda b, h, i: (b, h)),
            pl.BlockSpec((seq, LANES), lambda b, h, i: (b, hh + h))],
        out_specs=pl.BlockSpec((tq, LANES), lambda b, h, i: (b * nqt + i, h)),
        out_shape=jax.ShapeDtypeStruct((nlat, hh * DA_V_DIM), BF16),
        scratch_shapes=_attn_scratch(tq, DA_V_DIM, 2, tk, ctx),
        compiler_params=_params(3),
        name="da_attn_lat",
    )(lam_vecs, subg, p, p, p, p, p)
    o_ctx = pl.pallas_call(
        functools.partial(_da_attn_kernel, n_chunks=0, tk=tk, lam_init=lam_init, scale=scale),
        grid=(nb, hh),
        in_specs=small + [
            pl.BlockSpec((ctx, LANES), lambda b, h: (cb + b, 2 * hh + h)),
            pl.BlockSpec((ctx, LANES), lambda b, h: (cb + b, h)),
            pl.BlockSpec((ctx, LANES), lambda b, h: (cb + b, hh + h))],
        out_specs=pl.BlockSpec((ctx, LANES), lambda b, h: (b, h)),
        out_shape=jax.ShapeDtypeStruct((nt - nlat, hh * DA_V_DIM), BF16),
        scratch_shapes=_attn_scratch(ctx, DA_V_DIM, 2),
        compiler_params=_params(2),
        name="da_attn_ctx",
    )(lam_vecs, subg, p, p, p)
    return o_lat, o_ctx


def _mla_attn_kernel(*refs, n_chunks, tk, scale):
    if n_chunks:
        (q_ref, knc_ref, krc_ref, vc_ref, knl_ref, krl_ref, vl_ref, o_ref, m0, a0, s_a, s_b, s_c) = refs
    else:
        (q_ref, knc_ref, krc_ref, vc_ref, o_ref, m0, a0) = refs
    q = q_ref[...]
    _flash_init(m0, a0)
    c2 = scale * LOG2E
    stats = [(m0, a0)]
    qk_fns = [lambda k: _dot_nt(q, k) * c2]
    k_ctx = jnp.concatenate([knc_ref[...], krc_ref[...]], axis=1)
    v_ctx = jnp.concatenate([vc_ref[...], _ones_column(vc_ref.shape[0])], axis=1)
    if n_chunks:
        ones = _ones_column(tk)

        def k_at(j):
            rows = _chunk_rows(j, tk)
            return jnp.concatenate([knl_ref[rows, :], krl_ref[rows, :]], axis=1)

        def v_at(j):
            return jnp.concatenate([vl_ref[_chunk_rows(j, tk), :], ones], axis=1)

        _flash_lat(qk_fns, stats, k_ctx, v_ctx, k_at, v_at, s_a, s_b, s_c, n_chunks)
    else:
        _softmax_pv([([qk_fns[0](k_ctx)], v_ctx)], stats)
    o_ref[...] = _flash_result(a0, MLA_V).astype(BF16)


def _mla_attention(q, kn, v, kr, geo):
    nt = q.shape[0]
    nb, seq, ctx, nlat = geo["b"], geo["seq"], geo["ctx"], geo["nlat"]
    hh = MLA_HEADS
    tq = _pick_tile(ATTN_TQ, seq)
    tk = _pick_tile(min(ATTN_TK, seq // 2), seq)
    assert (seq // tk) % 2 == 0
    nqt = seq // tq
    cb = nlat // ctx
    scale = (MLA_NOPE + MLA_ROPE) ** -0.5
    o_lat = pl.pallas_call(
        functools.partial(_mla_attn_kernel, n_chunks=seq // tk, tk=tk, scale=scale),
        grid=(nb, hh, nqt),
        in_specs=[
            pl.BlockSpec((tq, 2 * LANES), lambda b, h, i: (b * nqt + i, h)),
            pl.BlockSpec((ctx, LANES), lambda b, h, i: (cb + b, h)),
            pl.BlockSpec((ctx, LANES), lambda b, h, i: (cb + b, 0)),
            pl.BlockSpec((ctx, LANES), lambda b, h, i: (cb + b, h)),
            pl.BlockSpec((seq, LANES), lambda b, h, i: (b, h)),
            pl.BlockSpec((seq, LANES), lambda b, h, i: (b, 0)),
            pl.BlockSpec((seq, LANES), lambda b, h, i: (b, h))],
        out_specs=pl.BlockSpec((tq, LANES), lambda b, h, i: (b * nqt + i, h)),
        out_shape=jax.ShapeDtypeStruct((nlat, hh * MLA_V), BF16),
        scratch_shapes=_attn_scratch(tq, MLA_V, 1, tk, ctx),
        compiler_params=_params(3),
        name="mla_attn_lat",
    )(q, kn, kr, v, kn, kr, v)
    o_ctx = pl.pallas_call(
        functools.partial(_mla_attn_kernel, n_chunks=0, tk=tk, scale=scale),
        grid=(nb, hh),
        in_specs=[
            pl.BlockSpec((ctx, 2 * LANES), lambda b, h: (cb + b, h)),
            pl.BlockSpec((ctx, LANES), lambda b, h: (cb + b, h)),
            pl.BlockSpec((ctx, LANES), lambda b, h: (cb + b, 0)),
            pl.BlockSpec((ctx, LANES), lambda b, h: (cb + b, h))],
        out_specs=pl.BlockSpec((ctx, LANES), lambda b, h: (b, h)),
        out_shape=jax.ShapeDtypeStruct((nt - nlat, hh * MLA_V), BF16),
        scratch_shapes=_attn_scratch(ctx, MLA_V, 1),
        compiler_params=_params(2),
        name="mla_attn_ctx",
    )(q, kn, kr, v)
    return o_lat, o_ctx


def _ret_kernel(*refs, reverse):
    if reverse:
        dl_ref, q_ref, k_ref, v0_ref, v1_ref, of_ref, g0_ref, g1_ref, o_ref, st = refs
    else:
        dl_ref, q_ref, k_ref, v0_ref, v1_ref, o_ref, st = refs
    s = pl.program_id(1)

    @pl.when(s == 0)
    def _():
        st[...] = jnp.zeros(st.shape, F32)

    dl = dl_ref[...]
    lsig = jnp.minimum(dl, 0.0) - jnp.log(1.0 + jnp.exp(-jnp.abs(dl)))
    row = lax.broadcasted_iota(jnp.int32, dl.shape, 0)
    col = lax.broadcasted_iota(jnp.int32, dl.shape, 1)
    c = RET_CHUNK
    i2 = lax.broadcasted_iota(jnp.int32, (c, c), 0)
    j2 = lax.broadcasted_iota(jnp.int32, (c, c), 1)
    pos = lax.broadcasted_iota(jnp.int32, (c, 1), 0).astype(F32)
    dist = ((j2 - i2) if reverse else (i2 - j2)).astype(F32)
    keep = dist >= 0.0
    dist = jnp.where(keep, dist, 0.0)
    per_blk = COL_CHUNK // RET_VDIM

    for hd in range(RET_HEADS):
        pick = jnp.logical_and(row == (1 if reverse else 0), col == hd)
        lg = jnp.sum(jnp.sum(jnp.where(pick, lsig, 0.0), axis=1, keepdims=True), axis=0, keepdims=True)
        if reverse:
            q_dec = jnp.exp((c - pos) * lg)
            k_dec = jnp.exp(pos * lg)
        else:
            q_dec = jnp.exp((pos + 1.0) * lg)
            k_dec = jnp.exp((c - 1.0 - pos) * lg)
        intra = jnp.where(keep, jnp.exp(dist * lg), 0.0)
        chunk_dec = jnp.exp(c * lg)

        ksl = slice(hd * RET_KDIM, (hd + 1) * RET_KDIM)
        vsl = slice((hd % per_blk) * RET_VDIM, (hd % per_blk + 1) * RET_VDIM)
        osl = slice(hd * RET_VDIM, (hd + 1) * RET_VDIM)
        qb = q_ref[:, ksl]
        kb = k_ref[:, ksl]
        vb = (v0_ref if hd < per_blk else v1_ref)[:, vsl]
        state = st[hd]
        sc = _dot_nt(qb, kb) * intra
        o = _dot(sc.astype(BF16), vb) + _dot((qb.astype(F32) * q_dec).astype(BF16), state.astype(BF16))
        st[hd] = chunk_dec * state + _dot_tn((kb.astype(F32) * k_dec).astype(BF16), vb)
        if reverse:
            tot = _rms(of_ref[:, osl] + o)
            g = (g0_ref if hd < per_blk else g1_ref)[:, vsl].astype(F32)
            o_ref[:, osl] = (g * (1.0 / (1.0 + jnp.exp(-g))) * tot).astype(BF16)
        else:
            o_ref[:, osl] = o


def _retention(p, decay_logit, geo):
    nt = p.shape[0]
    nb, seq, ctx, nlat = geo["b"], geo["seq"], geo["ctx"], geo["nlat"]
    c = RET_CHUNK
    ncc, ncl = ctx // c, seq // c
    hh = RET_HEADS
    kq, kv = RET_KDIM, RET_VDIM
    wb = COL_CHUNK
    assert hh * kq == wb and hh * kv == 2 * wb
    dl = jnp.zeros((8, LANES), F32).at[:2, :hh].set(decay_logit.astype(F32))

    def rows_fwd(b, s):
        return jnp.where(s < ncc, nlat // c + b * ncc + s, b * ncl + (s - ncc))

    def rows_bwd(b, s):
        return jnp.where(s < ncc, nlat // c + b * ncc + (ncc - 1 - s), b * ncl + (ncl - 1 - (s - ncc)))

    def blk(rows, col):
        return pl.BlockSpec((c, wb), lambda b, s: (rows(b, s), col))

    def specs(rows):
        return [pl.BlockSpec((8, LANES), lambda b, s: (0, 0)), blk(rows, 3), blk(rows, 0), blk(rows, 1), blk(rows, 2)]

    grid = (nb, ncc + ncl)
    scratch = [pltpu.VMEM((hh, kq, kv), F32)]
    o_f = pl.pallas_call(
        functools.partial(_ret_kernel, reverse=False),
        grid=grid,
        in_specs=specs(rows_fwd),
        out_specs=pl.BlockSpec((c, hh * kv), lambda b, s: (rows_fwd(b, s), 0)),
        out_shape=jax.ShapeDtypeStruct((nt, hh * kv), F32),
        scratch_shapes=scratch,
        compiler_params=_params(2),
        name="ret_fwd",
    )(dl, p, p, p, p)
    return pl.pallas_call(
        functools.partial(_ret_kernel, reverse=True),
        grid=grid,
        in_specs=specs(rows_bwd) + [
            pl.BlockSpec((c, hh * kv), lambda b, s: (rows_bwd(b, s), 0)), blk(rows_bwd, 4), blk(rows_bwd, 5)],
        out_specs=pl.BlockSpec((c, hh * kv), lambda b, s: (rows_bwd(b, s), 0)),
        out_shape=jax.ShapeDtypeStruct((nt, hh * kv), BF16),
        scratch_shapes=scratch,
        compiler_params=_params(2),
        name="ret_bwd",
    )(dl, p, p, p, p, o_f, p, p)


def _post_kernel(ol_ref, oc_ref, w_ref, x_ref, mod_ref, g2_ref, wr_ref, br_ref, tri_ref,
                 xo_ref, h2_ref, idx_ref, gate_ref, rank_ref, cnt_ref, cnt_scr, *, nlt):
    t = pl.program_id(0)

    @pl.when(t == 0)
    def _():
        cnt_scr[...] = jnp.zeros(cnt_scr.shape, F32)

    o = jnp.where(t < nlt, ol_ref[...], oc_ref[...])
    y = _dot(o, w_ref[...])
    x = x_ref[...] + mod_ref[2:3, :] * y
    xo_ref[...] = x
    h2 = _norm_mod(x, g2_ref[...], mod_ref[3:4, :], mod_ref[4:5, :])
    h2_ref[...] = h2
    logits = jnp.dot(h2, wr_ref[...], precision=lax.Precision.HIGHEST,
                     preferred_element_type=F32) + br_ref[...]
    lane = lax.broadcasted_iota(jnp.int32, logits.shape, 1).astype(F32)
    work = logits
    vals, idxs = [], []
    for _ in range(TOP_K):
        mx = jnp.max(work, axis=1, keepdims=True)
        ix = jnp.min(jnp.where(work == mx, lane, float(LANES)), axis=1, keepdims=True)
        vals.append(mx)
        idxs.append(ix)
        work = jnp.where(lane == ix, -jnp.inf, work)
    es = [jnp.exp(v - vals[0]) for v in vals]
    den = es[0] + es[1] + es[2] + es[3]
    onehots = [lane == ix for ix in idxs]
    oh = jnp.zeros(logits.shape, F32)
    for m in onehots:
        oh = oh + jnp.where(m, 1.0, 0.0)
    before = _dot(tri_ref[...], oh.astype(BF16)) + cnt_scr[0:1, :]
    idx_out = jnp.zeros(logits.shape, F32)
    gate_out = jnp.zeros(logits.shape, F32)
    rank_out = jnp.zeros(logits.shape, F32)
    for r in range(TOP_K):
        sel = lane == float(r)
        rk = jnp.sum(jnp.where(onehots[r], before, 0.0), axis=1, keepdims=True)
        idx_out = jnp.where(sel, idxs[r], idx_out)
        gate_out = jnp.where(sel, es[r] / den, gate_out)
        rank_out = jnp.where(sel, rk, rank_out)
    idx_ref[...] = idx_out.astype(jnp.int32)
    gate_ref[...] = gate_out
    rank_ref[...] = rank_out.astype(jnp.int32)
    cnt_scr[0:1, :] = cnt_scr[0:1, :] + jnp.sum(oh, axis=0, keepdims=True)
    cnt_ref[...] = cnt_scr[...]


def _post(o_lat, o_ctx, ctx_off, w_out, xs, mod_l, g2, w_router, b_router, geo):
    nt, d = xs.shape
    ko = o_lat.shape[1]
    tm, per, nb, nlt = geo["tm"], geo["per"], geo["b"], geo["nlt"]
    ii = lax.broadcasted_iota(jnp.int32, (tm, tm), 0)
    jj = lax.broadcasted_iota(jnp.int32, (tm, tm), 1)
    tri = (jj < ii).astype(BF16)
    wr = jnp.zeros((d, LANES), F32).at[:, :N_EXPERTS].set(w_router.astype(F32))
    br = jnp.full((1, LANES), NEG_BIG, F32).at[0, :N_EXPERTS].set(b_router.astype(F32))
    row = lambda w: pl.BlockSpec((tm, w), lambda i: (i, 0))
    full = lambda a: pl.BlockSpec(a.shape, lambda i: (0,) * a.ndim)
    return pl.pallas_call(
        functools.partial(_post_kernel, nlt=nlt),
        grid=(nt // tm,),
        in_specs=[pl.BlockSpec((tm, ko), lambda i: (jnp.minimum(i, nlt - 1), 0)),
                  pl.BlockSpec((tm, ko), lambda i: (jnp.maximum(i - nlt, 0) + ctx_off, 0)),
                  full(w_out), row(d),
                  pl.BlockSpec((None, 6, d), lambda i: (jnp.minimum(i // per, nb), 0, 0)),
                  full(g2), full(wr), full(br), full(tri)],
        out_specs=[row(d), row(d), row(LANES), row(LANES), row(LANES),
                   pl.BlockSpec((8, LANES), lambda i: (0, 0))],
        out_shape=[jax.ShapeDtypeStruct((nt, d), F32), jax.ShapeDtypeStruct((nt, d), F32),
                   jax.ShapeDtypeStruct((nt, LANES), jnp.int32), jax.ShapeDtypeStruct((nt, LANES), F32),
                   jax.ShapeDtypeStruct((nt, LANES), jnp.int32), jax.ShapeDtypeStruct((8, LANES), F32)],
        scratch_shapes=[pltpu.VMEM((8, LANES), F32)],
        input_output_aliases={3: 0},
        compiler_params=_params(1),
        name="post",
    )(o_lat, o_ctx, w_out, xs, mod_l, g2, wr, br, tri)


def _row_copy(src, dst, sem):
    return pltpu.make_async_copy(src, dst, sem)


def _dispatch_kernel(zs_ref, zc_ref, slot_ref, h_ref, xs_ref, zero_scr, sem, *, tm):
    t = pl.program_id(0)

    @pl.when(t == 0)
    def _():
        zero_scr[...] = jnp.zeros(zero_scr.shape, F32)

        def per_expert(e, carry):
            start = zs_ref[e]
            n = zc_ref[e]

            def issue(r, c2):
                _row_copy(zero_scr.at[pl.ds(0, 1)], xs_ref.at[pl.ds(start + r, 1)], sem).start()
                return c2

            lax.fori_loop(0, n, issue, 0)

            def drain(r, c2):
                _row_copy(zero_scr.at[pl.ds(0, 1)], xs_ref.at[pl.ds(0, 1)], sem).wait()
                return c2

            lax.fori_loop(0, n, drain, 0)
            return carry

        lax.fori_loop(0, N_EXPERTS, per_expert, 0)

    def issue(r, carry):
        for k in range(TOP_K):
            s = slot_ref[r * TOP_K + k]
            _row_copy(h_ref.at[pl.ds(r, 1)], xs_ref.at[pl.ds(s, 1)], sem).start(priority=k % 2)
        return carry

    lax.fori_loop(0, tm, issue, 0)

    def drain(r, carry):
        for k in range(TOP_K):
            _row_copy(h_ref.at[pl.ds(0, 1)], xs_ref.at[pl.ds(0, 1)], sem).wait()
        return carry

    lax.fori_loop(0, tm, drain, 0)


def _dispatch(h2, slot_flat, zstart, zcount, n_slots, geo):
    nt, d = h2.shape
    tm = geo["tm"]
    return pl.pallas_call(
        functools.partial(_dispatch_kernel, tm=tm),
        grid_spec=pltpu.PrefetchScalarGridSpec(
            num_scalar_prefetch=2,
            grid=(nt // tm,),
            in_specs=[pl.BlockSpec((tm * TOP_K,), lambda i, zs, zc: (i,), memory_space=pltpu.SMEM),
                      pl.BlockSpec((tm, d), lambda i, zs, zc: (i, 0))],
            out_specs=pl.BlockSpec(memory_space=pl.ANY),
            scratch_shapes=[pltpu.VMEM((8, d), F32), pltpu.SemaphoreType.DMA(())]),
        out_shape=jax.ShapeDtypeStruct((n_slots, d), F32),
        compiler_params=_params(1),
        name="moe_dispatch",
    )(zstart, zcount, slot_flat, h2)


def _expert_kernel(be_ref, nu_ref, x_ref, wgu_ref, bgu_ref, wd_ref, bd_ref, y_ref, wgu_bf, wd_bf):
    i = pl.program_id(0)

    @pl.when(jnp.logical_or(i == 0, be_ref[i] != be_ref[jnp.maximum(i - 1, 0)]))
    def _():
        wgu_bf[...] = wgu_ref[...].astype(BF16)
        wd_bf[...] = wd_ref[...].astype(BF16)

    @pl.when(i < nu_ref[0])
    def _():
        x = x_ref[...].astype(BF16)
        gu = _dot(x, wgu_bf[...]) + bgu_ref[...]
        glu = jnp.minimum(gu[:, :D_EXPERT], SWIGLU_LIMIT)
        lin = jnp.clip(gu[:, D_EXPERT:], -SWIGLU_LIMIT, SWIGLU_LIMIT)
        act = glu * (1.0 / (1.0 + jnp.exp(-SWIGLU_ALPHA * glu))) * (lin + 1.0)
        y_ref[...] = _dot(act.astype(BF16), wd_bf[...]) + bd_ref[...]


def _experts(xs, block_expert, n_used, layer, wgu, bgu, wd, bd):
    ns, d = xs.shape
    nblk = ns // MOE_BLOCK
    depth, ne, _, f2 = wgu.shape
    f = wd.shape[2]
    blk = lambda i, be, nu: (jnp.minimum(i, nu[0] - 1), 0)
    wsel = lambda i, be, nu: (layer, be[i], 0, 0)
    return pl.pallas_call(
        _expert_kernel,
        grid_spec=pltpu.PrefetchScalarGridSpec(
            num_scalar_prefetch=2,
            grid=(nblk,),
            in_specs=[pl.BlockSpec((MOE_BLOCK, d), blk),
                      pl.BlockSpec((None, None, d, f2), wsel),
                      pl.BlockSpec((None, None, 1, f2), wsel),
                      pl.BlockSpec((None, None, f, d), wsel),
                      pl.BlockSpec((None, None, 1, d), wsel)],
            out_specs=pl.BlockSpec((MOE_BLOCK, d), blk),
            scratch_shapes=[pltpu.VMEM((d, f2), BF16), pltpu.VMEM((f, d), BF16)]),
        out_shape=jax.ShapeDtypeStruct((ns, d), F32),
        compiler_params=_params(1),
        name="moe_experts",
    )(block_expert, n_used, xs, wgu, bgu.reshape(depth, ne, 1, f2), wd, bd.reshape(depth, ne, 1, d))


def _combine_kernel(slot_ref, next_slot_ref, gate_ref, x_ref, mod_ref, y_ref, xo_ref, ybuf, sems, *, tm):
    t = pl.program_id(0)
    cur = t % 2

    def gather(slots, buf):
        def issue(r, carry):
            for k in range(TOP_K):
                s = slots[r * TOP_K + k]
                _row_copy(y_ref.at[pl.ds(s, 1)], ybuf.at[buf, k, pl.ds(r, 1)], sems.at[buf]).start(priority=k % 2)
            return carry
        lax.fori_loop(0, tm, issue, 0)

    @pl.when(t == 0)
    def _():
        gather(slot_ref, 0)

    @pl.when(t + 1 < pl.num_programs(0))
    def _():
        gather(next_slot_ref, 1 - cur)

    def drain(r, carry):
        for k in range(TOP_K):
            _row_copy(y_ref.at[pl.ds(0, 1)], ybuf.at[cur, k, pl.ds(0, 1)], sems.at[cur]).wait()
        return carry

    lax.fori_loop(0, tm, drain, 0)
    gates = gate_ref[...]
    f = gates[:, 0:1] * ybuf[cur, 0]
    for k in range(1, TOP_K):
        f = f + gates[:, k:k + 1] * ybuf[cur, k]
    xo_ref[...] = x_ref[...] + mod_ref[5:6, :] * f


def _combine(y, slot_flat, gates, xs, mod_l, geo):
    nt, d = xs.shape
    tm, per, nb = geo["tm"], geo["per"], geo["b"]
    last = nt // tm - 1
    return pl.pallas_call(
        functools.partial(_combine_kernel, tm=tm),
        grid=(nt // tm,),
        in_specs=[pl.BlockSpec((tm * TOP_K,), lambda i: (i,), memory_space=pltpu.SMEM),
                  pl.BlockSpec((tm * TOP_K,), lambda i: (jnp.minimum(i + 1, last),), memory_space=pltpu.SMEM),
                  pl.BlockSpec((tm, LANES), lambda i: (i, 0)),
                  pl.BlockSpec((tm, d), lambda i: (i, 0)),
                  pl.BlockSpec((None, 6, d), lambda i: (jnp.minimum(i // per, nb), 0, 0)),
                  pl.BlockSpec(memory_space=pl.ANY)],
        out_specs=pl.BlockSpec((tm, d), lambda i: (i, 0)),
        out_shape=jax.ShapeDtypeStruct((nt, d), F32),
        scratch_shapes=[pltpu.VMEM((2, TOP_K, tm, d), F32), pltpu.SemaphoreType.DMA((2,))],
        input_output_aliases={3: 0},
        compiler_params=_params(1),
        name="moe_combine",
    )(slot_flat, slot_flat, gates, xs, mod_l, y)


def _moe(h2, top_idx, gates, rank, counts, xs, mod_l, layer, wgu, bgu, wd, bd, geo):
    nt = h2.shape[0]
    n_assign = nt * TOP_K
    nblk = n_assign // MOE_BLOCK + N_EXPERTS
    n_slots = nblk * MOE_BLOCK
    counts = counts.astype(jnp.int32)
    padded = (counts + MOE_BLOCK - 1) // MOE_BLOCK * MOE_BLOCK
    pad_end = jnp.cumsum(padded)
    pad_start = pad_end - padded
    experts = jnp.arange(N_EXPERTS, dtype=jnp.int32)
    idx4 = top_idx[:, :TOP_K]
    base = jnp.sum(jnp.where(idx4[:, :, None] == experts[None, None, :], pad_start[None, None, :], 0), axis=-1)
    slot_flat = (base + rank[:, :TOP_K]).reshape(n_assign).astype(jnp.int32)
    n_used = (pad_end[-1] // MOE_BLOCK).astype(jnp.int32)
    blk_id = jnp.minimum(jnp.arange(nblk, dtype=jnp.int32), n_used - 1)
    block_expert = jnp.minimum(
        jnp.sum((pad_end[None, :] <= (blk_id * MOE_BLOCK)[:, None]).astype(jnp.int32), axis=1), N_EXPERTS - 1)
    xs_slots = _dispatch(h2, slot_flat, (pad_start + counts).astype(jnp.int32),
                         (padded - counts).astype(jnp.int32), n_slots, geo)
    y = _experts(xs_slots, block_expert.astype(jnp.int32), n_used.reshape(1), layer, wgu, bgu, wd, bd)
    return _combine(y, slot_flat, gates, xs, mod_l, geo)


def _final_kernel(x_ref, g_ref, o_ref):
    o_ref[...] = _rms(x_ref[...]) * g_ref[...]


def _final_norm(xs, g, nlat, tm):
    d = xs.shape[1]
    return pl.pallas_call(
        _final_kernel,
        grid=(nlat // tm,),
        in_specs=[pl.BlockSpec((tm, d), lambda i: (i, 0)), pl.BlockSpec((1, d), lambda i: (0, 0))],
        out_specs=pl.BlockSpec((tm, d), lambda i: (i, 0)),
        out_shape=jax.ShapeDtypeStruct((nlat, d), F32),
        compiler_params=_params(1),
        name="final_norm",
    )(xs, g)


def _with_identity_rows(cos, sin, tm):
    ones = jnp.ones((tm, cos.shape[1]), F32)
    return jnp.concatenate([cos, ones], axis=0), jnp.concatenate([sin, 0.0 * ones], axis=0)


def _axial_tables(seq, tm, pad_to_lanes):
    pos = jnp.arange(seq)
    rows = (pos // GRID_W).astype(F32)
    cols = (pos % GRID_W).astype(F32)
    half = DA_HEAD_DIM // 2
    inv = ROPE_BASE ** (-jnp.arange(0, half, 2, dtype=F32) / half)
    ar = rows[:, None] * inv[None, :]
    ac = cols[:, None] * inv[None, :]
    cos64 = jnp.concatenate([jnp.cos(ar), jnp.cos(ar), jnp.cos(ac), jnp.cos(ac)], axis=1)
    sin64 = jnp.concatenate([-jnp.sin(ar), jnp.sin(ar), -jnp.sin(ac), jnp.sin(ac)], axis=1)
    if pad_to_lanes:
        cos = jnp.concatenate([cos64, jnp.ones_like(cos64)], axis=1)
        sin = jnp.concatenate([sin64, jnp.zeros_like(sin64)], axis=1)
    else:
        cos = jnp.concatenate([cos64, cos64], axis=1)
        sin = jnp.concatenate([sin64, sin64], axis=1)
    return _with_identity_rows(cos, sin, tm)


def _ret_tables(seq, tm):
    inv = ROPE_BASE ** (-jnp.arange(0, RET_KDIM, 2, dtype=F32) / RET_KDIM)
    ang = jnp.arange(seq).astype(F32)[:, None] * inv[None, :]
    return _with_identity_rows(jnp.cos(ang), jnp.sin(ang), tm)


def _mla_weights(w_in, w_q_up, w_kv_up):
    d = w_in.shape[0]
    kvw = MLA_KV_RANK + MLA_ROPE
    w_in_r = jnp.concatenate([w_in[:, :MLA_KV_RANK], w_in[:, kvw:], w_in[:, MLA_KV_RANK:kvw],
                              jnp.zeros((d, LANES - MLA_ROPE), w_in.dtype)], axis=1)
    wq = w_q_up.reshape(MLA_Q_RANK, MLA_HEADS, MLA_NOPE + MLA_ROPE)
    wq = jnp.concatenate([wq, jnp.zeros((MLA_Q_RANK, MLA_HEADS, LANES - MLA_ROPE), wq.dtype)], axis=2)
    wq = wq.reshape(MLA_Q_RANK, MLA_HEADS * 2 * LANES)
    wkv = w_kv_up.reshape(MLA_KV_RANK, MLA_HEADS, MLA_NOPE + MLA_V)
    wkv = jnp.concatenate([wkv[:, :, :MLA_NOPE].reshape(MLA_KV_RANK, -1),
                           wkv[:, :, MLA_NOPE:].reshape(MLA_KV_RANK, -1)], axis=1)
    return w_in_r.astype(BF16), wq.astype(BF16), wkv.astype(BF16)


def kernel(x, c, ctx, c_ctx, ada_w, ada_b, norm_g, final_g, da_w_in, da_w_out, da_lambda, da_subln_g,
           ret_w_in, ret_decay_logit, ret_w_out, mla_w_in, mla_q_norm_g, mla_w_q_up, mla_kv_norm_g,
           mla_w_kv_up, mla_w_out, moe_w_router, moe_b_router, moe_w_gate_up, moe_b_gate_up,
           moe_w_down, moe_b_down):
    nb, seq, d = x.shape
    nctx_per = ctx.shape[1]
    depth = ada_w.shape[0]
    nlat = nb * seq
    tm = _pick_tile(512, seq, nb * nctx_per)
    geo = dict(b=nb, seq=seq, ctx=nctx_per, nlat=nlat, tm=tm, per=seq // tm, nlt=nlat // tm)
    assert nb < MOD_ROWS and nlat % nctx_per == 0 and nctx_per % RET_CHUNK == 0

    xs = jnp.concatenate([x.reshape(nlat, d), ctx.reshape(nb * nctx_per, d)], axis=0).astype(F32)
    cond = jnp.zeros((MOD_ROWS, d), F32).at[:nb].set(c).at[nb].set(c_ctx)
    mod = _adaln(cond, ada_w, ada_b).reshape(depth, MOD_ROWS, 6, d)

    cos_a, sin_a = _axial_tables(seq, tm, pad_to_lanes=False)
    cos_m, sin_m = _axial_tables(seq, tm, pad_to_lanes=True)
    cos_r, sin_r = _ret_tables(seq, tm)

    for i in range(depth):
        kind = i % N_MIXERS
        j = i // N_MIXERS
        mod_l = mod[i]
        g1 = norm_g[i, 0].reshape(1, d)
        g2 = norm_g[i, 1].reshape(1, d)
        if kind == 0:
            lam_init = 0.8 - 0.6 * math.exp(-0.3 * i)
            p = _proj(xs, g1, mod_l, da_w_in[j].astype(BF16), cos_a, sin_a,
                      ["rope16", "plain", "rope16"], geo)
            o_lat, o_ctx = _da_attention(p, da_lambda[j].astype(F32),
                                         da_subln_g[j].reshape(1, DA_V_DIM).astype(F32), lam_init, geo)
            ctx_off = 0
            w_out = da_w_out[j]
        elif kind == 1:
            p = _proj(xs, g1, mod_l, ret_w_in[j].astype(BF16), cos_r, sin_r,
                      ["ret_k", "plain", "plain", "ret_q", "plain", "plain"], geo,
                      ret_scale=RET_KDIM ** -0.5)
            o_lat = o_ctx = _retention(p, ret_decay_logit[j], geo)
            ctx_off = geo["nlt"]
            w_out = ret_w_out[j]
        else:
            w_in_r, wq, wkv = _mla_weights(mla_w_in[j], mla_w_q_up[j], mla_w_kv_up[j])
            q, kn, v, kr = _mla_proj(xs, g1, mod_l, w_in_r, mla_q_norm_g[j].reshape(1, -1).astype(F32),
                                     mla_kv_norm_g[j].reshape(1, -1).astype(F32), wq, wkv, cos_m, sin_m, geo)
            o_lat, o_ctx = _mla_attention(q, kn, v, kr, geo)
            ctx_off = 0
            w_out = mla_w_out[j]
        xs, h2, top_idx, gates, rank, counts = _post(o_lat, o_ctx, ctx_off, w_out.astype(BF16), xs, mod_l, g2,
                                                     moe_w_router[i], moe_b_router[i], geo)
        xs = _moe(h2, top_idx, gates, rank, counts[0, :N_EXPERTS], xs, mod_l, i,
                  moe_w_gate_up, moe_b_gate_up, moe_w_down, moe_b_down, geo)
    return _final_norm(xs, final_g.reshape(1, d).astype(F32), nlat, tm).reshape(nb, seq, d)
```

```python
import functools
import math

import jax
import jax.numpy as jnp
from jax import lax
from jax.experimental import pallas as pl
from jax.experimental.pallas import tpu as pltpu

F32 = jnp.float32
BF16 = jnp.bfloat16

D_MODEL = 1024
GRID_W = 64
RMS_EPS = 1e-6
ROPE_BASE = 10000.0
N_MIXERS = 3

DA_HEADS = 8
DA_HEAD_DIM = 64
DA_V_DIM = 128

RET_HEADS = 4
RET_KDIM = 256
RET_VDIM = 512
RET_CHUNK = 128

MLA_HEADS = 8
MLA_NOPE = 128
MLA_ROPE = 64
MLA_V = 128
MLA_Q_RANK = 256
MLA_KV_RANK = 128

N_EXPERTS = 32
TOP_K = 4
D_EXPERT = 1024
SWIGLU_LIMIT = 7.0
SWIGLU_ALPHA = 1.702

LANES = 128
MOD_ROWS = 16
MOE_BLOCK = 512
COL_CHUNK = 1024
ATTN_TQ = 1024
ATTN_TK = 1024
VMEM_LIMIT = 56 * 1024 * 1024
NEG_BIG = -1e30


def _params(n_axes):
    return pltpu.CompilerParams(dimension_semantics=("arbitrary",) * n_axes,
                                vmem_limit_bytes=VMEM_LIMIT)


def _pick_tile(cap, *dims):
    t = cap
    while any(d % t for d in dims):
        t //= 2
    return t


def _rms(x):
    return x * lax.rsqrt(jnp.mean(x * x, axis=-1, keepdims=True) + RMS_EPS)


def _norm_mod(x, g, shift, scale):
    return (_rms(x) * g) * (1.0 + scale) + shift


def _dot(a, b):
    return jnp.dot(a, b, preferred_element_type=F32)


def _dot_nt(a, b):
    return lax.dot_general(a, b, (((1,), (1,)), ((), ())), preferred_element_type=F32)


def _dot_tn(a, b):
    return lax.dot_general(a, b, (((0,), (0,)), ((), ())), preferred_element_type=F32)


def _adaln_kernel(c_ref, w_ref, b_ref, o_ref):
    c = c_ref[...]
    a = c * (1.0 / (1.0 + jnp.exp(-c)))
    o_ref[...] = jnp.dot(a, w_ref[...], precision=lax.Precision.HIGHEST,
                         preferred_element_type=F32) + b_ref[...]


def _adaln(cond, ada_w, ada_b):
    depth, d, n = ada_w.shape
    tn = COL_CHUNK
    return pl.pallas_call(
        _adaln_kernel,
        grid=(depth, n // tn),
        in_specs=[pl.BlockSpec((MOD_ROWS, d), lambda l, j: (0, 0)),
                  pl.BlockSpec((None, d, tn), lambda l, j: (l, 0, j)),
                  pl.BlockSpec((None, 1, tn), lambda l, j: (l, 0, j))],
        out_specs=pl.BlockSpec((None, MOD_ROWS, tn), lambda l, j: (l, 0, j)),
        out_shape=jax.ShapeDtypeStruct((depth, MOD_ROWS, n), F32),
        compiler_params=_params(2),
        name="adaln",
    )(cond, ada_w, ada_b.reshape(depth, 1, n))


def _rope_pairs16(x, cos, sin):
    lane = lax.broadcasted_iota(jnp.int32, x.shape, 1)
    first = (lane % 32) < 16
    partner = jnp.where(first, pltpu.roll(x, LANES - 16, 1), pltpu.roll(x, 16, 1))
    return x * cos + partner * sin


def _proj_kernel(x_ref, g_ref, mod_ref, w_ref, cos_ref, sin_ref, o_ref, h_scr, *, kinds, ret_scale):
    j = pl.program_id(1)

    @pl.when(j == 0)
    def _():
        h = _norm_mod(x_ref[...], g_ref[...], mod_ref[0:1, :], mod_ref[1:2, :])
        h_scr[...] = h.astype(BF16)

    def chunks_of(kind):
        cs = [c for c, k in enumerate(kinds) if k == kind]
        pred = None
        for c in cs:
            pred = (j == c) if pred is None else jnp.logical_or(pred, j == c)
        return pred

    def acc():
        return _dot(h_scr[...], w_ref[...])

    if "plain" in kinds:
        @pl.when(chunks_of("plain"))
        def _():
            o_ref[...] = acc().astype(BF16)

    if "rope16" in kinds:
        @pl.when(chunks_of("rope16"))
        def _():
            a = acc()
            cos = cos_ref[...]
            sin = sin_ref[...]
            for g in range(COL_CHUNK // LANES):
                sl = slice(g * LANES, (g + 1) * LANES)
                o_ref[:, sl] = _rope_pairs16(a[:, sl], cos, sin).astype(BF16)

    for kind, scale in (("ret_k", ret_scale), ("ret_q", 1.0)):
        if kind in kinds:
            @pl.when(chunks_of(kind))
            def _(scale=scale):
                a = acc()
                cos = cos_ref[...]
                sin = sin_ref[...]
                for hh in range(COL_CHUNK // RET_KDIM):
                    lo = slice(hh * RET_KDIM, hh * RET_KDIM + LANES)
                    hi = slice(hh * RET_KDIM + LANES, (hh + 1) * RET_KDIM)
                    x1 = a[:, lo]
                    x2 = a[:, hi]
                    o_ref[:, lo] = ((x1 * cos - x2 * sin) * scale).astype(BF16)
                    o_ref[:, hi] = ((x1 * sin + x2 * cos) * scale).astype(BF16)


def _proj(xs, g, mod_l, w, cos_t, sin_t, kinds, geo, ret_scale=1.0):
    nt, d = xs.shape
    n = w.shape[1]
    tm, per, nlt, nb = geo["tm"], geo["per"], geo["nlt"], geo["b"]
    return pl.pallas_call(
        functools.partial(_proj_kernel, kinds=tuple(kinds), ret_scale=ret_scale),
        grid=(nt // tm, n // COL_CHUNK),
        in_specs=[pl.BlockSpec((tm, d), lambda i, j: (i, 0)),
                  pl.BlockSpec((1, d), lambda i, j: (0, 0)),
                  pl.BlockSpec((None, 6, d), lambda i, j: (jnp.minimum(i // per, nb), 0, 0)),
                  pl.BlockSpec((d, COL_CHUNK), lambda i, j: (0, j)),
                  pl.BlockSpec((tm, LANES), lambda i, j: (jnp.where(i < nlt, i % per, per), 0)),
                  pl.BlockSpec((tm, LANES), lambda i, j: (jnp.where(i < nlt, i % per, per), 0))],
        out_specs=pl.BlockSpec((tm, COL_CHUNK), lambda i, j: (i, j)),
        out_shape=jax.ShapeDtypeStruct((nt, n), BF16),
        scratch_shapes=[pltpu.VMEM((tm, d), BF16)],
        compiler_params=_params(2),
        name="proj",
    )(xs, g, mod_l, w, cos_t, sin_t)


def _mla_proj_kernel(x_ref, g_ref, mod_ref, win_ref, qg_ref, kvg_ref, wq_ref, wkv_ref, cos_ref, sin_ref,
                     q_ref, kn_ref, v_ref, kr_ref):
    h = _norm_mod(x_ref[...], g_ref[...], mod_ref[0:1, :], mod_ref[1:2, :]).astype(BF16)
    p = _dot(h, win_ref[...])
    cos = cos_ref[...]
    sin = sin_ref[...]
    ckv = (_rms(p[:, :MLA_KV_RANK]) * kvg_ref[...]).astype(BF16)
    kv = _dot(ckv, wkv_ref[...])
    nk = MLA_HEADS * MLA_NOPE
    kn_ref[...] = kv[:, :nk].astype(BF16)
    v_ref[...] = kv[:, nk:].astype(BF16)
    kr_ref[...] = _rope_pairs16(p[:, MLA_KV_RANK + MLA_Q_RANK:], cos, sin).astype(BF16)
    cq = (_rms(p[:, MLA_KV_RANK:MLA_KV_RANK + MLA_Q_RANK]) * qg_ref[...]).astype(BF16)
    q = _dot(cq, wq_ref[...])
    for hh in range(MLA_HEADS):
        lo = slice(hh * 2 * LANES, hh * 2 * LANES + LANES)
        hi = slice(hh * 2 * LANES + LANES, (hh + 1) * 2 * LANES)
        q_ref[:, lo] = q[:, lo].astype(BF16)
        q_ref[:, hi] = _rope_pairs16(q[:, hi], cos, sin).astype(BF16)


def _mla_proj(xs, g, mod_l, w_in, qg, kvg, wq, wkv, cos_t, sin_t, geo):
    nt, d = xs.shape
    tm, per, nlt, nb = geo["tm"], geo["per"], geo["nlt"], geo["b"]
    full = lambda a: pl.BlockSpec(a.shape, lambda i: (0,) * a.ndim)
    tab = pl.BlockSpec((tm, LANES), lambda i: (jnp.where(i < nlt, i % per, per), 0))
    nq = MLA_HEADS * 2 * LANES
    nk = MLA_HEADS * MLA_NOPE
    return pl.pallas_call(
        _mla_proj_kernel,
        grid=(nt // tm,),
        in_specs=[pl.BlockSpec((tm, d), lambda i: (i, 0)), full(g),
                  pl.BlockSpec((None, 6, d), lambda i: (jnp.minimum(i // per, nb), 0, 0)),
                  full(w_in), full(qg), full(kvg), full(wq), full(wkv), tab, tab],
        out_specs=[pl.BlockSpec((tm, nq), lambda i: (i, 0)),
                   pl.BlockSpec((tm, nk), lambda i: (i, 0)),
                   pl.BlockSpec((tm, nk), lambda i: (i, 0)),
                   pl.BlockSpec((tm, LANES), lambda i: (i, 0))],
        out_shape=[jax.ShapeDtypeStruct((nt, nq), BF16), jax.ShapeDtypeStruct((nt, nk), BF16),
                   jax.ShapeDtypeStruct((nt, nk), BF16), jax.ShapeDtypeStruct((nt, LANES), BF16)],
        compiler_params=_params(1),
        name="mla_proj",
    )(xs, g, mod_l, w_in, qg, kvg, wq, wkv, cos_t, sin_t)


LOG2E = 1.4426950408889634


def _flash_init(m_ref, a_ref):
    m_ref[...] = jnp.full(m_ref.shape, -jnp.inf, F32)
    a_ref[...] = jnp.zeros(a_ref.shape, F32)


def _ones_column(rows):
    lane = lax.broadcasted_iota(jnp.int32, (rows, LANES), 1)
    return jnp.where(lane == 0, 1.0, 0.0).astype(BF16)


def _chunk_rows(j, tk):
    if isinstance(j, int):
        return pl.ds(j * tk, tk)
    return pl.ds(pl.multiple_of(j * tk, tk), tk)


def _softmax_pv(parts, stats):
    for mp, (m_ref, a_ref) in enumerate(stats):
        tiles = [scores[mp] for scores, _ in parts]
        m_prev = m_ref[...]
        m_new = m_prev
        for s2 in tiles:
            m_new = jnp.maximum(m_new, jnp.max(s2, axis=1, keepdims=True))
        acc = jnp.exp2(m_prev - m_new) * a_ref[...]
        for s2, (_, v_aug) in zip(tiles, parts):
            acc = acc + _dot(jnp.exp2(s2 - m_new).astype(BF16), v_aug)
        a_ref[...] = acc
        m_ref[...] = m_new


def _flash_lat(qk_fns, stats, k_ctx, v_ctx, k_at, v_at, s_a, s_b, s_c, n_chunks):
    def qk_into(dst, k):
        for mp, f in enumerate(qk_fns):
            dst[mp] = f(k)

    qk_into(s_c, k_ctx)
    qk_into(s_a, k_at(0))
    qk_into(s_b, k_at(1))
    _softmax_pv([(s_c, v_ctx), (s_a, v_at(0))], stats)

    def body(t, carry):
        j = 2 * t + 1
        qk_into(s_a, k_at(j + 1))
        _softmax_pv([(s_b, v_at(j))], stats)
        qk_into(s_b, k_at(j + 2))
        _softmax_pv([(s_a, v_at(j + 1))], stats)
        return carry

    lax.fori_loop(0, (n_chunks - 2) // 2, body, 0)
    _softmax_pv([(s_b, v_at(n_chunks - 1))], stats)


def _flash_result(a_ref, vdim):
    a = a_ref[...]
    return a[:, :vdim] / a[:, vdim:vdim + 1]


def _da_attn_kernel(*refs, n_chunks, tk, lam_init, scale):
    if n_chunks:
        (lam_ref, sg_ref, q_ref, kc_ref, vc_ref, kl_ref, vl_ref, o_ref, m0, a0, m1, a1, s_a, s_b, s_c) = refs
    else:
        (lam_ref, sg_ref, q_ref, kc_ref, vc_ref, o_ref, m0, a0, m1, a1) = refs
    q = q_ref[...]
    lane = lax.broadcasted_iota(jnp.int32, q.shape, 1)
    zero = jnp.zeros_like(q)
    q_lo = jnp.where(lane < DA_HEAD_DIM, q, zero)
    q_hi = jnp.where(lane >= DA_HEAD_DIM, q, zero)
    _flash_init(m0, a0)
    _flash_init(m1, a1)
    c2 = scale * LOG2E
    stats = [(m0, a0), (m1, a1)]
    qk_fns = [lambda k: _dot_nt(q_lo, k) * c2, lambda k: _dot_nt(q_hi, k) * c2]
    v_ctx = jnp.concatenate([vc_ref[...], _ones_column(vc_ref.shape[0])], axis=1)
    if n_chunks:
        ones = _ones_column(tk)

        def k_at(j):
            return kl_ref[_chunk_rows(j, tk), :]

        def v_at(j):
            return jnp.concatenate([vl_ref[_chunk_rows(j, tk), :], ones], axis=1)

        _flash_lat(qk_fns, stats, kc_ref[...], v_ctx, k_at, v_at, s_a, s_b, s_c, n_chunks)
    else:
        _softmax_pv([([f(kc_ref[...]) for f in qk_fns], v_ctx)], stats)

    lf = lam_ref[...]
    lam = (jnp.exp(jnp.sum(lf[0:1] * lf[1:2], axis=1, keepdims=True))
           - jnp.exp(jnp.sum(lf[2:3] * lf[3:4], axis=1, keepdims=True)) + lam_init)
    o = _flash_result(a0, DA_V_DIM) - lam * _flash_result(a1, DA_V_DIM)
    o = _rms(o) * sg_ref[...] * (1.0 - lam_init)
    o_ref[...] = o.astype(BF16)


def _attn_scratch(tq, vdim, n_maps, tk=0, ctx=0):
    s = []
    for _ in range(n_maps):
        s += [pltpu.VMEM((tq, 1), F32), pltpu.VMEM((tq, vdim + LANES), F32)]
    if tk:
        s += [pltpu.VMEM((n_maps, tq, tk), F32), pltpu.VMEM((n_maps, tq, tk), F32),
              pltpu.VMEM((n_maps, tq, ctx), F32)]
    return s


def _da_attention(p, lam_vecs, subg, lam_init, geo):
    nt = p.shape[0]
    nb, seq, ctx, nlat = geo["b"], geo["seq"], geo["ctx"], geo["nlat"]
    hh = DA_HEADS
    tq = _pick_tile(ATTN_TQ, seq)
    tk = _pick_tile(min(ATTN_TK, seq // 2), seq)
    assert (seq // tk) % 2 == 0
    nqt = seq // tq
    cb = nlat // ctx
    scale = DA_HEAD_DIM ** -0.5
    small = [pl.BlockSpec(lam_vecs.shape, lambda *a: (0, 0)), pl.BlockSpec(subg.shape, lambda *a: (0, 0))]
    o_lat = pl.pallas_call(
        functools.partial(_da_attn_kernel, n_chunks=seq // tk, tk=tk, lam_init=lam_init, scale=scale),
        grid=(nb, hh, nqt),
        in_specs=small + [
            pl.BlockSpec((tq, LANES), lambda b, h, i: (b * nqt + i, 2 * hh + h)),
            pl.BlockSpec((ctx, LANES), lambda b, h, i: (cb + b, h)),
            pl.BlockSpec((ctx, LANES), lambda b, h, i: (cb + b, hh + h)),
            pl.BlockSpec((seq, LANES), lambda b, h, i: (b, h)),
            pl.BlockSpec((seq, LANES), lambda b, h, i: (b, hh + h))],
        out_specs=pl.BlockSpec((tq, LANES), lambda b, h, i: (b * nqt + i, h)),
        out_shape=jax.ShapeDtypeStruct((nlat, hh * DA_V_DIM), BF16),
        scratch_shapes=_attn_scratch(tq, DA_V_DIM, 2, tk, ctx),
        compiler_params=_params(3),
        name="da_attn_lat",
    )(lam_vecs, subg, p, p, p, p, p)
    o_ctx = pl.pallas_call(
        functools.partial(_da_attn_kernel, n_chunks=0, tk=tk, lam_init=lam_init, scale=scale),
        grid=(nb, hh),
        in_specs=small + [
            pl.BlockSpec((ctx, LANES), lambda b, h: (cb + b, 2 * hh + h)),
            pl.BlockSpec((ctx, LANES), lambda b, h: (cb + b, h)),
            pl.BlockSpec((ctx, LANES), lambda b, h: (cb + b, hh + h))],
        out_specs=pl.BlockSpec((ctx, LANES), lambda b, h: (b, h)),
        out_shape=jax.ShapeDtypeStruct((nt - nlat, hh * DA_V_DIM), BF16),
        scratch_shapes=_attn_scratch(ctx, DA_V_DIM, 2),
        compiler_params=_params(2),
        name="da_attn_ctx",
    )(lam_vecs, subg, p, p, p)
    return o_lat, o_ctx


def _mla_attn_kernel(*refs, n_chunks, tk, scale):
    if n_chunks:
        (q_ref, knc_ref, krc_ref, vc_ref, knl_ref, krl_ref, vl_ref, o_ref, m0, a0, s_a, s_b, s_c) = refs
    else:
        (q_ref, knc_ref, krc_ref, vc_ref, o_ref, m0, a0) = refs
    q = q_ref[...]
    _flash_init(m0, a0)
    c2 = scale * LOG2E
    stats = [(m0, a0)]
    qk_fns = [lambda k: _dot_nt(q, k) * c2]
    k_ctx = jnp.concatenate([knc_ref[...], krc_ref[...]], axis=1)
    v_ctx = jnp.concatenate([vc_ref[...], _ones_column(vc_ref.shape[0])], axis=1)
    if n_chunks:
        ones = _ones_column(tk)

        def k_at(j):
            rows = _chunk_rows(j, tk)
            return jnp.concatenate([knl_ref[rows, :], krl_ref[rows, :]], axis=1)

        def v_at(j):
            return jnp.concatenate([vl_ref[_chunk_rows(j, tk), :], ones], axis=1)

        _flash_lat(qk_fns, stats, k_ctx, v_ctx, k_at, v_at, s_a, s_b, s_c, n_chunks)
    else:
        _softmax_pv([([qk_fns[0](k_ctx)], v_ctx)], stats)
    o_ref[...] = _flash_result(a0, MLA_V).astype(BF16)


def _mla_attention(q, kn, v, kr, geo):
    nt = q.shape[0]
    nb, seq, ctx, nlat = geo["b"], geo["seq"], geo["ctx"], geo["nlat"]
    hh = MLA_HEADS
    tq = _pick_tile(ATTN_TQ, seq)
    tk = _pick_tile(min(ATTN_TK, seq // 2), seq)
    assert (seq // tk) % 2 == 0
    nqt = seq // tq
    cb = nlat // ctx
    scale = (MLA_NOPE + MLA_ROPE) ** -0.5
    o_lat = pl.pallas_call(
        functools.partial(_mla_attn_kernel, n_chunks=seq // tk, tk=tk, scale=scale),
        grid=(nb, hh, nqt),
        in_specs=[
            pl.BlockSpec((tq, 2 * LANES), lambda b, h, i: (b * nqt + i, h)),
            pl.BlockSpec((ctx, LANES), lambda b, h, i: (cb + b, h)),
            pl.BlockSpec((ctx, LANES), lambda b, h, i: (cb + b, 0)),
            pl.BlockSpec((ctx, LANES), lambda b, h, i: (cb + b, h)),
            pl.BlockSpec((seq, LANES), lambda b, h, i: (b, h)),
            pl.BlockSpec((seq, LANES), lambda b, h, i: (b, 0)),
            pl.BlockSpec((seq, LANES), lambda b, h, i: (b, h))],
        out_specs=pl.BlockSpec((tq, LANES), lambda b, h, i: (b * nqt + i, h)),
        out_shape=jax.ShapeDtypeStruct((nlat, hh * MLA_V), BF16),
        scratch_shapes=_attn_scratch(tq, MLA_V, 1, tk, ctx),
        compiler_params=_params(3),
        name="mla_attn_lat",
    )(q, kn, kr, v, kn, kr, v)
    o_ctx = pl.pallas_call(
        functools.partial(_mla_attn_kernel, n_chunks=0, tk=tk, scale=scale),
        grid=(nb, hh),
        in_specs=[
            pl.BlockSpec((ctx, 2 * LANES), lambda b, h: (cb + b, h)),
            pl.BlockSpec((ctx, LANES), lambda b, h: (cb + b, h)),
            pl.BlockSpec((ctx, LANES), lambda b, h: (cb + b, 0)),
            pl.BlockSpec((ctx, LANES), lambda b, h: (cb + b, h))],
        out_specs=pl.BlockSpec((ctx, LANES), lambda b, h: (b, h)),
        out_shape=jax.ShapeDtypeStruct((nt - nlat, hh * MLA_V), BF16),
        scratch_shapes=_attn_scratch(ctx, MLA_V, 1),
        compiler_params=_params(2),
        name="mla_attn_ctx",
    )(q, kn, kr, v)
    return o_lat, o_ctx


def _ret_kernel(*refs, reverse):
    if reverse:
        dl_ref, q_ref, k_ref, v0_ref, v1_ref, of_ref, g0_ref, g1_ref, o_ref, st = refs
    else:
        dl_ref, q_ref, k_ref, v0_ref, v1_ref, o_ref, st = refs
    s = pl.program_id(1)

    @pl.when(s == 0)
    def _():
        st[...] = jnp.zeros(st.shape, F32)

    dl = dl_ref[...]
    lsig = jnp.minimum(dl, 0.0) - jnp.log(1.0 + jnp.exp(-jnp.abs(dl)))
    row = lax.broadcasted_iota(jnp.int32, dl.shape, 0)
    col = lax.broadcasted_iota(jnp.int32, dl.shape, 1)
    c = RET_CHUNK
    i2 = lax.broadcasted_iota(jnp.int32, (c, c), 0)
    j2 = lax.broadcasted_iota(jnp.int32, (c, c), 1)
    pos = lax.broadcasted_iota(jnp.int32, (c, 1), 0).astype(F32)
    dist = ((j2 - i2) if reverse else (i2 - j2)).astype(F32)
    keep = dist >= 0.0
    dist = jnp.where(keep, dist, 0.0)
    per_blk = COL_CHUNK // RET_VDIM

    for hd in range(RET_HEADS):
        pick = jnp.logical_and(row == (1 if reverse else 0), col == hd)
        lg = jnp.sum(jnp.sum(jnp.where(pick, lsig, 0.0), axis=1, keepdims=True), axis=0, keepdims=True)
        if reverse:
            q_dec = jnp.exp((c - pos) * lg)
            k_dec = jnp.exp(pos * lg)
        else:
            q_dec = jnp.exp((pos + 1.0) * lg)
            k_dec = jnp.exp((c - 1.0 - pos) * lg)
        intra = jnp.where(keep, jnp.exp(dist * lg), 0.0)
        chunk_dec = jnp.exp(c * lg)

        ksl = slice(hd * RET_KDIM, (hd + 1) * RET_KDIM)
        vsl = slice((hd % per_blk) * RET_VDIM, (hd % per_blk + 1) * RET_VDIM)
        osl = slice(hd * RET_VDIM, (hd + 1) * RET_VDIM)
        qb = q_ref[:, ksl]
        kb = k_ref[:, ksl]
        vb = (v0_ref if hd < per_blk else v1_ref)[:, vsl]
        state = st[hd]
        sc = _dot_nt(qb, kb) * intra
        o = _dot(sc.astype(BF16), vb) + _dot((qb.astype(F32) * q_dec).astype(BF16), state.astype(BF16))
        st[hd] = chunk_dec * state + _dot_tn((kb.astype(F32) * k_dec).astype(BF16), vb)
        if reverse:
            tot = _rms(of_ref[:, osl] + o)
            g = (g0_ref if hd < per_blk else g1_ref)[:, vsl].astype(F32)
            o_ref[:, osl] = (g * (1.0 / (1.0 + jnp.exp(-g))) * tot).astype(BF16)
        else:
            o_ref[:, osl] = o


def _retention(p, decay_logit, geo):
    nt = p.shape[0]
    nb, seq, ctx, nlat = geo["b"], geo["seq"], geo["ctx"], geo["nlat"]
    c = RET_CHUNK
    ncc, ncl = ctx // c, seq // c
    hh = RET_HEADS
    kq, kv = RET_KDIM, RET_VDIM
    wb = COL_CHUNK
    assert hh * kq == wb and hh * kv == 2 * wb
    dl = jnp.zeros((8, LANES), F32).at[:2, :hh].set(decay_logit.astype(F32))

    def rows_fwd(b, s):
        return jnp.where(s < ncc, nlat // c + b * ncc + s, b * ncl + (s - ncc))

    def rows_bwd(b, s):
        return jnp.where(s < ncc, nlat // c + b * ncc + (ncc - 1 - s), b * ncl + (ncl - 1 - (s - ncc)))

    def blk(rows, col):
        return pl.BlockSpec((c, wb), lambda b, s: (rows(b, s), col))

    def specs(rows):
        return [pl.BlockSpec((8, LANES), lambda b, s: (0, 0)), blk(rows, 3), blk(rows, 0), blk(rows, 1), blk(rows, 2)]

    grid = (nb, ncc + ncl)
    scratch = [pltpu.VMEM((hh, kq, kv), F32)]
    o_f = pl.pallas_call(
        functools.partial(_ret_kernel, reverse=False),
        grid=grid,
        in_specs=specs(rows_fwd),
        out_specs=pl.BlockSpec((c, hh * kv), lambda b, s: (rows_fwd(b, s), 0)),
        out_shape=jax.ShapeDtypeStruct((nt, hh * kv), F32),
        scratch_shapes=scratch,
        compiler_params=_params(2),
        name="ret_fwd",
    )(dl, p, p, p, p)
    return pl.pallas_call(
        functools.partial(_ret_kernel, reverse=True),
        grid=grid,
        in_specs=specs(rows_bwd) + [
            pl.BlockSpec((c, hh * kv), lambda b, s: (rows_bwd(b, s), 0)), blk(rows_bwd, 4), blk(rows_bwd, 5)],
        out_specs=pl.BlockSpec((c, hh * kv), lambda b, s: (rows_bwd(b, s), 0)),
        out_shape=jax.ShapeDtypeStruct((nt, hh * kv), BF16),
        scratch_shapes=scratch,
        compiler_params=_params(2),
        name="ret_bwd",
    )(dl, p, p, p, p, o_f, p, p)


def _post_kernel(ol_ref, oc_ref, w_ref, x_ref, mod_ref, g2_ref, wr_ref, br_ref, tri_ref,
                 xo_ref, h2_ref, idx_ref, gate_ref, rank_ref, cnt_ref, cnt_scr, *, nlt):
    t = pl.program_id(0)

    @pl.when(t == 0)
    def _():
        cnt_scr[...] = jnp.zeros(cnt_scr.shape, F32)

    o = jnp.where(t < nlt, ol_ref[...], oc_ref[...])
    y = _dot(o, w_ref[...])
    x = x_ref[...] + mod_ref[2:3, :] * y
    xo_ref[...] = x
    h2 = _norm_mod(x, g2_ref[...], mod_ref[3:4, :], mod_ref[4:5, :])
    _store_rows_as_tiles(h2_ref, h2)
    logits = jnp.dot(h2, wr_ref[...], precision=lax.Precision.HIGHEST,
                     preferred_element_type=F32) + br_ref[...]
    lane = lax.broadcasted_iota(jnp.int32, logits.shape, 1).astype(F32)
    work = logits
    vals, idxs = [], []
    for _ in range(TOP_K):
        mx = jnp.max(work, axis=1, keepdims=True)
        ix = jnp.min(jnp.where(work == mx, lane, float(LANES)), axis=1, keepdims=True)
        vals.append(mx)
        idxs.append(ix)
        work = jnp.where(lane == ix, -jnp.inf, work)
    es = [jnp.exp(v - vals[0]) for v in vals]
    den = es[0] + es[1] + es[2] + es[3]
    onehots = [lane == ix for ix in idxs]
    oh = jnp.zeros(logits.shape, F32)
    for m in onehots:
        oh = oh + jnp.where(m, 1.0, 0.0)
    before = _dot(tri_ref[...], oh.astype(BF16)) + cnt_scr[0:1, :]
    idx_out = jnp.zeros(logits.shape, F32)
    gate_out = jnp.zeros(logits.shape, F32)
    rank_out = jnp.zeros(logits.shape, F32)
    for r in range(TOP_K):
        sel = lane == float(r)
        rk = jnp.sum(jnp.where(onehots[r], before, 0.0), axis=1, keepdims=True)
        idx_out = jnp.where(sel, idxs[r], idx_out)
        gate_out = jnp.where(sel, es[r] / den, gate_out)
        rank_out = jnp.where(sel, rk, rank_out)
    idx_ref[...] = idx_out.astype(jnp.int32)
    gate_ref[...] = gate_out
    rank_ref[...] = rank_out.astype(jnp.int32)
    cnt_scr[0:1, :] = cnt_scr[0:1, :] + jnp.sum(oh, axis=0, keepdims=True)
    cnt_ref[...] = cnt_scr[...]


def _post(o_lat, o_ctx, ctx_off, w_out, xs, mod_l, g2, w_router, b_router, geo):
    nt, d = xs.shape
    ko = o_lat.shape[1]
    tm, per, nb, nlt = geo["tm"], geo["per"], geo["b"], geo["nlt"]
    ii = lax.broadcasted_iota(jnp.int32, (tm, tm), 0)
    jj = lax.broadcasted_iota(jnp.int32, (tm, tm), 1)
    tri = (jj < ii).astype(BF16)
    wr = jnp.zeros((d, LANES), F32).at[:, :N_EXPERTS].set(w_router.astype(F32))
    br = jnp.full((1, LANES), NEG_BIG, F32).at[0, :N_EXPERTS].set(b_router.astype(F32))
    row = lambda w: pl.BlockSpec((tm, w), lambda i: (i, 0))
    full = lambda a: pl.BlockSpec(a.shape, lambda i: (0,) * a.ndim)
    return pl.pallas_call(
        functools.partial(_post_kernel, nlt=nlt),
        grid=(nt // tm,),
        in_specs=[pl.BlockSpec((tm, ko), lambda i: (jnp.minimum(i, nlt - 1), 0)),
                  pl.BlockSpec((tm, ko), lambda i: (jnp.maximum(i - nlt, 0) + ctx_off, 0)),
                  full(w_out), row(d),
                  pl.BlockSpec((None, 6, d), lambda i: (jnp.minimum(i // per, nb), 0, 0)),
                  full(g2), full(wr), full(br), full(tri)],
        out_specs=[row(d), pl.BlockSpec((tm * SUBLANES, LANES), lambda i: (i, 0)),
                   row(LANES), row(LANES), row(LANES),
                   pl.BlockSpec((8, LANES), lambda i: (0, 0))],
        out_shape=[jax.ShapeDtypeStruct((nt, d), F32), jax.ShapeDtypeStruct((nt * SUBLANES, LANES), F32),
                   jax.ShapeDtypeStruct((nt, LANES), jnp.int32), jax.ShapeDtypeStruct((nt, LANES), F32),
                   jax.ShapeDtypeStruct((nt, LANES), jnp.int32), jax.ShapeDtypeStruct((8, LANES), F32)],
        scratch_shapes=[pltpu.VMEM((8, LANES), F32)],
        input_output_aliases={3: 0},
        compiler_params=_params(1),
        name="post",
    )(o_lat, o_ctx, w_out, xs, mod_l, g2, wr, br, tri)


SUBLANES = 8


def _row_copy(src, dst, sem):
    return pltpu.make_async_copy(src, dst, sem)


def _store_rows_as_tiles(ref, val):
    rows = val.shape[0]
    for j in range(SUBLANES):
        ref[pl.ds(j, rows, stride=SUBLANES), :] = val[:, j * LANES:(j + 1) * LANES]


def _load_rows_from_tiles(ref):
    rows = ref.shape[0] // SUBLANES
    return jnp.concatenate([ref[pl.ds(j, rows, stride=SUBLANES), :] for j in range(SUBLANES)], axis=1)


def _tile_of(ref, row):
    if isinstance(row, int):
        return ref.at[pl.ds(row * SUBLANES, SUBLANES)]
    return ref.at[pl.ds(pl.multiple_of(row * SUBLANES, SUBLANES), SUBLANES)]


def _dispatch_kernel(zs_ref, zc_ref, slot_ref, h_ref, xs_ref, zero_scr, sem, *, tm):
    t = pl.program_id(0)

    @pl.when(t == 0)
    def _():
        zero_scr[...] = jnp.zeros(zero_scr.shape, F32)

        def per_expert(e, carry):
            start = zs_ref[e]
            n = zc_ref[e]

            def issue(r, c2):
                _row_copy(zero_scr, _tile_of(xs_ref, start + r), sem).start()
                return c2

            lax.fori_loop(0, n, issue, 0)

            def drain(r, c2):
                _row_copy(zero_scr, _tile_of(xs_ref, 0), sem).wait()
                return c2

            lax.fori_loop(0, n, drain, 0)
            return carry

        lax.fori_loop(0, N_EXPERTS, per_expert, 0)

    def issue(r, carry):
        for k in range(TOP_K):
            s = slot_ref[r * TOP_K + k]
            _row_copy(_tile_of(h_ref, r), _tile_of(xs_ref, s), sem).start(priority=k % 2)
        return carry

    lax.fori_loop(0, tm, issue, 0)

    def drain(r, carry):
        for k in range(TOP_K):
            _row_copy(_tile_of(h_ref, 0), _tile_of(xs_ref, 0), sem).wait()
        return carry

    lax.fori_loop(0, tm, drain, 0)


def _dispatch(h2, slot_flat, zstart, zcount, n_slots, geo):
    sub, lanes = SUBLANES, h2.shape[1]
    nt = h2.shape[0] // sub
    tm = geo["tm"]
    return pl.pallas_call(
        functools.partial(_dispatch_kernel, tm=tm),
        grid_spec=pltpu.PrefetchScalarGridSpec(
            num_scalar_prefetch=2,
            grid=(nt // tm,),
            in_specs=[pl.BlockSpec((tm * TOP_K,), lambda i, zs, zc: (i,), memory_space=pltpu.SMEM),
                      pl.BlockSpec((tm * sub, lanes), lambda i, zs, zc: (i, 0))],
            out_specs=pl.BlockSpec(memory_space=pl.ANY),
            scratch_shapes=[pltpu.VMEM((sub, lanes), F32), pltpu.SemaphoreType.DMA(())]),
        out_shape=jax.ShapeDtypeStruct((n_slots * sub, lanes), F32),
        compiler_params=_params(1),
        name="moe_dispatch",
    )(zstart, zcount, slot_flat, h2)


def _expert_kernel(be_ref, nu_ref, x_ref, wgu_ref, bgu_ref, wd_ref, bd_ref, y_ref, wgu_bf, wd_bf):
    i = pl.program_id(0)

    @pl.when(jnp.logical_or(i == 0, be_ref[i] != be_ref[jnp.maximum(i - 1, 0)]))
    def _():
        wgu_bf[...] = wgu_ref[...].astype(BF16)
        wd_bf[...] = wd_ref[...].astype(BF16)

    @pl.when(i < nu_ref[0])
    def _():
        x = _load_rows_from_tiles(x_ref).astype(BF16)
        gu = _dot(x, wgu_bf[...]) + bgu_ref[...]
        glu = jnp.minimum(gu[:, :D_EXPERT], SWIGLU_LIMIT)
        lin = jnp.clip(gu[:, D_EXPERT:], -SWIGLU_LIMIT, SWIGLU_LIMIT)
        act = glu * (1.0 / (1.0 + jnp.exp(-SWIGLU_ALPHA * glu))) * (lin + 1.0)
        _store_rows_as_tiles(y_ref, _dot(act.astype(BF16), wd_bf[...]) + bd_ref[...])


def _experts(xs, block_expert, n_used, layer, wgu, bgu, wd, bd):
    sub, lanes = SUBLANES, xs.shape[1]
    ns = xs.shape[0] // sub
    nblk = ns // MOE_BLOCK
    depth, ne, d, f2 = wgu.shape
    f = wd.shape[2]
    blk = lambda i, be, nu: (jnp.minimum(i, nu[0] - 1), 0)
    wsel = lambda i, be, nu: (layer, be[i], 0, 0)
    return pl.pallas_call(
        _expert_kernel,
        grid_spec=pltpu.PrefetchScalarGridSpec(
            num_scalar_prefetch=2,
            grid=(nblk,),
            in_specs=[pl.BlockSpec((MOE_BLOCK * sub, lanes), blk),
                      pl.BlockSpec((None, None, d, f2), wsel),
                      pl.BlockSpec((None, None, 1, f2), wsel),
                      pl.BlockSpec((None, None, f, d), wsel),
                      pl.BlockSpec((None, None, 1, d), wsel)],
            out_specs=pl.BlockSpec((MOE_BLOCK * sub, lanes), blk),
            scratch_shapes=[pltpu.VMEM((d, f2), BF16), pltpu.VMEM((f, d), BF16)]),
        out_shape=jax.ShapeDtypeStruct((ns * sub, lanes), F32),
        compiler_params=_params(1),
        name="moe_experts",
    )(block_expert, n_used, xs, wgu, bgu.reshape(depth, ne, 1, f2), wd, bd.reshape(depth, ne, 1, d))


def _combine_kernel(slot_ref, next_slot_ref, gate_ref, x_ref, mod_ref, y_ref, xo_ref, ybuf, sems, *, tm):
    t = pl.program_id(0)
    cur = t % 2

    def gather(slots, buf):
        def issue(r, carry):
            for k in range(TOP_K):
                s = slots[r * TOP_K + k]
                _row_copy(_tile_of(y_ref, s), _tile_of(ybuf.at[buf, k], r), sems.at[buf]).start(priority=k % 2)
            return carry
        lax.fori_loop(0, tm, issue, 0)

    @pl.when(t == 0)
    def _():
        gather(slot_ref, 0)

    @pl.when(t + 1 < pl.num_programs(0))
    def _():
        gather(next_slot_ref, 1 - cur)

    def drain(r, carry):
        for k in range(TOP_K):
            _row_copy(_tile_of(y_ref, 0), _tile_of(ybuf.at[cur, k], 0), sems.at[cur]).wait()
        return carry

    lax.fori_loop(0, tm, drain, 0)
    gates = gate_ref[...]
    f = gates[:, 0:1] * _load_rows_from_tiles(ybuf.at[cur, 0])
    for k in range(1, TOP_K):
        f = f + gates[:, k:k + 1] * _load_rows_from_tiles(ybuf.at[cur, k])
    xo_ref[...] = x_ref[...] + mod_ref[5:6, :] * f


def _combine(y, slot_flat, gates, xs, mod_l, geo):
    nt, d = xs.shape
    tm, per, nb = geo["tm"], geo["per"], geo["b"]
    last = nt // tm - 1
    return pl.pallas_call(
        functools.partial(_combine_kernel, tm=tm),
        grid=(nt // tm,),
        in_specs=[pl.BlockSpec((tm * TOP_K,), lambda i: (i,), memory_space=pltpu.SMEM),
                  pl.BlockSpec((tm * TOP_K,), lambda i: (jnp.minimum(i + 1, last),), memory_space=pltpu.SMEM),
                  pl.BlockSpec((tm, LANES), lambda i: (i, 0)),
                  pl.BlockSpec((tm, d), lambda i: (i, 0)),
                  pl.BlockSpec((None, 6, d), lambda i: (jnp.minimum(i // per, nb), 0, 0)),
                  pl.BlockSpec(memory_space=pl.ANY)],
        out_specs=pl.BlockSpec((tm, d), lambda i: (i, 0)),
        out_shape=jax.ShapeDtypeStruct((nt, d), F32),
        scratch_shapes=[pltpu.VMEM((2, TOP_K, tm * SUBLANES, LANES), F32), pltpu.SemaphoreType.DMA((2,))],
        input_output_aliases={3: 0},
        compiler_params=_params(1),
        name="moe_combine",
    )(slot_flat, slot_flat, gates, xs, mod_l, y)


def _moe(h2, top_idx, gates, rank, counts, xs, mod_l, layer, wgu, bgu, wd, bd, geo):
    nt = xs.shape[0]
    n_assign = nt * TOP_K
    nblk = n_assign // MOE_BLOCK + N_EXPERTS
    n_slots = nblk * MOE_BLOCK
    counts = counts.astype(jnp.int32)
    padded = (counts + MOE_BLOCK - 1) // MOE_BLOCK * MOE_BLOCK
    pad_end = jnp.cumsum(padded)
    pad_start = pad_end - padded
    experts = jnp.arange(N_EXPERTS, dtype=jnp.int32)
    idx4 = top_idx[:, :TOP_K]
    base = jnp.sum(jnp.where(idx4[:, :, None] == experts[None, None, :], pad_start[None, None, :], 0), axis=-1)
    slot_flat = (base + rank[:, :TOP_K]).reshape(n_assign).astype(jnp.int32)
    n_used = (pad_end[-1] // MOE_BLOCK).astype(jnp.int32)
    blk_id = jnp.minimum(jnp.arange(nblk, dtype=jnp.int32), n_used - 1)
    block_expert = jnp.minimum(
        jnp.sum((pad_end[None, :] <= (blk_id * MOE_BLOCK)[:, None]).astype(jnp.int32), axis=1), N_EXPERTS - 1)
    xs_slots = _dispatch(h2, slot_flat, (pad_start + counts).astype(jnp.int32),
                         (padded - counts).astype(jnp.int32), n_slots, geo)
    y = _experts(xs_slots, block_expert.astype(jnp.int32), n_used.reshape(1), layer, wgu, bgu, wd, bd)
    return _combine(y, slot_flat, gates, xs, mod_l, geo)


def _final_kernel(x_ref, g_ref, o_ref):
    o_ref[...] = _rms(x_ref[...]) * g_ref[...]


def _final_norm(xs, g, nlat, tm):
    d = xs.shape[1]
    return pl.pallas_call(
        _final_kernel,
        grid=(nlat // tm,),
        in_specs=[pl.BlockSpec((tm, d), lambda i: (i, 0)), pl.BlockSpec((1, d), lambda i: (0, 0))],
        out_specs=pl.BlockSpec((tm, d), lambda i: (i, 0)),
        out_shape=jax.ShapeDtypeStruct((nlat, d), F32),
        compiler_params=_params(1),
        name="final_norm",
    )(xs, g)


def _with_identity_rows(cos, sin, tm):
    ones = jnp.ones((tm, cos.shape[1]), F32)
    return jnp.concatenate([cos, ones], axis=0), jnp.concatenate([sin, 0.0 * ones], axis=0)


def _axial_tables(seq, tm, pad_to_lanes):
    pos = jnp.arange(seq)
    rows = (pos // GRID_W).astype(F32)
    cols = (pos % GRID_W).astype(F32)
    half = DA_HEAD_DIM // 2
    inv = ROPE_BASE ** (-jnp.arange(0, half, 2, dtype=F32) / half)
    ar = rows[:, None] * inv[None, :]
    ac = cols[:, None] * inv[None, :]
    cos64 = jnp.concatenate([jnp.cos(ar), jnp.cos(ar), jnp.cos(ac), jnp.cos(ac)], axis=1)
    sin64 = jnp.concatenate([-jnp.sin(ar), jnp.sin(ar), -jnp.sin(ac), jnp.sin(ac)], axis=1)
    if pad_to_lanes:
        cos = jnp.concatenate([cos64, jnp.ones_like(cos64)], axis=1)
        sin = jnp.concatenate([sin64, jnp.zeros_like(sin64)], axis=1)
    else:
        cos = jnp.concatenate([cos64, cos64], axis=1)
        sin = jnp.concatenate([sin64, sin64], axis=1)
    return _with_identity_rows(cos, sin, tm)


def _ret_tables(seq, tm):
    inv = ROPE_BASE ** (-jnp.arange(0, RET_KDIM, 2, dtype=F32) / RET_KDIM)
    ang = jnp.arange(seq).astype(F32)[:, None] * inv[None, :]
    return _with_identity_rows(jnp.cos(ang), jnp.sin(ang), tm)


def _mla_weights(w_in, w_q_up, w_kv_up):
    d = w_in.shape[0]
    kvw = MLA_KV_RANK + MLA_ROPE
    w_in_r = jnp.concatenate([w_in[:, :MLA_KV_RANK], w_in[:, kvw:], w_in[:, MLA_KV_RANK:kvw],
                              jnp.zeros((d, LANES - MLA_ROPE), w_in.dtype)], axis=1)
    wq = w_q_up.reshape(MLA_Q_RANK, MLA_HEADS, MLA_NOPE + MLA_ROPE)
    wq = jnp.concatenate([wq, jnp.zeros((MLA_Q_RANK, MLA_HEADS, LANES - MLA_ROPE), wq.dtype)], axis=2)
    wq = wq.reshape(MLA_Q_RANK, MLA_HEADS * 2 * LANES)
    wkv = w_kv_up.reshape(MLA_KV_RANK, MLA_HEADS, MLA_NOPE + MLA_V)
    wkv = jnp.concatenate([wkv[:, :, :MLA_NOPE].reshape(MLA_KV_RANK, -1),
                           wkv[:, :, MLA_NOPE:].reshape(MLA_KV_RANK, -1)], axis=1)
    return w_in_r.astype(BF16), wq.astype(BF16), wkv.astype(BF16)


def kernel(x, c, ctx, c_ctx, ada_w, ada_b, norm_g, final_g, da_w_in, da_w_out, da_lambda, da_subln_g,
           ret_w_in, ret_decay_logit, ret_w_out, mla_w_in, mla_q_norm_g, mla_w_q_up, mla_kv_norm_g,
           mla_w_kv_up, mla_w_out, moe_w_router, moe_b_router, moe_w_gate_up, moe_b_gate_up,
           moe_w_down, moe_b_down):
    nb, seq, d = x.shape
    nctx_per = ctx.shape[1]
    depth = ada_w.shape[0]
    nlat = nb * seq
    tm = _pick_tile(512, seq, nb * nctx_per)
    geo = dict(b=nb, seq=seq, ctx=nctx_per, nlat=nlat, tm=tm, per=seq // tm, nlt=nlat // tm)
    assert nb < MOD_ROWS and nlat % nctx_per == 0 and nctx_per % RET_CHUNK == 0

    xs = jnp.concatenate([x.reshape(nlat, d), ctx.reshape(nb * nctx_per, d)], axis=0).astype(F32)
    cond = jnp.zeros((MOD_ROWS, d), F32).at[:nb].set(c).at[nb].set(c_ctx)
    mod = _adaln(cond, ada_w, ada_b).reshape(depth, MOD_ROWS, 6, d)

    cos_a, sin_a = _axial_tables(seq, tm, pad_to_lanes=False)
    cos_m, sin_m = _axial_tables(seq, tm, pad_to_lanes=True)
    cos_r, sin_r = _ret_tables(seq, tm)

    for i in range(depth):
        kind = i % N_MIXERS
        j = i // N_MIXERS
        mod_l = mod[i]
        g1 = norm_g[i, 0].reshape(1, d)
        g2 = norm_g[i, 1].reshape(1, d)
        if kind == 0:
            lam_init = 0.8 - 0.6 * math.exp(-0.3 * i)
            p = _proj(xs, g1, mod_l, da_w_in[j].astype(BF16), cos_a, sin_a,
                      ["rope16", "plain", "rope16"], geo)
            o_lat, o_ctx = _da_attention(p, da_lambda[j].astype(F32),
                                         da_subln_g[j].reshape(1, DA_V_DIM).astype(F32), lam_init, geo)
            ctx_off = 0
            w_out = da_w_out[j]
        elif kind == 1:
            p = _proj(xs, g1, mod_l, ret_w_in[j].astype(BF16), cos_r, sin_r,
                      ["ret_k", "plain", "plain", "ret_q", "plain", "plain"], geo,
                      ret_scale=RET_KDIM ** -0.5)
            o_lat = o_ctx = _retention(p, ret_decay_logit[j], geo)
            ctx_off = geo["nlt"]
            w_out = ret_w_out[j]
        else:
            w_in_r, wq, wkv = _mla_weights(mla_w_in[j], mla_w_q_up[j], mla_w_kv_up[j])
            q, kn, v, kr = _mla_proj(xs, g1, mod_l, w_in_r, mla_q_norm_g[j].reshape(1, -1).astype(F32),
                                     mla_kv_norm_g[j].reshape(1, -1).astype(F32), wq, wkv, cos_m, sin_m, geo)
            o_lat, o_ctx = _mla_attention(q, kn, v, kr, geo)
            ctx_off = 0
            w_out = mla_w_out[j]
        xs, h2, top_idx, gates, rank, counts = _post(o_lat, o_ctx, ctx_off, w_out.astype(BF16), xs, mod_l, g2,
                                                     moe_w_router[i], moe_b_router[i], geo)
        xs = _moe(h2, top_idx, gates, rank, counts[0, :N_EXPERTS], xs, mod_l, i,
                  moe_w_gate_up, moe_b_gate_up, moe_w_down, moe_b_down, geo)
    return _final_norm(xs, final_g.reshape(1, d).astype(F32), nlat, tm).reshape(nb, seq, d)
```

```python
import functools
import math

import jax
import jax.numpy as jnp
from jax import lax
from jax.experimental import pallas as pl
from jax.experimental.pallas import tpu as pltpu

F32 = jnp.float32
BF16 = jnp.bfloat16

D_MODEL = 1024
GRID_W = 64
RMS_EPS = 1e-6
ROPE_BASE = 10000.0
N_MIXERS = 3

DA_HEADS = 8
DA_HEAD_DIM = 64
DA_V_DIM = 128

RET_HEADS = 4
RET_KDIM = 256
RET_VDIM = 512
RET_CHUNK = 128

MLA_HEADS = 8
MLA_NOPE = 128
MLA_ROPE = 64
MLA_V = 128
MLA_Q_RANK = 256
MLA_KV_RANK = 128

N_EXPERTS = 32
TOP_K = 4
D_EXPERT = 1024
SWIGLU_LIMIT = 7.0
SWIGLU_ALPHA = 1.702

LANES = 128
MOD_ROWS = 16
MOE_BLOCK = 512
COL_CHUNK = 1024
ATTN_TQ = 1024
ATTN_TK = 1024
ATTN_ROW_SPLIT = 8
VMEM_LIMIT = 56 * 1024 * 1024
NEG_BIG = -1e30


def _params(n_axes):
    return pltpu.CompilerParams(dimension_semantics=("arbitrary",) * n_axes,
                                vmem_limit_bytes=VMEM_LIMIT)


def _pick_tile(cap, *dims):
    t = cap
    while any(d % t for d in dims):
        t //= 2
    return t


def _rms(x):
    return x * lax.rsqrt(jnp.mean(x * x, axis=-1, keepdims=True) + RMS_EPS)


def _norm_mod(x, g, shift, scale):
    return (_rms(x) * g) * (1.0 + scale) + shift


def _dot(a, b):
    return jnp.dot(a, b, preferred_element_type=F32)


def _dot_split(a, b_ref):
    half = b_ref.shape[1] // 2
    return jnp.concatenate([_dot(a, b_ref[:, :half]), _dot(a, b_ref[:, half:])], axis=1)


def _dot_nt(a, b):
    return lax.dot_general(a, b, (((1,), (1,)), ((), ())), preferred_element_type=F32)


def _dot_tn(a, b):
    return lax.dot_general(a, b, (((0,), (0,)), ((), ())), preferred_element_type=F32)


def _adaln_kernel(c_ref, w_ref, b_ref, o_ref):
    c = c_ref[...]
    a = c * (1.0 / (1.0 + jnp.exp(-c)))
    o_ref[...] = jnp.dot(a, w_ref[...], precision=lax.Precision.HIGHEST,
                         preferred_element_type=F32) + b_ref[...]


def _adaln(cond, ada_w, ada_b):
    depth, d, n = ada_w.shape
    tn = COL_CHUNK
    return pl.pallas_call(
        _adaln_kernel,
        grid=(depth, n // tn),
        in_specs=[pl.BlockSpec((MOD_ROWS, d), lambda l, j: (0, 0)),
                  pl.BlockSpec((None, d, tn), lambda l, j: (l, 0, j)),
                  pl.BlockSpec((None, 1, tn), lambda l, j: (l, 0, j))],
        out_specs=pl.BlockSpec((None, MOD_ROWS, tn), lambda l, j: (l, 0, j)),
        out_shape=jax.ShapeDtypeStruct((depth, MOD_ROWS, n), F32),
        compiler_params=_params(2),
        name="adaln",
    )(cond, ada_w, ada_b.reshape(depth, 1, n))


def _rope_pairs16(x, cos, sin):
    lane = lax.broadcasted_iota(jnp.int32, x.shape, 1)
    first = (lane % 32) < 16
    partner = jnp.where(first, pltpu.roll(x, LANES - 16, 1), pltpu.roll(x, 16, 1))
    return x * cos + partner * sin


def _proj_kernel(x_ref, g_ref, mod_ref, w_ref, cos_ref, sin_ref, o_ref, h_scr, *, kinds, ret_scale):
    j = pl.program_id(1)

    @pl.when(j == 0)
    def _():
        h = _norm_mod(x_ref[...], g_ref[...], mod_ref[0:1, :], mod_ref[1:2, :])
        h_scr[...] = h.astype(BF16)

    def chunks_of(kind):
        cs = [c for c, k in enumerate(kinds) if k == kind]
        pred = None
        for c in cs:
            pred = (j == c) if pred is None else jnp.logical_or(pred, j == c)
        return pred

    def acc():
        return _dot_split(h_scr[...], w_ref)

    if "plain" in kinds:
        @pl.when(chunks_of("plain"))
        def _():
            o_ref[...] = acc().astype(BF16)

    if "rope16" in kinds:
        @pl.when(chunks_of("rope16"))
        def _():
            a = acc()
            cos = cos_ref[...]
            sin = sin_ref[...]
            for g in range(COL_CHUNK // LANES):
                sl = slice(g * LANES, (g + 1) * LANES)
                o_ref[:, sl] = _rope_pairs16(a[:, sl], cos, sin).astype(BF16)

    for kind, scale in (("ret_k", ret_scale), ("ret_q", 1.0)):
        if kind in kinds:
            @pl.when(chunks_of(kind))
            def _(scale=scale):
                a = acc()
                cos = cos_ref[...]
                sin = sin_ref[...]
                for hh in range(COL_CHUNK // RET_KDIM):
                    lo = slice(hh * RET_KDIM, hh * RET_KDIM + LANES)
                    hi = slice(hh * RET_KDIM + LANES, (hh + 1) * RET_KDIM)
                    x1 = a[:, lo]
                    x2 = a[:, hi]
                    o_ref[:, lo] = ((x1 * cos - x2 * sin) * scale).astype(BF16)
                    o_ref[:, hi] = ((x1 * sin + x2 * cos) * scale).astype(BF16)


def _proj(xs, g, mod_l, w, cos_t, sin_t, kinds, geo, ret_scale=1.0):
    nt, d = xs.shape
    n = w.shape[1]
    tm, per, nlt, nb = geo["tm"], geo["per"], geo["nlt"], geo["b"]
    return pl.pallas_call(
        functools.partial(_proj_kernel, kinds=tuple(kinds), ret_scale=ret_scale),
        grid=(nt // tm, n // COL_CHUNK),
        in_specs=[pl.BlockSpec((tm, d), lambda i, j: (i, 0)),
                  pl.BlockSpec((1, d), lambda i, j: (0, 0)),
                  pl.BlockSpec((None, 6, d), lambda i, j: (jnp.minimum(i // per, nb), 0, 0)),
                  pl.BlockSpec((d, COL_CHUNK), lambda i, j: (0, j)),
                  pl.BlockSpec((tm, LANES), lambda i, j: (jnp.where(i < nlt, i % per, per), 0)),
                  pl.BlockSpec((tm, LANES), lambda i, j: (jnp.where(i < nlt, i % per, per), 0))],
        out_specs=pl.BlockSpec((tm, COL_CHUNK), lambda i, j: (i, j)),
        out_shape=jax.ShapeDtypeStruct((nt, n), BF16),
        scratch_shapes=[pltpu.VMEM((tm, d), BF16)],
        compiler_params=_params(2),
        name="proj",
    )(xs, g, mod_l, w, cos_t, sin_t)


def _mla_proj_kernel(x_ref, g_ref, mod_ref, win_ref, qg_ref, kvg_ref, wq_ref, wkv_ref, cos_ref, sin_ref,
                     q_ref, kn_ref, v_ref, kr_ref):
    h = _norm_mod(x_ref[...], g_ref[...], mod_ref[0:1, :], mod_ref[1:2, :]).astype(BF16)
    p = _dot(h, win_ref[...])
    cos = cos_ref[...]
    sin = sin_ref[...]
    ckv = (_rms(p[:, :MLA_KV_RANK]) * kvg_ref[...]).astype(BF16)
    kv = _dot(ckv, wkv_ref[...])
    nk = MLA_HEADS * MLA_NOPE
    kn_ref[...] = kv[:, :nk].astype(BF16)
    v_ref[...] = kv[:, nk:].astype(BF16)
    kr_ref[...] = _rope_pairs16(p[:, MLA_KV_RANK + MLA_Q_RANK:], cos, sin).astype(BF16)
    cq = (_rms(p[:, MLA_KV_RANK:MLA_KV_RANK + MLA_Q_RANK]) * qg_ref[...]).astype(BF16)
    q = _dot(cq, wq_ref[...])
    for hh in range(MLA_HEADS):
        lo = slice(hh * 2 * LANES, hh * 2 * LANES + LANES)
        hi = slice(hh * 2 * LANES + LANES, (hh + 1) * 2 * LANES)
        q_ref[:, lo] = q[:, lo].astype(BF16)
        q_ref[:, hi] = _rope_pairs16(q[:, hi], cos, sin).astype(BF16)


def _mla_proj(xs, g, mod_l, w_in, qg, kvg, wq, wkv, cos_t, sin_t, geo):
    nt, d = xs.shape
    tm, per, nlt, nb = geo["tm"], geo["per"], geo["nlt"], geo["b"]
    full = lambda a: pl.BlockSpec(a.shape, lambda i: (0,) * a.ndim)
    tab = pl.BlockSpec((tm, LANES), lambda i: (jnp.where(i < nlt, i % per, per), 0))
    nq = MLA_HEADS * 2 * LANES
    nk = MLA_HEADS * MLA_NOPE
    return pl.pallas_call(
        _mla_proj_kernel,
        grid=(nt // tm,),
        in_specs=[pl.BlockSpec((tm, d), lambda i: (i, 0)), full(g),
                  pl.BlockSpec((None, 6, d), lambda i: (jnp.minimum(i // per, nb), 0, 0)),
                  full(w_in), full(qg), full(kvg), full(wq), full(wkv), tab, tab],
        out_specs=[pl.BlockSpec((tm, nq), lambda i: (i, 0)),
                   pl.BlockSpec((tm, nk), lambda i: (i, 0)),
                   pl.BlockSpec((tm, nk), lambda i: (i, 0)),
                   pl.BlockSpec((tm, LANES), lambda i: (i, 0))],
        out_shape=[jax.ShapeDtypeStruct((nt, nq), BF16), jax.ShapeDtypeStruct((nt, nk), BF16),
                   jax.ShapeDtypeStruct((nt, nk), BF16), jax.ShapeDtypeStruct((nt, LANES), BF16)],
        compiler_params=_params(1),
        name="mla_proj",
    )(xs, g, mod_l, w_in, qg, kvg, wq, wkv, cos_t, sin_t)


LOG2E = 1.4426950408889634


def _flash_init(m_ref, a_ref):
    m_ref[...] = jnp.full(m_ref.shape, -jnp.inf, F32)
    a_ref[...] = jnp.zeros(a_ref.shape, F32)


def _ones_column(rows):
    lane = lax.broadcasted_iota(jnp.int32, (rows, LANES), 1)
    return jnp.where(lane == 0, 1.0, 0.0).astype(BF16)


def _chunk_rows(j, tk):
    if isinstance(j, int):
        return pl.ds(j * tk, tk)
    return pl.ds(pl.multiple_of(j * tk, tk), tk)


def _softmax_pv(parts, stats):
    for mp, (m_ref, a_ref) in enumerate(stats):
        tq = m_ref.shape[0]
        rb = tq // ATTN_ROW_SPLIT if tq % ATTN_ROW_SPLIT == 0 else tq
        for r0 in range(0, tq, rb):
            rows = slice(r0, r0 + rb)
            tiles = [scores[mp][rows, :] for scores, _ in parts]
            m_prev = m_ref[rows, :]
            m_new = m_prev
            for s2 in tiles:
                m_new = jnp.maximum(m_new, jnp.max(s2, axis=1, keepdims=True))
            acc = jnp.exp2(m_prev - m_new) * a_ref[rows, :]
            for s2, (_, v_aug) in zip(tiles, parts):
                acc = acc + _dot(jnp.exp2(s2 - m_new).astype(BF16), v_aug)
            a_ref[rows, :] = acc
            m_ref[rows, :] = m_new


def _flash_lat(qk_fns, stats, k_ctx, v_ctx, k_at, v_at, s_a, s_b, s_c, n_chunks):
    def qk_into(dst, k):
        for mp, f in enumerate(qk_fns):
            dst[mp] = f(k)

    qk_into(s_c, k_ctx)
    qk_into(s_a, k_at(0))
    qk_into(s_b, k_at(1))
    _softmax_pv([(s_c, v_ctx), (s_a, v_at(0))], stats)

    def body(t, carry):
        j = 2 * t + 1
        qk_into(s_a, k_at(j + 1))
        _softmax_pv([(s_b, v_at(j))], stats)
        qk_into(s_b, k_at(j + 2))
        _softmax_pv([(s_a, v_at(j + 1))], stats)
        return carry

    lax.fori_loop(0, (n_chunks - 2) // 2, body, 0)
    _softmax_pv([(s_b, v_at(n_chunks - 1))], stats)


def _flash_result(a_ref, vdim):
    a = a_ref[...]
    return a[:, :vdim] / a[:, vdim:vdim + 1]


def _da_attn_kernel(*refs, n_chunks, tk, lam_init, scale):
    if n_chunks:
        (lam_ref, sg_ref, q_ref, kc_ref, vc_ref, kl_ref, vl_ref, o_ref, m0, a0, m1, a1, s_a, s_b, s_c) = refs
    else:
        (lam_ref, sg_ref, q_ref, kc_ref, vc_ref, o_ref, m0, a0, m1, a1) = refs
    q = q_ref[...]
    lane = lax.broadcasted_iota(jnp.int32, q.shape, 1)
    zero = jnp.zeros_like(q)
    q_lo = jnp.where(lane < DA_HEAD_DIM, q, zero)
    q_hi = jnp.where(lane >= DA_HEAD_DIM, q, zero)
    _flash_init(m0, a0)
    _flash_init(m1, a1)
    c2 = scale * LOG2E
    stats = [(m0, a0), (m1, a1)]
    qk_fns = [lambda k: _dot_nt(q_lo, k) * c2, lambda k: _dot_nt(q_hi, k) * c2]
    v_ctx = jnp.concatenate([vc_ref[...], _ones_column(vc_ref.shape[0])], axis=1)
    if n_chunks:
        ones = _ones_column(tk)

        def k_at(j):
            return kl_ref[_chunk_rows(j, tk), :]

        def v_at(j):
            return jnp.concatenate([vl_ref[_chunk_rows(j, tk), :], ones], axis=1)

        _flash_lat(qk_fns, stats, kc_ref[...], v_ctx, k_at, v_at, s_a, s_b, s_c, n_chunks)
    else:
        _softmax_pv([([f(kc_ref[...]) for f in qk_fns], v_ctx)], stats)

    lf = lam_ref[...]
    lam = (jnp.exp(jnp.sum(lf[0:1] * lf[1:2], axis=1, keepdims=True))
           - jnp.exp(jnp.sum(lf[2:3] * lf[3:4], axis=1, keepdims=True)) + lam_init)
    o = _flash_result(a0, DA_V_DIM) - lam * _flash_result(a1, DA_V_DIM)
    o = _rms(o) * sg_ref[...] * (1.0 - lam_init)
    o_ref[...] = o.astype(BF16)


def _attn_scratch(tq, vdim, n_maps, tk=0, ctx=0):
    s = []
    for _ in range(n_maps):
        s += [pltpu.VMEM((tq, 1), F32), pltpu.VMEM((tq, vdim + LANES), F32)]
    if tk:
        s += [pltpu.VMEM((n_maps, tq, tk), F32), pltpu.VMEM((n_maps, tq, tk), F32),
              pltpu.VMEM((n_maps, tq, ctx), F32)]
    return s


def _da_attention(p, lam_vecs, subg, lam_init, geo):
    nt = p.shape[0]
    nb, seq, ctx, nlat = geo["b"], geo["seq"], geo["ctx"], geo["nlat"]
    hh = DA_HEADS
    tq = _pick_tile(ATTN_TQ, seq)
    tk = _pick_tile(min(ATTN_TK, seq // 2), seq)
    assert (seq // tk) % 2 == 0
    nqt = seq // tq
    cb = nlat // ctx
    scale = DA_HEAD_DIM ** -0.5
    small = [pl.BlockSpec(lam_vecs.shape, lambda *a: (0, 0)), pl.BlockSpec(subg.shape, lambda *a: (0, 0))]
    o_lat = pl.pallas_call(
        functools.partial(_da_attn_kernel, n_chunks=seq // tk, tk=tk, lam_init=lam_init, scale=scale),
        grid=(nb, hh, nqt),
        in_specs=small + [
            pl.BlockSpec((tq, LANES), lambda b, h, i: (b * nqt + i, 2 * hh + h)),
            pl.BlockSpec((ctx, LANES), lambda b, h, i: (cb + b, h)),
            pl.BlockSpec((ctx, LANES), lambda b, h, i: (cb + b, hh + h)),
            pl.BlockSpec((seq, LANES), lambda b, h, i: (b, h)),
            pl.BlockSpec((seq, LANES), lambda b, h, i: (b, hh + h))],
        out_specs=pl.BlockSpec((tq, LANES), lambda b, h, i: (b * nqt + i, h)),
        out_shape=jax.ShapeDtypeStruct((nlat, hh * DA_V_DIM), BF16),
        scratch_shapes=_attn_scratch(tq, DA_V_DIM, 2, tk, ctx),
        compiler_params=_params(3),
        name="da_attn_lat",
    )(lam_vecs, subg, p, p, p, p, p)
    o_ctx = pl.pallas_call(
        functools.partial(_da_attn_kernel, n_chunks=0, tk=tk, lam_init=lam_init, scale=scale),
        grid=(nb, hh),
        in_specs=small + [
            pl.BlockSpec((ctx, LANES), lambda b, h: (cb + b, 2 * hh + h)),
            pl.BlockSpec((ctx, LANES), lambda b, h: (cb + b, h)),
            pl.BlockSpec((ctx, LANES), lambda b, h: (cb + b, hh + h))],
        out_specs=pl.BlockSpec((ctx, LANES), lambda b, h: (b, h)),
        out_shape=jax.ShapeDtypeStruct((nt - nlat, hh * DA_V_DIM), BF16),
        scratch_shapes=_attn_scratch(ctx, DA_V_DIM, 2),
        compiler_params=_params(2),
        name="da_attn_ctx",
    )(lam_vecs, subg, p, p, p)
    return o_lat, o_ctx


def _mla_attn_kernel(*refs, n_chunks, tk, scale):
    if n_chunks:
        (q_ref, knc_ref, krc_ref, vc_ref, knl_ref, krl_ref, vl_ref, o_ref, m0, a0, s_a, s_b, s_c) = refs
    else:
        (q_ref, knc_ref, krc_ref, vc_ref, o_ref, m0, a0) = refs
    q = q_ref[...]
    _flash_init(m0, a0)
    c2 = scale * LOG2E
    stats = [(m0, a0)]
    qk_fns = [lambda k: _dot_nt(q, k) * c2]
    k_ctx = jnp.concatenate([knc_ref[...], krc_ref[...]], axis=1)
    v_ctx = jnp.concatenate([vc_ref[...], _ones_column(vc_ref.shape[0])], axis=1)
    if n_chunks:
        ones = _ones_column(tk)

        def k_at(j):
            rows = _chunk_rows(j, tk)
            return jnp.concatenate([knl_ref[rows, :], krl_ref[rows, :]], axis=1)

        def v_at(j):
            return jnp.concatenate([vl_ref[_chunk_rows(j, tk), :], ones], axis=1)

        _flash_lat(qk_fns, stats, k_ctx, v_ctx, k_at, v_at, s_a, s_b, s_c, n_chunks)
    else:
        _softmax_pv([([qk_fns[0](k_ctx)], v_ctx)], stats)
    o_ref[...] = _flash_result(a0, MLA_V).astype(BF16)


def _mla_attention(q, kn, v, kr, geo):
    nt = q.shape[0]
    nb, seq, ctx, nlat = geo["b"], geo["seq"], geo["ctx"], geo["nlat"]
    hh = MLA_HEADS
    tq = _pick_tile(ATTN_TQ, seq)
    tk = _pick_tile(min(ATTN_TK, seq // 2), seq)
    assert (seq // tk) % 2 == 0
    nqt = seq // tq
    cb = nlat // ctx
    scale = (MLA_NOPE + MLA_ROPE) ** -0.5
    o_lat = pl.pallas_call(
        functools.partial(_mla_attn_kernel, n_chunks=seq // tk, tk=tk, scale=scale),
        grid=(nb, hh, nqt),
        in_specs=[
            pl.BlockSpec((tq, 2 * LANES), lambda b, h, i: (b * nqt + i, h)),
            pl.BlockSpec((ctx, LANES), lambda b, h, i: (cb + b, h)),
            pl.BlockSpec((ctx, LANES), lambda b, h, i: (cb + b, 0)),
            pl.BlockSpec((ctx, LANES), lambda b, h, i: (cb + b, h)),
            pl.BlockSpec((seq, LANES), lambda b, h, i: (b, h)),
            pl.BlockSpec((seq, LANES), lambda b, h, i: (b, 0)),
            pl.BlockSpec((seq, LANES), lambda b, h, i: (b, h))],
        out_specs=pl.BlockSpec((tq, LANES), lambda b, h, i: (b * nqt + i, h)),
        out_shape=jax.ShapeDtypeStruct((nlat, hh * MLA_V), BF16),
        scratch_shapes=_attn_scratch(tq, MLA_V, 1, tk, ctx),
        compiler_params=_params(3),
        name="mla_attn_lat",
    )(q, kn, kr, v, kn, kr, v)
    o_ctx = pl.pallas_call(
        functools.partial(_mla_attn_kernel, n_chunks=0, tk=tk, scale=scale),
        grid=(nb, hh),
        in_specs=[
            pl.BlockSpec((ctx, 2 * LANES), lambda b, h: (cb + b, h)),
            pl.BlockSpec((ctx, LANES), lambda b, h: (cb + b, h)),
            pl.BlockSpec((ctx, LANES), lambda b, h: (cb + b, 0)),
            pl.BlockSpec((ctx, LANES), lambda b, h: (cb + b, h))],
        out_specs=pl.BlockSpec((ctx, LANES), lambda b, h: (b, h)),
        out_shape=jax.ShapeDtypeStruct((nt - nlat, hh * MLA_V), BF16),
        scratch_shapes=_attn_scratch(ctx, MLA_V, 1),
        compiler_params=_params(2),
        name="mla_attn_ctx",
    )(q, kn, kr, v)
    return o_lat, o_ctx


def _ret_kernel(*refs, reverse):
    if reverse:
        dl_ref, q_ref, k_ref, v0_ref, v1_ref, of_ref, g0_ref, g1_ref, o_ref, st = refs
    else:
        dl_ref, q_ref, k_ref, v0_ref, v1_ref, o_ref, st = refs
    s = pl.program_id(1)

    @pl.when(s == 0)
    def _():
        st[...] = jnp.zeros(st.shape, F32)

    dl = dl_ref[...]
    lsig = jnp.minimum(dl, 0.0) - jnp.log(1.0 + jnp.exp(-jnp.abs(dl)))
    row = lax.broadcasted_iota(jnp.int32, dl.shape, 0)
    col = lax.broadcasted_iota(jnp.int32, dl.shape, 1)
    c = RET_CHUNK
    i2 = lax.broadcasted_iota(jnp.int32, (c, c), 0)
    j2 = lax.broadcasted_iota(jnp.int32, (c, c), 1)
    pos = lax.broadcasted_iota(jnp.int32, (c, 1), 0).astype(F32)
    dist = ((j2 - i2) if reverse else (i2 - j2)).astype(F32)
    keep = dist >= 0.0
    dist = jnp.where(keep, dist, 0.0)
    per_blk = COL_CHUNK // RET_VDIM

    for hd in range(RET_HEADS):
        pick = jnp.logical_and(row == (1 if reverse else 0), col == hd)
        lg = jnp.sum(jnp.sum(jnp.where(pick, lsig, 0.0), axis=1, keepdims=True), axis=0, keepdims=True)
        if reverse:
            q_dec = jnp.exp((c - pos) * lg)
            k_dec = jnp.exp(pos * lg)
        else:
            q_dec = jnp.exp((pos + 1.0) * lg)
            k_dec = jnp.exp((c - 1.0 - pos) * lg)
        intra = jnp.where(keep, jnp.exp(dist * lg), 0.0)
        chunk_dec = jnp.exp(c * lg)

        ksl = slice(hd * RET_KDIM, (hd + 1) * RET_KDIM)
        vsl = slice((hd % per_blk) * RET_VDIM, (hd % per_blk + 1) * RET_VDIM)
        osl = slice(hd * RET_VDIM, (hd + 1) * RET_VDIM)
        qb = q_ref[:, ksl]
        kb = k_ref[:, ksl]
        vb = (v0_ref if hd < per_blk else v1_ref)[:, vsl]
        state = st[hd]
        sc = _dot_nt(qb, kb) * intra
        o = _dot(sc.astype(BF16), vb) + _dot((qb.astype(F32) * q_dec).astype(BF16), state.astype(BF16))
        st[hd] = chunk_dec * state + _dot_tn((kb.astype(F32) * k_dec).astype(BF16), vb)
        if reverse:
            tot = _rms(of_ref[:, osl] + o)
            g = (g0_ref if hd < per_blk else g1_ref)[:, vsl].astype(F32)
            o_ref[:, osl] = (g * (1.0 / (1.0 + jnp.exp(-g))) * tot).astype(BF16)
        else:
            o_ref[:, osl] = o


def _retention(p, decay_logit, geo):
    nt = p.shape[0]
    nb, seq, ctx, nlat = geo["b"], geo["seq"], geo["ctx"], geo["nlat"]
    c = RET_CHUNK
    ncc, ncl = ctx // c, seq // c
    hh = RET_HEADS
    kq, kv = RET_KDIM, RET_VDIM
    wb = COL_CHUNK
    assert hh * kq == wb and hh * kv == 2 * wb
    dl = jnp.zeros((8, LANES), F32).at[:2, :hh].set(decay_logit.astype(F32))

    def rows_fwd(b, s):
        return jnp.where(s < ncc, nlat // c + b * ncc + s, b * ncl + (s - ncc))

    def rows_bwd(b, s):
        return jnp.where(s < ncc, nlat // c + b * ncc + (ncc - 1 - s), b * ncl + (ncl - 1 - (s - ncc)))

    def blk(rows, col):
        return pl.BlockSpec((c, wb), lambda b, s: (rows(b, s), col))

    def specs(rows):
        return [pl.BlockSpec((8, LANES), lambda b, s: (0, 0)), blk(rows, 3), blk(rows, 0), blk(rows, 1), blk(rows, 2)]

    grid = (nb, ncc + ncl)
    scratch = [pltpu.VMEM((hh, kq, kv), F32)]
    o_f = pl.pallas_call(
        functools.partial(_ret_kernel, reverse=False),
        grid=grid,
        in_specs=specs(rows_fwd),
        out_specs=pl.BlockSpec((c, hh * kv), lambda b, s: (rows_fwd(b, s), 0)),
        out_shape=jax.ShapeDtypeStruct((nt, hh * kv), F32),
        scratch_shapes=scratch,
        compiler_params=_params(2),
        name="ret_fwd",
    )(dl, p, p, p, p)
    return pl.pallas_call(
        functools.partial(_ret_kernel, reverse=True),
        grid=grid,
        in_specs=specs(rows_bwd) + [
            pl.BlockSpec((c, hh * kv), lambda b, s: (rows_bwd(b, s), 0)), blk(rows_bwd, 4), blk(rows_bwd, 5)],
        out_specs=pl.BlockSpec((c, hh * kv), lambda b, s: (rows_bwd(b, s), 0)),
        out_shape=jax.ShapeDtypeStruct((nt, hh * kv), BF16),
        scratch_shapes=scratch,
        compiler_params=_params(2),
        name="ret_bwd",
    )(dl, p, p, p, p, o_f, p, p)


def _post_kernel(ol_ref, oc_ref, w_ref, x_ref, mod_ref, g2_ref, wr_ref, br_ref, tri_ref,
                 xo_ref, h2_ref, idx_ref, gate_ref, rank_ref, cnt_ref, cnt_scr, *, nlt):
    t = pl.program_id(0)

    @pl.when(t == 0)
    def _():
        cnt_scr[...] = jnp.zeros(cnt_scr.shape, F32)

    o = jnp.where(t < nlt, ol_ref[...], oc_ref[...])
    y = _dot_split(o, w_ref)
    x = x_ref[...] + mod_ref[2:3, :] * y
    xo_ref[...] = x
    h2 = _norm_mod(x, g2_ref[...], mod_ref[3:4, :], mod_ref[4:5, :])
    _store_rows_as_tiles(h2_ref, h2)
    h_hi = h2.astype(BF16)
    h_lo = (h2 - h_hi.astype(F32)).astype(BF16)
    logits = (_dot(h_hi, wr_ref[0]) + _dot(h_lo, wr_ref[0]) + _dot(h_hi, wr_ref[1])) + br_ref[...]
    lane = lax.broadcasted_iota(jnp.int32, logits.shape, 1).astype(F32)
    work = logits
    vals, idxs = [], []
    for _ in range(TOP_K):
        mx = jnp.max(work, axis=1, keepdims=True)
        ix = jnp.min(jnp.where(work == mx, lane, float(LANES)), axis=1, keepdims=True)
        vals.append(mx)
        idxs.append(ix)
        work = jnp.where(lane == ix, -jnp.inf, work)
    es = [jnp.exp(v - vals[0]) for v in vals]
    den = es[0] + es[1] + es[2] + es[3]
    onehots = [lane == ix for ix in idxs]
    oh = jnp.zeros(logits.shape, F32)
    for m in onehots:
        oh = oh + jnp.where(m, 1.0, 0.0)
    before = _dot(tri_ref[...], oh.astype(BF16)) + cnt_scr[0:1, :]
    idx_out = jnp.zeros(logits.shape, F32)
    gate_out = jnp.zeros(logits.shape, F32)
    rank_out = jnp.zeros(logits.shape, F32)
    for r in range(TOP_K):
        sel = lane == float(r)
        rk = jnp.sum(jnp.where(onehots[r], before, 0.0), axis=1, keepdims=True)
        idx_out = jnp.where(sel, idxs[r], idx_out)
        gate_out = jnp.where(sel, es[r] / den, gate_out)
        rank_out = jnp.where(sel, rk, rank_out)
    idx_ref[...] = idx_out.astype(jnp.int32)
    gate_ref[...] = gate_out
    rank_ref[...] = rank_out.astype(jnp.int32)
    cnt_scr[0:1, :] = cnt_scr[0:1, :] + jnp.sum(oh, axis=0, keepdims=True)
    cnt_ref[...] = cnt_scr[...]


def _post(o_lat, o_ctx, ctx_off, w_out, xs, mod_l, g2, w_router, b_router, geo):
    nt, d = xs.shape
    ko = o_lat.shape[1]
    tm, per, nb, nlt = geo["tm"], geo["per"], geo["b"], geo["nlt"]
    ii = lax.broadcasted_iota(jnp.int32, (tm, tm), 0)
    jj = lax.broadcasted_iota(jnp.int32, (tm, tm), 1)
    tri = (jj < ii).astype(BF16)
    wr32 = jnp.zeros((d, LANES), F32).at[:, :N_EXPERTS].set(w_router.astype(F32))
    wr_hi = wr32.astype(BF16)
    wr = jnp.stack([wr_hi, (wr32 - wr_hi.astype(F32)).astype(BF16)])
    br = jnp.full((1, LANES), NEG_BIG, F32).at[0, :N_EXPERTS].set(b_router.astype(F32))
    row = lambda w: pl.BlockSpec((tm, w), lambda i: (i, 0))
    full = lambda a: pl.BlockSpec(a.shape, lambda i: (0,) * a.ndim)
    return pl.pallas_call(
        functools.partial(_post_kernel, nlt=nlt),
        grid=(nt // tm,),
        in_specs=[pl.BlockSpec((tm, ko), lambda i: (jnp.minimum(i, nlt - 1), 0)),
                  pl.BlockSpec((tm, ko), lambda i: (jnp.maximum(i - nlt, 0) + ctx_off, 0)),
                  full(w_out), row(d),
                  pl.BlockSpec((None, 6, d), lambda i: (jnp.minimum(i // per, nb), 0, 0)),
                  full(g2), full(wr), full(br), full(tri)],
        out_specs=[row(d), pl.BlockSpec((tm * SUBLANES, LANES), lambda i: (i, 0)),
                   row(LANES), row(LANES), row(LANES),
                   pl.BlockSpec((8, LANES), lambda i: (0, 0))],
        out_shape=[jax.ShapeDtypeStruct((nt, d), F32), jax.ShapeDtypeStruct((nt * SUBLANES, LANES), F32),
                   jax.ShapeDtypeStruct((nt, LANES), jnp.int32), jax.ShapeDtypeStruct((nt, LANES), F32),
                   jax.ShapeDtypeStruct((nt, LANES), jnp.int32), jax.ShapeDtypeStruct((8, LANES), F32)],
        scratch_shapes=[pltpu.VMEM((8, LANES), F32)],
        input_output_aliases={3: 0},
        compiler_params=_params(1),
        name="post",
    )(o_lat, o_ctx, w_out, xs, mod_l, g2, wr, br, tri)


SUBLANES = 8
DMA_UNROLL = 4


def _row_copy(src, dst, sem):
    return pltpu.make_async_copy(src, dst, sem)


def _store_rows_as_tiles(ref, val):
    rows = val.shape[0]
    for j in range(SUBLANES):
        ref[pl.ds(j, rows, stride=SUBLANES), :] = val[:, j * LANES:(j + 1) * LANES]


def _load_rows_from_tiles(ref):
    rows = ref.shape[0] // SUBLANES
    return jnp.concatenate([ref[pl.ds(j, rows, stride=SUBLANES), :] for j in range(SUBLANES)], axis=1)


def _tile_of(ref, row):
    if isinstance(row, int):
        return ref.at[pl.ds(row * SUBLANES, SUBLANES)]
    return ref.at[pl.ds(pl.multiple_of(row * SUBLANES, SUBLANES), SUBLANES)]


def _dispatch_kernel(zs_ref, zc_ref, slot_ref, h_ref, xs_ref, zero_scr, sem, *, tm):
    t = pl.program_id(0)

    @pl.when(t == 0)
    def _():
        zero_scr[...] = jnp.zeros(zero_scr.shape, F32)

        def per_expert(e, carry):
            start = zs_ref[e]
            n = zc_ref[e]

            def issue(r, c2):
                _row_copy(zero_scr, _tile_of(xs_ref, start + r), sem).start()
                return c2

            lax.fori_loop(0, n, issue, 0)

            def drain(r, c2):
                _row_copy(zero_scr, _tile_of(xs_ref, 0), sem).wait()
                return c2

            lax.fori_loop(0, n, drain, 0)
            return carry

        lax.fori_loop(0, N_EXPERTS, per_expert, 0)

    def issue(r, carry):
        for k in range(TOP_K):
            s = slot_ref[r * TOP_K + k]
            _row_copy(_tile_of(h_ref, r), _tile_of(xs_ref, s), sem).start(priority=k % 2)
        return carry

    lax.fori_loop(0, tm, issue, 0, unroll=DMA_UNROLL)

    def drain(r, carry):
        for k in range(TOP_K):
            _row_copy(_tile_of(h_ref, 0), _tile_of(xs_ref, 0), sem).wait()
        return carry

    lax.fori_loop(0, tm, drain, 0, unroll=2 * DMA_UNROLL)


def _dispatch(h2, slot_flat, zstart, zcount, n_slots, geo):
    sub, lanes = SUBLANES, h2.shape[1]
    nt = h2.shape[0] // sub
    tm = geo["tm"]
    return pl.pallas_call(
        functools.partial(_dispatch_kernel, tm=tm),
        grid_spec=pltpu.PrefetchScalarGridSpec(
            num_scalar_prefetch=2,
            grid=(nt // tm,),
            in_specs=[pl.BlockSpec((tm * TOP_K,), lambda i, zs, zc: (i,), memory_space=pltpu.SMEM),
                      pl.BlockSpec((tm * sub, lanes), lambda i, zs, zc: (i, 0))],
            out_specs=pl.BlockSpec(memory_space=pl.ANY),
            scratch_shapes=[pltpu.VMEM((sub, lanes), F32), pltpu.SemaphoreType.DMA(())]),
        out_shape=jax.ShapeDtypeStruct((n_slots * sub, lanes), F32),
        compiler_params=_params(1),
        name="moe_dispatch",
    )(zstart, zcount, slot_flat, h2)


def _expert_kernel(be_ref, nu_ref, x_ref, wgu_ref, bgu_ref, wd_ref, bd_ref, y_ref, wgu_bf, wd_bf):
    i = pl.program_id(0)

    @pl.when(jnp.logical_or(i == 0, be_ref[i] != be_ref[jnp.maximum(i - 1, 0)]))
    def _():
        wgu_bf[...] = wgu_ref[...].astype(BF16)
        wd_bf[...] = wd_ref[...].astype(BF16)

    @pl.when(i < nu_ref[0])
    def _():
        x = _load_rows_from_tiles(x_ref).astype(BF16)
        gu = _dot(x, wgu_bf[...]) + bgu_ref[...]
        glu = jnp.minimum(gu[:, :D_EXPERT], SWIGLU_LIMIT)
        lin = jnp.clip(gu[:, D_EXPERT:], -SWIGLU_LIMIT, SWIGLU_LIMIT)
        act = glu * (1.0 / (1.0 + jnp.exp(-SWIGLU_ALPHA * glu))) * (lin + 1.0)
        _store_rows_as_tiles(y_ref, _dot(act.astype(BF16), wd_bf[...]) + bd_ref[...])


def _experts(xs, block_expert, n_used, layer, wgu, bgu, wd, bd):
    sub, lanes = SUBLANES, xs.shape[1]
    ns = xs.shape[0] // sub
    nblk = ns // MOE_BLOCK
    depth, ne, d, f2 = wgu.shape
    f = wd.shape[2]
    blk = lambda i, be, nu: (jnp.minimum(i, nu[0] - 1), 0)
    wsel = lambda i, be, nu: (layer, be[i], 0, 0)
    return pl.pallas_call(
        _expert_kernel,
        grid_spec=pltpu.PrefetchScalarGridSpec(
            num_scalar_prefetch=2,
            grid=(nblk,),
            in_specs=[pl.BlockSpec((MOE_BLOCK * sub, lanes), blk),
                      pl.BlockSpec((None, None, d, f2), wsel),
                      pl.BlockSpec((None, None, 1, f2), wsel),
                      pl.BlockSpec((None, None, f, d), wsel),
                      pl.BlockSpec((None, None, 1, d), wsel)],
            out_specs=pl.BlockSpec((MOE_BLOCK * sub, lanes), blk),
            scratch_shapes=[pltpu.VMEM((d, f2), BF16), pltpu.VMEM((f, d), BF16)]),
        out_shape=jax.ShapeDtypeStruct((ns * sub, lanes), F32),
        compiler_params=_params(1),
        name="moe_experts",
    )(block_expert, n_used, xs, wgu, bgu.reshape(depth, ne, 1, f2), wd, bd.reshape(depth, ne, 1, d))


def _combine_kernel(slot_ref, next_slot_ref, gate_ref, x_ref, mod_ref, y_ref, xo_ref, ybuf, sems, *, tm):
    t = pl.program_id(0)
    cur = t % 2

    def gather(slots, buf):
        def issue(r, carry):
            for k in range(TOP_K):
                s = slots[r * TOP_K + k]
                _row_copy(_tile_of(y_ref, s), _tile_of(ybuf.at[buf, k], r), sems.at[buf]).start(priority=k % 2)
            return carry
        lax.fori_loop(0, tm, issue, 0, unroll=DMA_UNROLL)

    @pl.when(t == 0)
    def _():
        gather(slot_ref, 0)

    @pl.when(t + 1 < pl.num_programs(0))
    def _():
        gather(next_slot_ref, 1 - cur)

    def drain(r, carry):
        for k in range(TOP_K):
            _row_copy(_tile_of(y_ref, 0), _tile_of(ybuf.at[cur, k], 0), sems.at[cur]).wait()
        return carry

    lax.fori_loop(0, tm, drain, 0, unroll=2 * DMA_UNROLL)
    gates = gate_ref[...]
    f = gates[:, 0:1] * _load_rows_from_tiles(ybuf.at[cur, 0])
    for k in range(1, TOP_K):
        f = f + gates[:, k:k + 1] * _load_rows_from_tiles(ybuf.at[cur, k])
    xo_ref[...] = x_ref[...] + mod_ref[5:6, :] * f


def _combine(y, slot_flat, gates, xs, mod_l, geo):
    nt, d = xs.shape
    tm, per, nb = geo["tm"], geo["per"], geo["b"]
    last = nt // tm - 1
    return pl.pallas_call(
        functools.partial(_combine_kernel, tm=tm),
        grid=(nt // tm,),
        in_specs=[pl.BlockSpec((tm * TOP_K,), lambda i: (i,), memory_space=pltpu.SMEM),
                  pl.BlockSpec((tm * TOP_K,), lambda i: (jnp.minimum(i + 1, last),), memory_space=pltpu.SMEM),
                  pl.BlockSpec((tm, LANES), lambda i: (i, 0)),
                  pl.BlockSpec((tm, d), lambda i: (i, 0)),
                  pl.BlockSpec((None, 6, d), lambda i: (jnp.minimum(i // per, nb), 0, 0)),
                  pl.BlockSpec(memory_space=pl.ANY)],
        out_specs=pl.BlockSpec((tm, d), lambda i: (i, 0)),
        out_shape=jax.ShapeDtypeStruct((nt, d), F32),
        scratch_shapes=[pltpu.VMEM((2, TOP_K, tm * SUBLANES, LANES), F32), pltpu.SemaphoreType.DMA((2,))],
        input_output_aliases={3: 0},
        compiler_params=_params(1),
        name="moe_combine",
    )(slot_flat, slot_flat, gates, xs, mod_l, y)


def _moe(h2, top_idx, gates, rank, counts, xs, mod_l, layer, wgu, bgu, wd, bd, geo):
    nt = xs.shape[0]
    n_assign = nt * TOP_K
    nblk = n_assign // MOE_BLOCK + N_EXPERTS
    n_slots = nblk * MOE_BLOCK
    counts = counts.astype(jnp.int32)
    padded = (counts + MOE_BLOCK - 1) // MOE_BLOCK * MOE_BLOCK
    pad_end = jnp.cumsum(padded)
    pad_start = pad_end - padded
    experts = jnp.arange(N_EXPERTS, dtype=jnp.int32)
    idx4 = top_idx[:, :TOP_K]
    base = jnp.sum(jnp.where(idx4[:, :, None] == experts[None, None, :], pad_start[None, None, :], 0), axis=-1)
    slot_flat = (base + rank[:, :TOP_K]).reshape(n_assign).astype(jnp.int32)
    n_used = (pad_end[-1] // MOE_BLOCK).astype(jnp.int32)
    blk_id = jnp.minimum(jnp.arange(nblk, dtype=jnp.int32), n_used - 1)
    block_expert = jnp.minimum(
        jnp.sum((pad_end[None, :] <= (blk_id * MOE_BLOCK)[:, None]).astype(jnp.int32), axis=1), N_EXPERTS - 1)
    xs_slots = _dispatch(h2, slot_flat, (pad_start + counts).astype(jnp.int32),
                         (padded - counts).astype(jnp.int32), n_slots, geo)
    y = _experts(xs_slots, block_expert.astype(jnp.int32), n_used.reshape(1), layer, wgu, bgu, wd, bd)
    return _combine(y, slot_flat, gates, xs, mod_l, geo)


def _final_kernel(x_ref, g_ref, o_ref):
    o_ref[...] = _rms(x_ref[...]) * g_ref[...]


def _final_norm(xs, g, nlat, tm):
    d = xs.shape[1]
    return pl.pallas_call(
        _final_kernel,
        grid=(nlat // tm,),
        in_specs=[pl.BlockSpec((tm, d), lambda i: (i, 0)), pl.BlockSpec((1, d), lambda i: (0, 0))],
        out_specs=pl.BlockSpec((tm, d), lambda i: (i, 0)),
        out_shape=jax.ShapeDtypeStruct((nlat, d), F32),
        compiler_params=_params(1),
        name="final_norm",
    )(xs, g)


def _with_identity_rows(cos, sin, tm):
    ones = jnp.ones((tm, cos.shape[1]), F32)
    return jnp.concatenate([cos, ones], axis=0), jnp.concatenate([sin, 0.0 * ones], axis=0)


def _axial_tables(seq, tm, pad_to_lanes):
    pos = jnp.arange(seq)
    rows = (pos // GRID_W).astype(F32)
    cols = (pos % GRID_W).astype(F32)
    half = DA_HEAD_DIM // 2
    inv = ROPE_BASE ** (-jnp.arange(0, half, 2, dtype=F32) / half)
    ar = rows[:, None] * inv[None, :]
    ac = cols[:, None] * inv[None, :]
    cos64 = jnp.concatenate([jnp.cos(ar), jnp.cos(ar), jnp.cos(ac), jnp.cos(ac)], axis=1)
    sin64 = jnp.concatenate([-jnp.sin(ar), jnp.sin(ar), -jnp.sin(ac), jnp.sin(ac)], axis=1)
    if pad_to_lanes:
        cos = jnp.concatenate([cos64, jnp.ones_like(cos64)], axis=1)
        sin = jnp.concatenate([sin64, jnp.zeros_like(sin64)], axis=1)
    else:
        cos = jnp.concatenate([cos64, cos64], axis=1)
        sin = jnp.concatenate([sin64, sin64], axis=1)
    return _with_identity_rows(cos, sin, tm)


def _ret_tables(seq, tm):
    inv = ROPE_BASE ** (-jnp.arange(0, RET_KDIM, 2, dtype=F32) / RET_KDIM)
    ang = jnp.arange(seq).astype(F32)[:, None] * inv[None, :]
    return _with_identity_rows(jnp.cos(ang), jnp.sin(ang), tm)


def _mla_weights(w_in, w_q_up, w_kv_up):
    d = w_in.shape[0]
    kvw = MLA_KV_RANK + MLA_ROPE
    w_in_r = jnp.concatenate([w_in[:, :MLA_KV_RANK], w_in[:, kvw:], w_in[:, MLA_KV_RANK:kvw],
                              jnp.zeros((d, LANES - MLA_ROPE), w_in.dtype)], axis=1)
    wq = w_q_up.reshape(MLA_Q_RANK, MLA_HEADS, MLA_NOPE + MLA_ROPE)
    wq = jnp.concatenate([wq, jnp.zeros((MLA_Q_RANK, MLA_HEADS, LANES - MLA_ROPE), wq.dtype)], axis=2)
    wq = wq.reshape(MLA_Q_RANK, MLA_HEADS * 2 * LANES)
    wkv = w_kv_up.reshape(MLA_KV_RANK, MLA_HEADS, MLA_NOPE + MLA_V)
    wkv = jnp.concatenate([wkv[:, :, :MLA_NOPE].reshape(MLA_KV_RANK, -1),
                           wkv[:, :, MLA_NOPE:].reshape(MLA_KV_RANK, -1)], axis=1)
    return w_in_r.astype(BF16), wq.astype(BF16), wkv.astype(BF16)


def kernel(x, c, ctx, c_ctx, ada_w, ada_b, norm_g, final_g, da_w_in, da_w_out, da_lambda, da_subln_g,
           ret_w_in, ret_decay_logit, ret_w_out, mla_w_in, mla_q_norm_g, mla_w_q_up, mla_kv_norm_g,
           mla_w_kv_up, mla_w_out, moe_w_router, moe_b_router, moe_w_gate_up, moe_b_gate_up,
           moe_w_down, moe_b_down):
    nb, seq, d = x.shape
    nctx_per = ctx.shape[1]
    depth = ada_w.shape[0]
    nlat = nb * seq
    tm = _pick_tile(512, seq, nb * nctx_per)
    geo = dict(b=nb, seq=seq, ctx=nctx_per, nlat=nlat, tm=tm, per=seq // tm, nlt=nlat // tm)
    assert nb < MOD_ROWS and nlat % nctx_per == 0 and nctx_per % RET_CHUNK == 0

    xs = jnp.concatenate([x.reshape(nlat, d), ctx.reshape(nb * nctx_per, d)], axis=0).astype(F32)
    cond = jnp.zeros((MOD_ROWS, d), F32).at[:nb].set(c).at[nb].set(c_ctx)
    mod = _adaln(cond, ada_w, ada_b).reshape(depth, MOD_ROWS, 6, d)

    cos_a, sin_a = _axial_tables(seq, tm, pad_to_lanes=False)
    cos_m, sin_m = _axial_tables(seq, tm, pad_to_lanes=True)
    cos_r, sin_r = _ret_tables(seq, tm)

    for i in range(depth):
        kind = i % N_MIXERS
        j = i // N_MIXERS
        mod_l = mod[i]
        g1 = norm_g[i, 0].reshape(1, d)
        g2 = norm_g[i, 1].reshape(1, d)
        if kind == 0:
            lam_init = 0.8 - 0.6 * math.exp(-0.3 * i)
            p = _proj(xs, g1, mod_l, da_w_in[j].astype(BF16), cos_a, sin_a,
                      ["rope16", "plain", "rope16"], geo)
            o_lat, o_ctx = _da_attention(p, da_lambda[j].astype(F32),
                                         da_subln_g[j].reshape(1, DA_V_DIM).astype(F32), lam_init, geo)
            ctx_off = 0
            w_out = da_w_out[j]
        elif kind == 1:
            p = _proj(xs, g1, mod_l, ret_w_in[j].astype(BF16), cos_r, sin_r,
                      ["ret_k", "plain", "plain", "ret_q", "plain", "plain"], geo,
                      ret_scale=RET_KDIM ** -0.5)
            o_lat = o_ctx = _retention(p, ret_decay_logit[j], geo)
            ctx_off = geo["nlt"]
            w_out = ret_w_out[j]
        else:
            w_in_r, wq, wkv = _mla_weights(mla_w_in[j], mla_w_q_up[j], mla_w_kv_up[j])
            q, kn, v, kr = _mla_proj(xs, g1, mod_l, w_in_r, mla_q_norm_g[j].reshape(1, -1).astype(F32),
                                     mla_kv_norm_g[j].reshape(1, -1).astype(F32), wq, wkv, cos_m, sin_m, geo)
            o_lat, o_ctx = _mla_attention(q, kn, v, kr, geo)
            ctx_off = 0
            w_out = mla_w_out[j]
        xs, h2, top_idx, gates, rank, counts = _post(o_lat, o_ctx, ctx_off, w_out.astype(BF16), xs, mod_l, g2,
                                                     moe_w_router[i], moe_b_router[i], geo)
        xs = _moe(h2, top_idx, gates, rank, counts[0, :N_EXPERTS], xs, mod_l, i,
                  moe_w_gate_up, moe_b_gate_up, moe_w_down, moe_b_down, geo)
    return _final_norm(xs, final_g.reshape(1, d).astype(F32), nlat, tm).reshape(nb, seq, d)
```

```python
import functools
import math

import jax
import jax.numpy as jnp
from jax import lax
from jax.experimental import pallas as pl
from jax.experimental.pallas import tpu as pltpu

F32 = jnp.float32
BF16 = jnp.bfloat16

D_MODEL = 1024
GRID_W = 64
RMS_EPS = 1e-6
ROPE_BASE = 10000.0
N_MIXERS = 3

DA_HEADS = 8
DA_HEAD_DIM = 64
DA_V_DIM = 128

RET_HEADS = 4
RET_KDIM = 256
RET_VDIM = 512
RET_CHUNK = 128

MLA_HEADS = 8
MLA_NOPE = 128
MLA_ROPE = 64
MLA_V = 128
MLA_Q_RANK = 256
MLA_KV_RANK = 128

N_EXPERTS = 32
TOP_K = 4
D_EXPERT = 1024
SWIGLU_LIMIT = 7.0
SWIGLU_ALPHA = 1.702

LANES = 128
MOD_ROWS = 16
MOE_BLOCK = 512
COL_CHUNK = 1024
ATTN_TQ = 1024
ATTN_TK = 1024
ATTN_ROW_SPLIT = 8
VMEM_LIMIT = 56 * 1024 * 1024
NEG_BIG = -1e30


def _params(n_axes):
    return pltpu.CompilerParams(dimension_semantics=("arbitrary",) * n_axes,
                                vmem_limit_bytes=VMEM_LIMIT)


def _pick_tile(cap, *dims):
    t = cap
    while any(d % t for d in dims):
        t //= 2
    return t


def _rms(x):
    return x * lax.rsqrt(jnp.mean(x * x, axis=-1, keepdims=True) + RMS_EPS)


def _norm_mod(x, g, shift, scale):
    return (_rms(x) * g) * (1.0 + scale) + shift


def _dot(a, b):
    return jnp.dot(a, b, preferred_element_type=F32)


def _dot_split(a, b_ref):
    half = b_ref.shape[1] // 2
    return jnp.concatenate([_dot(a, b_ref[:, :half]), _dot(a, b_ref[:, half:])], axis=1)


def _dot_nt(a, b):
    return lax.dot_general(a, b, (((1,), (1,)), ((), ())), preferred_element_type=F32)


def _dot_tn(a, b):
    return lax.dot_general(a, b, (((0,), (0,)), ((), ())), preferred_element_type=F32)


def _adaln_kernel(c_ref, w_ref, b_ref, o_ref):
    c = c_ref[...]
    a = c * (1.0 / (1.0 + jnp.exp(-c)))
    o_ref[...] = jnp.dot(a, w_ref[...], precision=lax.Precision.HIGHEST,
                         preferred_element_type=F32) + b_ref[...]


def _adaln(cond, ada_w, ada_b):
    depth, d, n = ada_w.shape
    tn = COL_CHUNK
    return pl.pallas_call(
        _adaln_kernel,
        grid=(depth, n // tn),
        in_specs=[pl.BlockSpec((MOD_ROWS, d), lambda l, j: (0, 0)),
                  pl.BlockSpec((None, d, tn), lambda l, j: (l, 0, j)),
                  pl.BlockSpec((None, 1, tn), lambda l, j: (l, 0, j))],
        out_specs=pl.BlockSpec((None, MOD_ROWS, tn), lambda l, j: (l, 0, j)),
        out_shape=jax.ShapeDtypeStruct((depth, MOD_ROWS, n), F32),
        compiler_params=_params(2),
        name="adaln",
    )(cond, ada_w, ada_b.reshape(depth, 1, n))


def _rope_pairs16(x, cos, sin):
    lane = lax.broadcasted_iota(jnp.int32, x.shape, 1)
    first = (lane % 32) < 16
    partner = jnp.where(first, pltpu.roll(x, LANES - 16, 1), pltpu.roll(x, 16, 1))
    return x * cos + partner * sin


def _proj_kernel(x_ref, g_ref, mod_ref, w_ref, cos_ref, sin_ref, o_ref, *, kinds, ret_scale):
    h = _norm_mod(x_ref[...], g_ref[...], mod_ref[0:1, :], mod_ref[1:2, :]).astype(BF16)
    cos = cos_ref[...]
    sin = sin_ref[...]
    half = COL_CHUNK // 2
    for c, kind in enumerate(kinds):
        base = c * COL_CHUNK
        a = jnp.concatenate([_dot(h, w_ref[:, base:base + half]),
                             _dot(h, w_ref[:, base + half:base + COL_CHUNK])], axis=1)
        if kind == "plain":
            o_ref[:, base:base + COL_CHUNK] = a.astype(BF16)
        elif kind == "rope16":
            for g in range(COL_CHUNK // LANES):
                sl = slice(g * LANES, (g + 1) * LANES)
                o_ref[:, base + g * LANES:base + (g + 1) * LANES] = _rope_pairs16(a[:, sl], cos, sin).astype(BF16)
        else:
            scale = ret_scale if kind == "ret_k" else 1.0
            for hh in range(COL_CHUNK // RET_KDIM):
                lo = hh * RET_KDIM
                x1 = a[:, lo:lo + LANES]
                x2 = a[:, lo + LANES:lo + RET_KDIM]
                o_ref[:, base + lo:base + lo + LANES] = ((x1 * cos - x2 * sin) * scale).astype(BF16)
                o_ref[:, base + lo + LANES:base + lo + RET_KDIM] = ((x1 * sin + x2 * cos) * scale).astype(BF16)


def _proj(xs, g, mod_l, w, cos_t, sin_t, kinds, geo, ret_scale=1.0):
    nt, d = xs.shape
    n = w.shape[1]
    tm, per, nlt, nb = geo["tm"], geo["per"], geo["nlt"], geo["b"]
    assert n == COL_CHUNK * len(kinds)
    tab = pl.BlockSpec((tm, LANES), lambda i: (jnp.where(i < nlt, i % per, per), 0))
    return pl.pallas_call(
        functools.partial(_proj_kernel, kinds=tuple(kinds), ret_scale=ret_scale),
        grid=(nt // tm,),
        in_specs=[pl.BlockSpec((tm, d), lambda i: (i, 0)),
                  pl.BlockSpec((1, d), lambda i: (0, 0)),
                  pl.BlockSpec((None, 6, d), lambda i: (jnp.minimum(i // per, nb), 0, 0)),
                  pl.BlockSpec((d, n), lambda i: (0, 0)),
                  tab, tab],
        out_specs=pl.BlockSpec((tm, n), lambda i: (i, 0)),
        out_shape=jax.ShapeDtypeStruct((nt, n), BF16),
        compiler_params=_params(1),
        name="proj",
    )(xs, g, mod_l, w, cos_t, sin_t)


def _mla_proj_kernel(x_ref, g_ref, mod_ref, win_ref, qg_ref, kvg_ref, wq_ref, wkv_ref, cos_ref, sin_ref,
                     q_ref, kn_ref, v_ref, kr_ref):
    h = _norm_mod(x_ref[...], g_ref[...], mod_ref[0:1, :], mod_ref[1:2, :]).astype(BF16)
    p = _dot(h, win_ref[...])
    cos = cos_ref[...]
    sin = sin_ref[...]
    ckv = (_rms(p[:, :MLA_KV_RANK]) * kvg_ref[...]).astype(BF16)
    kv = _dot(ckv, wkv_ref[...])
    nk = MLA_HEADS * MLA_NOPE
    kn_ref[...] = kv[:, :nk].astype(BF16)
    v_ref[...] = kv[:, nk:].astype(BF16)
    kr_ref[...] = _rope_pairs16(p[:, MLA_KV_RANK + MLA_Q_RANK:], cos, sin).astype(BF16)
    cq = (_rms(p[:, MLA_KV_RANK:MLA_KV_RANK + MLA_Q_RANK]) * qg_ref[...]).astype(BF16)
    q = _dot(cq, wq_ref[...])
    for hh in range(MLA_HEADS):
        lo = slice(hh * 2 * LANES, hh * 2 * LANES + LANES)
        hi = slice(hh * 2 * LANES + LANES, (hh + 1) * 2 * LANES)
        q_ref[:, lo] = q[:, lo].astype(BF16)
        q_ref[:, hi] = _rope_pairs16(q[:, hi], cos, sin).astype(BF16)


def _mla_proj(xs, g, mod_l, w_in, qg, kvg, wq, wkv, cos_t, sin_t, geo):
    nt, d = xs.shape
    tm, per, nlt, nb = geo["tm"], geo["per"], geo["nlt"], geo["b"]
    full = lambda a: pl.BlockSpec(a.shape, lambda i: (0,) * a.ndim)
    tab = pl.BlockSpec((tm, LANES), lambda i: (jnp.where(i < nlt, i % per, per), 0))
    nq = MLA_HEADS * 2 * LANES
    nk = MLA_HEADS * MLA_NOPE
    return pl.pallas_call(
        _mla_proj_kernel,
        grid=(nt // tm,),
        in_specs=[pl.BlockSpec((tm, d), lambda i: (i, 0)), full(g),
                  pl.BlockSpec((None, 6, d), lambda i: (jnp.minimum(i // per, nb), 0, 0)),
                  full(w_in), full(qg), full(kvg), full(wq), full(wkv), tab, tab],
        out_specs=[pl.BlockSpec((tm, nq), lambda i: (i, 0)),
                   pl.BlockSpec((tm, nk), lambda i: (i, 0)),
                   pl.BlockSpec((tm, nk), lambda i: (i, 0)),
                   pl.BlockSpec((tm, LANES), lambda i: (i, 0))],
        out_shape=[jax.ShapeDtypeStruct((nt, nq), BF16), jax.ShapeDtypeStruct((nt, nk), BF16),
                   jax.ShapeDtypeStruct((nt, nk), BF16), jax.ShapeDtypeStruct((nt, LANES), BF16)],
        compiler_params=_params(1),
        name="mla_proj",
    )(xs, g, mod_l, w_in, qg, kvg, wq, wkv, cos_t, sin_t)


LOG2E = 1.4426950408889634


def _flash_init(m_ref, a_ref):
    m_ref[...] = jnp.full(m_ref.shape, -jnp.inf, F32)
    a_ref[...] = jnp.zeros(a_ref.shape, F32)


def _ones_column(rows):
    lane = lax.broadcasted_iota(jnp.int32, (rows, LANES), 1)
    return jnp.where(lane == 0, 1.0, 0.0).astype(BF16)


def _chunk_rows(j, tk):
    if isinstance(j, int):
        return pl.ds(j * tk, tk)
    return pl.ds(pl.multiple_of(j * tk, tk), tk)


def _softmax_pv(parts, stats):
    for mp, (m_ref, a_ref) in enumerate(stats):
        tq = m_ref.shape[0]
        rb = tq // ATTN_ROW_SPLIT if tq % ATTN_ROW_SPLIT == 0 else tq
        for r0 in range(0, tq, rb):
            rows = slice(r0, r0 + rb)
            tiles = [scores[mp][rows, :] for scores, _ in parts]
            m_prev = m_ref[rows, :]
            m_new = m_prev
            for s2 in tiles:
                m_new = jnp.maximum(m_new, jnp.max(s2, axis=1, keepdims=True))
            acc = jnp.exp2(m_prev - m_new) * a_ref[rows, :]
            for s2, (_, v_aug) in zip(tiles, parts):
                acc = acc + _dot(jnp.exp2(s2 - m_new).astype(BF16), v_aug)
            a_ref[rows, :] = acc
            m_ref[rows, :] = m_new


def _flash_lat(qk_fns, stats, k_ctx, v_ctx, k_at, v_at, s_a, s_b, s_c, n_chunks):
    def qk_into(dst, k):
        for mp, f in enumerate(qk_fns):
            dst[mp] = f(k)

    qk_into(s_c, k_ctx)
    qk_into(s_a, k_at(0))
    qk_into(s_b, k_at(1))
    _softmax_pv([(s_c, v_ctx), (s_a, v_at(0))], stats)

    def body(t, carry):
        j = 2 * t + 1
        qk_into(s_a, k_at(j + 1))
        _softmax_pv([(s_b, v_at(j))], stats)
        qk_into(s_b, k_at(j + 2))
        _softmax_pv([(s_a, v_at(j + 1))], stats)
        return carry

    lax.fori_loop(0, (n_chunks - 2) // 2, body, 0)
    _softmax_pv([(s_b, v_at(n_chunks - 1))], stats)


def _flash_result(a_ref, vdim):
    a = a_ref[...]
    return a[:, :vdim] / a[:, vdim:vdim + 1]


def _da_attn_kernel(*refs, n_chunks, tk, lam_init, scale):
    if n_chunks:
        (lam_ref, sg_ref, q_ref, kc_ref, vc_ref, kl_ref, vl_ref, o_ref, m0, a0, m1, a1, s_a, s_b, s_c) = refs
    else:
        (lam_ref, sg_ref, q_ref, kc_ref, vc_ref, o_ref, m0, a0, m1, a1) = refs
    q = q_ref[...]
    lane = lax.broadcasted_iota(jnp.int32, q.shape, 1)
    zero = jnp.zeros_like(q)
    q_lo = jnp.where(lane < DA_HEAD_DIM, q, zero)
    q_hi = jnp.where(lane >= DA_HEAD_DIM, q, zero)
    _flash_init(m0, a0)
    _flash_init(m1, a1)
    c2 = scale * LOG2E
    stats = [(m0, a0), (m1, a1)]
    qk_fns = [lambda k: _dot_nt(q_lo, k) * c2, lambda k: _dot_nt(q_hi, k) * c2]
    v_ctx = jnp.concatenate([vc_ref[...], _ones_column(vc_ref.shape[0])], axis=1)
    if n_chunks:
        ones = _ones_column(tk)

        def k_at(j):
            return kl_ref[_chunk_rows(j, tk), :]

        def v_at(j):
            return jnp.concatenate([vl_ref[_chunk_rows(j, tk), :], ones], axis=1)

        _flash_lat(qk_fns, stats, kc_ref[...], v_ctx, k_at, v_at, s_a, s_b, s_c, n_chunks)
    else:
        _softmax_pv([([f(kc_ref[...]) for f in qk_fns], v_ctx)], stats)

    lf = lam_ref[...]
    lam = (jnp.exp(jnp.sum(lf[0:1] * lf[1:2], axis=1, keepdims=True))
           - jnp.exp(jnp.sum(lf[2:3] * lf[3:4], axis=1, keepdims=True)) + lam_init)
    o = _flash_result(a0, DA_V_DIM) - lam * _flash_result(a1, DA_V_DIM)
    o = _rms(o) * sg_ref[...] * (1.0 - lam_init)
    o_ref[...] = o.astype(BF16)


def _attn_scratch(tq, vdim, n_maps, tk=0, ctx=0):
    s = []
    for _ in range(n_maps):
        s += [pltpu.VMEM((tq, 1), F32), pltpu.VMEM((tq, vdim + LANES), F32)]
    if tk:
        s += [pltpu.VMEM((n_maps, tq, tk), F32), pltpu.VMEM((n_maps, tq, tk), F32),
              pltpu.VMEM((n_maps, tq, ctx), F32)]
    return s


def _da_attention(p, lam_vecs, subg, lam_init, geo):
    nt = p.shape[0]
    nb, seq, ctx, nlat = geo["b"], geo["seq"], geo["ctx"], geo["nlat"]
    hh = DA_HEADS
    tq = _pick_tile(ATTN_TQ, seq)
    tk = _pick_tile(min(ATTN_TK, seq // 2), seq)
    assert (seq // tk) % 2 == 0
    nqt = seq // tq
    cb = nlat // ctx
    scale = DA_HEAD_DIM ** -0.5
    small = [pl.BlockSpec(lam_vecs.shape, lambda *a: (0, 0)), pl.BlockSpec(subg.shape, lambda *a: (0, 0))]
    o_lat = pl.pallas_call(
        functools.partial(_da_attn_kernel, n_chunks=seq // tk, tk=tk, lam_init=lam_init, scale=scale),
        grid=(nb, hh, nqt),
        in_specs=small + [
            pl.BlockSpec((tq, LANES), lambda b, h, i: (b * nqt + i, 2 * hh + h)),
            pl.BlockSpec((ctx, LANES), lambda b, h, i: (cb + b, h)),
            pl.BlockSpec((ctx, LANES), lambda b, h, i: (cb + b, hh + h)),
            pl.BlockSpec((seq, LANES), lambda b, h, i: (b, h)),
            pl.BlockSpec((seq, LANES), lambda b, h, i: (b, hh + h))],
        out_specs=pl.BlockSpec((tq, LANES), lambda b, h, i: (b * nqt + i, h)),
        out_shape=jax.ShapeDtypeStruct((nlat, hh * DA_V_DIM), BF16),
        scratch_shapes=_attn_scratch(tq, DA_V_DIM, 2, tk, ctx),
        compiler_params=_params(3),
        name="da_attn_lat",
    )(lam_vecs, subg, p, p, p, p, p)
    o_ctx = pl.pallas_call(
        functools.partial(_da_attn_kernel, n_chunks=0, tk=tk, lam_init=lam_init, scale=scale),
        grid=(nb, hh),
        in_specs=small + [
            pl.BlockSpec((ctx, LANES), lambda b, h: (cb + b, 2 * hh + h)),
            pl.BlockSpec((ctx, LANES), lambda b, h: (cb + b, h)),
            pl.BlockSpec((ctx, LANES), lambda b, h: (cb + b, hh + h))],
        out_specs=pl.BlockSpec((ctx, LANES), lambda b, h: (b, h)),
        out_shape=jax.ShapeDtypeStruct((nt - nlat, hh * DA_V_DIM), BF16),
        scratch_shapes=_attn_scratch(ctx, DA_V_DIM, 2),
        compiler_params=_params(2),
        name="da_attn_ctx",
    )(lam_vecs, subg, p, p, p)
    return o_lat, o_ctx


def _mla_attn_kernel(*refs, n_chunks, tk, scale):
    if n_chunks:
        (q_ref, knc_ref, krc_ref, vc_ref, knl_ref, krl_ref, vl_ref, o_ref, m0, a0, s_a, s_b, s_c) = refs
    else:
        (q_ref, knc_ref, krc_ref, vc_ref, o_ref, m0, a0) = refs
    q = q_ref[...]
    _flash_init(m0, a0)
    c2 = scale * LOG2E
    stats = [(m0, a0)]
    qk_fns = [lambda k: _dot_nt(q, k) * c2]
    k_ctx = jnp.concatenate([knc_ref[...], krc_ref[...]], axis=1)
    v_ctx = jnp.concatenate([vc_ref[...], _ones_column(vc_ref.shape[0])], axis=1)
    if n_chunks:
        ones = _ones_column(tk)

        def k_at(j):
            rows = _chunk_rows(j, tk)
            return jnp.concatenate([knl_ref[rows, :], krl_ref[rows, :]], axis=1)

        def v_at(j):
            return jnp.concatenate([vl_ref[_chunk_rows(j, tk), :], ones], axis=1)

        _flash_lat(qk_fns, stats, k_ctx, v_ctx, k_at, v_at, s_a, s_b, s_c, n_chunks)
    else:
        _softmax_pv([([qk_fns[0](k_ctx)], v_ctx)], stats)
    o_ref[...] = _flash_result(a0, MLA_V).astype(BF16)


def _mla_attention(q, kn, v, kr, geo):
    nt = q.shape[0]
    nb, seq, ctx, nlat = geo["b"], geo["seq"], geo["ctx"], geo["nlat"]
    hh = MLA_HEADS
    tq = _pick_tile(ATTN_TQ, seq)
    tk = _pick_tile(min(ATTN_TK, seq // 2), seq)
    assert (seq // tk) % 2 == 0
    nqt = seq // tq
    cb = nlat // ctx
    scale = (MLA_NOPE + MLA_ROPE) ** -0.5
    o_lat = pl.pallas_call(
        functools.partial(_mla_attn_kernel, n_chunks=seq // tk, tk=tk, scale=scale),
        grid=(nb, hh, nqt),
        in_specs=[
            pl.BlockSpec((tq, 2 * LANES), lambda b, h, i: (b * nqt + i, h)),
            pl.BlockSpec((ctx, LANES), lambda b, h, i: (cb + b, h)),
            pl.BlockSpec((ctx, LANES), lambda b, h, i: (cb + b, 0)),
            pl.BlockSpec((ctx, LANES), lambda b, h, i: (cb + b, h)),
            pl.BlockSpec((seq, LANES), lambda b, h, i: (b, h)),
            pl.BlockSpec((seq, LANES), lambda b, h, i: (b, 0)),
            pl.BlockSpec((seq, LANES), lambda b, h, i: (b, h))],
        out_specs=pl.BlockSpec((tq, LANES), lambda b, h, i: (b * nqt + i, h)),
        out_shape=jax.ShapeDtypeStruct((nlat, hh * MLA_V), BF16),
        scratch_shapes=_attn_scratch(tq, MLA_V, 1, tk, ctx),
        compiler_params=_params(3),
        name="mla_attn_lat",
    )(q, kn, kr, v, kn, kr, v)
    o_ctx = pl.pallas_call(
        functools.partial(_mla_attn_kernel, n_chunks=0, tk=tk, scale=scale),
        grid=(nb, hh),
        in_specs=[
            pl.BlockSpec((ctx, 2 * LANES), lambda b, h: (cb + b, h)),
            pl.BlockSpec((ctx, LANES), lambda b, h: (cb + b, h)),
            pl.BlockSpec((ctx, LANES), lambda b, h: (cb + b, 0)),
            pl.BlockSpec((ctx, LANES), lambda b, h: (cb + b, h))],
        out_specs=pl.BlockSpec((ctx, LANES), lambda b, h: (b, h)),
        out_shape=jax.ShapeDtypeStruct((nt - nlat, hh * MLA_V), BF16),
        scratch_shapes=_attn_scratch(ctx, MLA_V, 1),
        compiler_params=_params(2),
        name="mla_attn_ctx",
    )(q, kn, kr, v)
    return o_lat, o_ctx


def _ret_kernel(*refs, reverse):
    if reverse:
        dl_ref, q_ref, k_ref, v0_ref, v1_ref, of_ref, g0_ref, g1_ref, o_ref, st = refs
    else:
        dl_ref, q_ref, k_ref, v0_ref, v1_ref, o_ref, st = refs
    s = pl.program_id(1)

    @pl.when(s == 0)
    def _():
        st[...] = jnp.zeros(st.shape, F32)

    dl = dl_ref[...]
    lsig = jnp.minimum(dl, 0.0) - jnp.log(1.0 + jnp.exp(-jnp.abs(dl)))
    row = lax.broadcasted_iota(jnp.int32, dl.shape, 0)
    col = lax.broadcasted_iota(jnp.int32, dl.shape, 1)
    c = RET_CHUNK
    i2 = lax.broadcasted_iota(jnp.int32, (c, c), 0)
    j2 = lax.broadcasted_iota(jnp.int32, (c, c), 1)
    pos = lax.broadcasted_iota(jnp.int32, (c, 1), 0).astype(F32)
    dist = ((j2 - i2) if reverse else (i2 - j2)).astype(F32)
    keep = dist >= 0.0
    dist = jnp.where(keep, dist, 0.0)
    per_blk = COL_CHUNK // RET_VDIM

    for hd in range(RET_HEADS):
        pick = jnp.logical_and(row == (1 if reverse else 0), col == hd)
        lg = jnp.sum(jnp.sum(jnp.where(pick, lsig, 0.0), axis=1, keepdims=True), axis=0, keepdims=True)
        if reverse:
            q_dec = jnp.exp((c - pos) * lg)
            k_dec = jnp.exp(pos * lg)
        else:
            q_dec = jnp.exp((pos + 1.0) * lg)
            k_dec = jnp.exp((c - 1.0 - pos) * lg)
        intra = jnp.where(keep, jnp.exp(dist * lg), 0.0)
        chunk_dec = jnp.exp(c * lg)

        ksl = slice(hd * RET_KDIM, (hd + 1) * RET_KDIM)
        vsl = slice((hd % per_blk) * RET_VDIM, (hd % per_blk + 1) * RET_VDIM)
        osl = slice(hd * RET_VDIM, (hd + 1) * RET_VDIM)
        qb = q_ref[:, ksl]
        kb = k_ref[:, ksl]
        vb = (v0_ref if hd < per_blk else v1_ref)[:, vsl]
        state = st[hd]
        sc = _dot_nt(qb, kb) * intra
        o = _dot(sc.astype(BF16), vb) + _dot((qb.astype(F32) * q_dec).astype(BF16), state.astype(BF16))
        st[hd] = chunk_dec * state + _dot_tn((kb.astype(F32) * k_dec).astype(BF16), vb)
        if reverse:
            tot = _rms(of_ref[:, osl] + o)
            g = (g0_ref if hd < per_blk else g1_ref)[:, vsl].astype(F32)
            o_ref[:, osl] = (g * (1.0 / (1.0 + jnp.exp(-g))) * tot).astype(BF16)
        else:
            o_ref[:, osl] = o


def _retention(p, decay_logit, geo):
    nt = p.shape[0]
    nb, seq, ctx, nlat = geo["b"], geo["seq"], geo["ctx"], geo["nlat"]
    c = RET_CHUNK
    ncc, ncl = ctx // c, seq // c
    hh = RET_HEADS
    kq, kv = RET_KDIM, RET_VDIM
    wb = COL_CHUNK
    assert hh * kq == wb and hh * kv == 2 * wb
    dl = jnp.zeros((8, LANES), F32).at[:2, :hh].set(decay_logit.astype(F32))

    def rows_fwd(b, s):
        return jnp.where(s < ncc, nlat // c + b * ncc + s, b * ncl + (s - ncc))

    def rows_bwd(b, s):
        return jnp.where(s < ncc, nlat // c + b * ncc + (ncc - 1 - s), b * ncl + (ncl - 1 - (s - ncc)))

    def blk(rows, col):
        return pl.BlockSpec((c, wb), lambda b, s: (rows(b, s), col))

    def specs(rows):
        return [pl.BlockSpec((8, LANES), lambda b, s: (0, 0)), blk(rows, 3), blk(rows, 0), blk(rows, 1), blk(rows, 2)]

    grid = (nb, ncc + ncl)
    scratch = [pltpu.VMEM((hh, kq, kv), F32)]
    o_f = pl.pallas_call(
        functools.partial(_ret_kernel, reverse=False),
        grid=grid,
        in_specs=specs(rows_fwd),
        out_specs=pl.BlockSpec((c, hh * kv), lambda b, s: (rows_fwd(b, s), 0)),
        out_shape=jax.ShapeDtypeStruct((nt, hh * kv), F32),
        scratch_shapes=scratch,
        compiler_params=_params(2),
        name="ret_fwd",
    )(dl, p, p, p, p)
    return pl.pallas_call(
        functools.partial(_ret_kernel, reverse=True),
        grid=grid,
        in_specs=specs(rows_bwd) + [
            pl.BlockSpec((c, hh * kv), lambda b, s: (rows_bwd(b, s), 0)), blk(rows_bwd, 4), blk(rows_bwd, 5)],
        out_specs=pl.BlockSpec((c, hh * kv), lambda b, s: (rows_bwd(b, s), 0)),
        out_shape=jax.ShapeDtypeStruct((nt, hh * kv), BF16),
        scratch_shapes=scratch,
        compiler_params=_params(2),
        name="ret_bwd",
    )(dl, p, p, p, p, o_f, p, p)


def _post_kernel(ol_ref, oc_ref, w_ref, x_ref, mod_ref, g2_ref, wr_ref, br_ref, tri_ref,
                 xo_ref, h2_ref, idx_ref, gate_ref, rank_ref, cnt_ref, cnt_scr, *, nlt):
    t = pl.program_id(0)

    @pl.when(t == 0)
    def _():
        cnt_scr[...] = jnp.zeros(cnt_scr.shape, F32)

    o = jnp.where(t < nlt, ol_ref[...], oc_ref[...])
    y = _dot_split(o, w_ref)
    x = x_ref[...] + mod_ref[2:3, :] * y
    xo_ref[...] = x
    h2 = _norm_mod(x, g2_ref[...], mod_ref[3:4, :], mod_ref[4:5, :])
    _store_rows_as_tiles(h2_ref, h2)
    h_hi = h2.astype(BF16)
    h_lo = (h2 - h_hi.astype(F32)).astype(BF16)
    logits = (_dot(h_hi, wr_ref[0]) + _dot(h_lo, wr_ref[0]) + _dot(h_hi, wr_ref[1])) + br_ref[...]
    lane = lax.broadcasted_iota(jnp.int32, logits.shape, 1).astype(F32)
    work = logits
    vals, idxs = [], []
    for _ in range(TOP_K):
        mx = jnp.max(work, axis=1, keepdims=True)
        ix = jnp.min(jnp.where(work == mx, lane, float(LANES)), axis=1, keepdims=True)
        vals.append(mx)
        idxs.append(ix)
        work = jnp.where(lane == ix, -jnp.inf, work)
    es = [jnp.exp(v - vals[0]) for v in vals]
    den = es[0] + es[1] + es[2] + es[3]
    onehots = [lane == ix for ix in idxs]
    oh = jnp.zeros(logits.shape, F32)
    for m in onehots:
        oh = oh + jnp.where(m, 1.0, 0.0)
    before = _dot(tri_ref[...], oh.astype(BF16)) + cnt_scr[0:1, :]
    idx_out = jnp.zeros(logits.shape, F32)
    gate_out = jnp.zeros(logits.shape, F32)
    rank_out = jnp.zeros(logits.shape, F32)
    for r in range(TOP_K):
        sel = lane == float(r)
        rk = jnp.sum(jnp.where(onehots[r], before, 0.0), axis=1, keepdims=True)
        idx_out = jnp.where(sel, idxs[r], idx_out)
        gate_out = jnp.where(sel, es[r] / den, gate_out)
        rank_out = jnp.where(sel, rk, rank_out)
    idx_ref[...] = idx_out.astype(jnp.int32)
    gate_ref[...] = gate_out
    rank_ref[...] = rank_out.astype(jnp.int32)
    cnt_scr[0:1, :] = cnt_scr[0:1, :] + jnp.sum(oh, axis=0, keepdims=True)
    cnt_ref[...] = cnt_scr[...]


def _post(o_lat, o_ctx, ctx_off, w_out, xs, mod_l, g2, w_router, b_router, geo):
    nt, d = xs.shape
    ko = o_lat.shape[1]
    tm, per, nb, nlt = geo["tm"], geo["per"], geo["b"], geo["nlt"]
    ii = lax.broadcasted_iota(jnp.int32, (tm, tm), 0)
    jj = lax.broadcasted_iota(jnp.int32, (tm, tm), 1)
    tri = (jj < ii).astype(BF16)
    wr32 = jnp.zeros((d, LANES), F32).at[:, :N_EXPERTS].set(w_router.astype(F32))
    wr_hi = wr32.astype(BF16)
    wr = jnp.stack([wr_hi, (wr32 - wr_hi.astype(F32)).astype(BF16)])
    br = jnp.full((1, LANES), NEG_BIG, F32).at[0, :N_EXPERTS].set(b_router.astype(F32))
    row = lambda w: pl.BlockSpec((tm, w), lambda i: (i, 0))
    full = lambda a: pl.BlockSpec(a.shape, lambda i: (0,) * a.ndim)
    return pl.pallas_call(
        functools.partial(_post_kernel, nlt=nlt),
        grid=(nt // tm,),
        in_specs=[pl.BlockSpec((tm, ko), lambda i: (jnp.minimum(i, nlt - 1), 0)),
                  pl.BlockSpec((tm, ko), lambda i: (jnp.maximum(i - nlt, 0) + ctx_off, 0)),
                  full(w_out), row(d),
                  pl.BlockSpec((None, 6, d), lambda i: (jnp.minimum(i // per, nb), 0, 0)),
                  full(g2), full(wr), full(br), full(tri)],
        out_specs=[row(d), pl.BlockSpec((tm * SUBLANES, LANES), lambda i: (i, 0)),
                   row(LANES), row(LANES), row(LANES),
                   pl.BlockSpec((8, LANES), lambda i: (0, 0))],
        out_shape=[jax.ShapeDtypeStruct((nt, d), F32), jax.ShapeDtypeStruct((nt * SUBLANES, LANES), F32),
                   jax.ShapeDtypeStruct((nt, LANES), jnp.int32), jax.ShapeDtypeStruct((nt, LANES), F32),
                   jax.ShapeDtypeStruct((nt, LANES), jnp.int32), jax.ShapeDtypeStruct((8, LANES), F32)],
        scratch_shapes=[pltpu.VMEM((8, LANES), F32)],
        input_output_aliases={3: 0},
        compiler_params=_params(1),
        name="post",
    )(o_lat, o_ctx, w_out, xs, mod_l, g2, wr, br, tri)


SUBLANES = 8
DMA_UNROLL = 4


def _row_copy(src, dst, sem):
    return pltpu.make_async_copy(src, dst, sem)


def _store_rows_as_tiles(ref, val):
    rows = val.shape[0]
    for j in range(SUBLANES):
        ref[pl.ds(j, rows, stride=SUBLANES), :] = val[:, j * LANES:(j + 1) * LANES]


def _load_rows_from_tiles(ref):
    rows = ref.shape[0] // SUBLANES
    return jnp.concatenate([ref[pl.ds(j, rows, stride=SUBLANES), :] for j in range(SUBLANES)], axis=1)


def _tile_of(ref, row):
    if isinstance(row, int):
        return ref.at[pl.ds(row * SUBLANES, SUBLANES)]
    return ref.at[pl.ds(pl.multiple_of(row * SUBLANES, SUBLANES), SUBLANES)]


def _dispatch_kernel(zs_ref, zc_ref, slot_ref, h_ref, xs_ref, zero_scr, sem, *, tm):
    t = pl.program_id(0)

    @pl.when(t == 0)
    def _():
        zero_scr[...] = jnp.zeros(zero_scr.shape, F32)

        def per_expert(e, carry):
            start = zs_ref[e]
            n = zc_ref[e]

            def issue(r, c2):
                _row_copy(zero_scr, _tile_of(xs_ref, start + r), sem).start()
                return c2

            lax.fori_loop(0, n, issue, 0)

            def drain(r, c2):
                _row_copy(zero_scr, _tile_of(xs_ref, 0), sem).wait()
                return c2

            lax.fori_loop(0, n, drain, 0)
            return carry

        lax.fori_loop(0, N_EXPERTS, per_expert, 0)

    def issue(r, carry):
        for k in range(TOP_K):
            s = slot_ref[r * TOP_K + k]
            _row_copy(_tile_of(h_ref, r), _tile_of(xs_ref, s), sem).start(priority=k % 2)
        return carry

    lax.fori_loop(0, tm, issue, 0, unroll=DMA_UNROLL)

    def drain(r, carry):
        for k in range(TOP_K):
            _row_copy(_tile_of(h_ref, 0), _tile_of(xs_ref, 0), sem).wait()
        return carry

    lax.fori_loop(0, tm, drain, 0, unroll=2 * DMA_UNROLL)


def _dispatch(h2, slot_flat, zstart, zcount, n_slots, geo):
    sub, lanes = SUBLANES, h2.shape[1]
    nt = h2.shape[0] // sub
    tm = geo["tm"]
    return pl.pallas_call(
        functools.partial(_dispatch_kernel, tm=tm),
        grid_spec=pltpu.PrefetchScalarGridSpec(
            num_scalar_prefetch=2,
            grid=(nt // tm,),
            in_specs=[pl.BlockSpec((tm * TOP_K,), lambda i, zs, zc: (i,), memory_space=pltpu.SMEM),
                      pl.BlockSpec((tm * sub, lanes), lambda i, zs, zc: (i, 0))],
            out_specs=pl.BlockSpec(memory_space=pl.ANY),
            scratch_shapes=[pltpu.VMEM((sub, lanes), F32), pltpu.SemaphoreType.DMA(())]),
        out_shape=jax.ShapeDtypeStruct((n_slots * sub, lanes), F32),
        compiler_params=_params(1),
        name="moe_dispatch",
    )(zstart, zcount, slot_flat, h2)


def _expert_kernel(be_ref, nu_ref, x_ref, wgu_ref, bgu_ref, wd_ref, bd_ref, y_ref, wgu_bf, wd_bf):
    i = pl.program_id(0)

    @pl.when(jnp.logical_or(i == 0, be_ref[i] != be_ref[jnp.maximum(i - 1, 0)]))
    def _():
        wgu_bf[...] = wgu_ref[...].astype(BF16)
        wd_bf[...] = wd_ref[...].astype(BF16)

    @pl.when(i < nu_ref[0])
    def _():
        x = _load_rows_from_tiles(x_ref).astype(BF16)
        gu = _dot(x, wgu_bf[...]) + bgu_ref[...]
        glu = jnp.minimum(gu[:, :D_EXPERT], SWIGLU_LIMIT)
        lin = jnp.clip(gu[:, D_EXPERT:], -SWIGLU_LIMIT, SWIGLU_LIMIT)
        act = glu * (1.0 / (1.0 + jnp.exp(-SWIGLU_ALPHA * glu))) * (lin + 1.0)
        _store_rows_as_tiles(y_ref, _dot(act.astype(BF16), wd_bf[...]) + bd_ref[...])


def _experts(xs, block_expert, n_used, layer, wgu, bgu, wd, bd):
    sub, lanes = SUBLANES, xs.shape[1]
    ns = xs.shape[0] // sub
    nblk = ns // MOE_BLOCK
    depth, ne, d, f2 = wgu.shape
    f = wd.shape[2]
    blk = lambda i, be, nu: (jnp.minimum(i, nu[0] - 1), 0)
    wsel = lambda i, be, nu: (layer, be[i], 0, 0)
    return pl.pallas_call(
        _expert_kernel,
        grid_spec=pltpu.PrefetchScalarGridSpec(
            num_scalar_prefetch=2,
            grid=(nblk,),
            in_specs=[pl.BlockSpec((MOE_BLOCK * sub, lanes), blk),
                      pl.BlockSpec((None, None, d, f2), wsel),
                      pl.BlockSpec((None, None, 1, f2), wsel),
                      pl.BlockSpec((None, None, f, d), wsel),
                      pl.BlockSpec((None, None, 1, d), wsel)],
            out_specs=pl.BlockSpec((MOE_BLOCK * sub, lanes), blk),
            scratch_shapes=[pltpu.VMEM((d, f2), BF16), pltpu.VMEM((f, d), BF16)]),
        out_shape=jax.ShapeDtypeStruct((ns * sub, lanes), F32),
        compiler_params=_params(1),
        name="moe_experts",
    )(block_expert, n_used, xs, wgu, bgu.reshape(depth, ne, 1, f2), wd, bd.reshape(depth, ne, 1, d))


def _combine_kernel(slot_ref, next_slot_ref, gate_ref, x_ref, mod_ref, y_ref, xo_ref, ybuf, sems, *, tm):
    t = pl.program_id(0)
    cur = t % 2

    def gather(slots, buf):
        def issue(r, carry):
            for k in range(TOP_K):
                s = slots[r * TOP_K + k]
                _row_copy(_tile_of(y_ref, s), _tile_of(ybuf.at[buf, k], r), sems.at[buf]).start(priority=k % 2)
            return carry
        lax.fori_loop(0, tm, issue, 0, unroll=DMA_UNROLL)

    @pl.when(t == 0)
    def _():
        gather(slot_ref, 0)

    @pl.when(t + 1 < pl.num_programs(0))
    def _():
        gather(next_slot_ref, 1 - cur)

    def drain(r, carry):
        for k in range(TOP_K):
            _row_copy(_tile_of(y_ref, 0), _tile_of(ybuf.at[cur, k], 0), sems.at[cur]).wait()
        return carry

    lax.fori_loop(0, tm, drain, 0, unroll=2 * DMA_UNROLL)
    gates = gate_ref[...]
    f = gates[:, 0:1] * _load_rows_from_tiles(ybuf.at[cur, 0])
    for k in range(1, TOP_K):
        f = f + gates[:, k:k + 1] * _load_rows_from_tiles(ybuf.at[cur, k])
    xo_ref[...] = x_ref[...] + mod_ref[5:6, :] * f


def _combine(y, slot_flat, gates, xs, mod_l, geo):
    nt, d = xs.shape
    tm, per, nb = geo["tm"], geo["per"], geo["b"]
    last = nt // tm - 1
    return pl.pallas_call(
        functools.partial(_combine_kernel, tm=tm),
        grid=(nt // tm,),
        in_specs=[pl.BlockSpec((tm * TOP_K,), lambda i: (i,), memory_space=pltpu.SMEM),
                  pl.BlockSpec((tm * TOP_K,), lambda i: (jnp.minimum(i + 1, last),), memory_space=pltpu.SMEM),
                  pl.BlockSpec((tm, LANES), lambda i: (i, 0)),
                  pl.BlockSpec((tm, d), lambda i: (i, 0)),
                  pl.BlockSpec((None, 6, d), lambda i: (jnp.minimum(i // per, nb), 0, 0)),
                  pl.BlockSpec(memory_space=pl.ANY)],
        out_specs=pl.BlockSpec((tm, d), lambda i: (i, 0)),
        out_shape=jax.ShapeDtypeStruct((nt, d), F32),
        scratch_shapes=[pltpu.VMEM((2, TOP_K, tm * SUBLANES, LANES), F32), pltpu.SemaphoreType.DMA((2,))],
        input_output_aliases={3: 0},
        compiler_params=_params(1),
        name="moe_combine",
    )(slot_flat, slot_flat, gates, xs, mod_l, y)


def _moe(h2, top_idx, gates, rank, counts, xs, mod_l, layer, wgu, bgu, wd, bd, geo):
    nt = xs.shape[0]
    n_assign = nt * TOP_K
    nblk = n_assign // MOE_BLOCK + N_EXPERTS
    n_slots = nblk * MOE_BLOCK
    counts = counts.astype(jnp.int32)
    padded = (counts + MOE_BLOCK - 1) // MOE_BLOCK * MOE_BLOCK
    pad_end = jnp.cumsum(padded)
    pad_start = pad_end - padded
    experts = jnp.arange(N_EXPERTS, dtype=jnp.int32)
    idx4 = top_idx[:, :TOP_K]
    base = jnp.sum(jnp.where(idx4[:, :, None] == experts[None, None, :], pad_start[None, None, :], 0), axis=-1)
    slot_flat = (base + rank[:, :TOP_K]).reshape(n_assign).astype(jnp.int32)
    n_used = (pad_end[-1] // MOE_BLOCK).astype(jnp.int32)
    blk_id = jnp.minimum(jnp.arange(nblk, dtype=jnp.int32), n_used - 1)
    block_expert = jnp.minimum(
        jnp.sum((pad_end[None, :] <= (blk_id * MOE_BLOCK)[:, None]).astype(jnp.int32), axis=1), N_EXPERTS - 1)
    xs_slots = _dispatch(h2, slot_flat, (pad_start + counts).astype(jnp.int32),
                         (padded - counts).astype(jnp.int32), n_slots, geo)
    y = _experts(xs_slots, block_expert.astype(jnp.int32), n_used.reshape(1), layer, wgu, bgu, wd, bd)
    return _combine(y, slot_flat, gates, xs, mod_l, geo)


def _final_kernel(x_ref, g_ref, o_ref):
    o_ref[...] = _rms(x_ref[...]) * g_ref[...]


def _final_norm(xs, g, nlat, tm):
    d = xs.shape[1]
    return pl.pallas_call(
        _final_kernel,
        grid=(nlat // tm,),
        in_specs=[pl.BlockSpec((tm, d), lambda i: (i, 0)), pl.BlockSpec((1, d), lambda i: (0, 0))],
        out_specs=pl.BlockSpec((tm, d), lambda i: (i, 0)),
        out_shape=jax.ShapeDtypeStruct((nlat, d), F32),
        compiler_params=_params(1),
        name="final_norm",
    )(xs, g)


def _with_identity_rows(cos, sin, tm):
    ones = jnp.ones((tm, cos.shape[1]), F32)
    return jnp.concatenate([cos, ones], axis=0), jnp.concatenate([sin, 0.0 * ones], axis=0)


def _axial_tables(seq, tm, pad_to_lanes):
    pos = jnp.arange(seq)
    rows = (pos // GRID_W).astype(F32)
    cols = (pos % GRID_W).astype(F32)
    half = DA_HEAD_DIM // 2
    inv = ROPE_BASE ** (-jnp.arange(0, half, 2, dtype=F32) / half)
    ar = rows[:, None] * inv[None, :]
    ac = cols[:, None] * inv[None, :]
    cos64 = jnp.concatenate([jnp.cos(ar), jnp.cos(ar), jnp.cos(ac), jnp.cos(ac)], axis=1)
    sin64 = jnp.concatenate([-jnp.sin(ar), jnp.sin(ar), -jnp.sin(ac), jnp.sin(ac)], axis=1)
    if pad_to_lanes:
        cos = jnp.concatenate([cos64, jnp.ones_like(cos64)], axis=1)
        sin = jnp.concatenate([sin64, jnp.zeros_like(sin64)], axis=1)
    else:
        cos = jnp.concatenate([cos64, cos64], axis=1)
        sin = jnp.concatenate([sin64, sin64], axis=1)
    return _with_identity_rows(cos, sin, tm)


def _ret_tables(seq, tm):
    inv = ROPE_BASE ** (-jnp.arange(0, RET_KDIM, 2, dtype=F32) / RET_KDIM)
    ang = jnp.arange(seq).astype(F32)[:, None] * inv[None, :]
    return _with_identity_rows(jnp.cos(ang), jnp.sin(ang), tm)


def _mla_weights(w_in, w_q_up, w_kv_up):
    d = w_in.shape[0]
    kvw = MLA_KV_RANK + MLA_ROPE
    w_in_r = jnp.concatenate([w_in[:, :MLA_KV_RANK], w_in[:, kvw:], w_in[:, MLA_KV_RANK:kvw],
                              jnp.zeros((d, LANES - MLA_ROPE), w_in.dtype)], axis=1)
    wq = w_q_up.reshape(MLA_Q_RANK, MLA_HEADS, MLA_NOPE + MLA_ROPE)
    wq = jnp.concatenate([wq, jnp.zeros((MLA_Q_RANK, MLA_HEADS, LANES - MLA_ROPE), wq.dtype)], axis=2)
    wq = wq.reshape(MLA_Q_RANK, MLA_HEADS * 2 * LANES)
    wkv = w_kv_up.reshape(MLA_KV_RANK, MLA_HEADS, MLA_NOPE + MLA_V)
    wkv = jnp.concatenate([wkv[:, :, :MLA_NOPE].reshape(MLA_KV_RANK, -1),
                           wkv[:, :, MLA_NOPE:].reshape(MLA_KV_RANK, -1)], axis=1)
    return w_in_r.astype(BF16), wq.astype(BF16), wkv.astype(BF16)


def kernel(x, c, ctx, c_ctx, ada_w, ada_b, norm_g, final_g, da_w_in, da_w_out, da_lambda, da_subln_g,
           ret_w_in, ret_decay_logit, ret_w_out, mla_w_in, mla_q_norm_g, mla_w_q_up, mla_kv_norm_g,
           mla_w_kv_up, mla_w_out, moe_w_router, moe_b_router, moe_w_gate_up, moe_b_gate_up,
           moe_w_down, moe_b_down):
    nb, seq, d = x.shape
    nctx_per = ctx.shape[1]
    depth = ada_w.shape[0]
    nlat = nb * seq
    tm = _pick_tile(512, seq, nb * nctx_per)
    geo = dict(b=nb, seq=seq, ctx=nctx_per, nlat=nlat, tm=tm, per=seq // tm, nlt=nlat // tm)
    assert nb < MOD_ROWS and nlat % nctx_per == 0 and nctx_per % RET_CHUNK == 0

    xs = jnp.concatenate([x.reshape(nlat, d), ctx.reshape(nb * nctx_per, d)], axis=0).astype(F32)
    cond = jnp.zeros((MOD_ROWS, d), F32).at[:nb].set(c).at[nb].set(c_ctx)
    mod = _adaln(cond, ada_w, ada_b).reshape(depth, MOD_ROWS, 6, d)

    cos_a, sin_a = _axial_tables(seq, tm, pad_to_lanes=False)
    cos_m, sin_m = _axial_tables(seq, tm, pad_to_lanes=True)
    cos_r, sin_r = _ret_tables(seq, tm)

    for i in range(depth):
        kind = i % N_MIXERS
        j = i // N_MIXERS
        mod_l = mod[i]
        g1 = norm_g[i, 0].reshape(1, d)
        g2 = norm_g[i, 1].reshape(1, d)
        if kind == 0:
            lam_init = 0.8 - 0.6 * math.exp(-0.3 * i)
            p = _proj(xs, g1, mod_l, da_w_in[j].astype(BF16), cos_a, sin_a,
                      ["rope16", "plain", "rope16"], geo)
            o_lat, o_ctx = _da_attention(p, da_lambda[j].astype(F32),
                                         da_subln_g[j].reshape(1, DA_V_DIM).astype(F32), lam_init, geo)
            ctx_off = 0
            w_out = da_w_out[j]
        elif kind == 1:
            p = _proj(xs, g1, mod_l, ret_w_in[j].astype(BF16), cos_r, sin_r,
                      ["ret_k", "plain", "plain", "ret_q", "plain", "plain"], geo,
                      ret_scale=RET_KDIM ** -0.5)
            o_lat = o_ctx = _retention(p, ret_decay_logit[j], geo)
            ctx_off = geo["nlt"]
            w_out = ret_w_out[j]
        else:
            w_in_r, wq, wkv = _mla_weights(mla_w_in[j], mla_w_q_up[j], mla_w_kv_up[j])
            q, kn, v, kr = _mla_proj(xs, g1, mod_l, w_in_r, mla_q_norm_g[j].reshape(1, -1).astype(F32),
                                     mla_kv_norm_g[j].reshape(1, -1).astype(F32), wq, wkv, cos_m, sin_m, geo)
            o_lat, o_ctx = _mla_attention(q, kn, v, kr, geo)
            ctx_off = 0
            w_out = mla_w_out[j]
        xs, h2, top_idx, gates, rank, counts = _post(o_lat, o_ctx, ctx_off, w_out.astype(BF16), xs, mod_l, g2,
                                                     moe_w_router[i], moe_b_router[i], geo)
        xs = _moe(h2, top_idx, gates, rank, counts[0, :N_EXPERTS], xs, mod_l, i,
                  moe_w_gate_up, moe_b_gate_up, moe_w_down, moe_b_down, geo)
    return _final_norm(xs, final_g.reshape(1, d).astype(F32), nlat, tm).reshape(nb, seq, d)
```

```python
import functools
import math

import jax
import jax.numpy as jnp
from jax import lax
from jax.experimental import pallas as pl
from jax.experimental.pallas import tpu as pltpu

F32 = jnp.float32
BF16 = jnp.bfloat16

D_MODEL = 1024
GRID_W = 64
RMS_EPS = 1e-6
ROPE_BASE = 10000.0
N_MIXERS = 3

DA_HEADS = 8
DA_HEAD_DIM = 64
DA_V_DIM = 128

RET_HEADS = 4
RET_KDIM = 256
RET_VDIM = 512
RET_STEP = 256

MLA_HEADS = 8
MLA_NOPE = 128
MLA_ROPE = 64
MLA_V = 128
MLA_Q_RANK = 256
MLA_KV_RANK = 128

N_EXPERTS = 32
TOP_K = 4
D_EXPERT = 1024
SWIGLU_LIMIT = 7.0
SWIGLU_ALPHA = 1.702

LANES = 128
MOD_ROWS = 16
MOE_BLOCK = 512
COL_CHUNK = 1024
ATTN_TQ = 1024
ATTN_TK = 1024
ATTN_Q_SUB = 2
ATTN_ROW_SPLIT = 8
VMEM_LIMIT = 56 * 1024 * 1024
NEG_BIG = -1e30


def _params(n_axes):
    return pltpu.CompilerParams(dimension_semantics=("arbitrary",) * n_axes,
                                vmem_limit_bytes=VMEM_LIMIT)


def _pick_tile(cap, *dims):
    t = cap
    while any(d % t for d in dims):
        t //= 2
    return t


def _rms(x):
    return x * lax.rsqrt(jnp.mean(x * x, axis=-1, keepdims=True) + RMS_EPS)


def _norm_mod(x, g, shift, scale):
    return (_rms(x) * g) * (1.0 + scale) + shift


def _dot(a, b):
    return jnp.dot(a, b, preferred_element_type=F32)


def _dot_split(a, b_ref):
    half = b_ref.shape[1] // 2
    return jnp.concatenate([_dot(a, b_ref[:, :half]), _dot(a, b_ref[:, half:])], axis=1)


def _dot_nt(a, b):
    return lax.dot_general(a, b, (((1,), (1,)), ((), ())), preferred_element_type=F32)


def _dot_tn(a, b):
    return lax.dot_general(a, b, (((0,), (0,)), ((), ())), preferred_element_type=F32)


def _adaln_kernel(c_ref, w_ref, b_ref, o_ref):
    c = c_ref[...]
    a = c * (1.0 / (1.0 + jnp.exp(-c)))
    o_ref[...] = jnp.dot(a, w_ref[...], precision=lax.Precision.HIGHEST,
                         preferred_element_type=F32) + b_ref[...]


def _adaln(cond, ada_w, ada_b):
    depth, d, n = ada_w.shape
    tn = COL_CHUNK
    return pl.pallas_call(
        _adaln_kernel,
        grid=(depth, n // tn),
        in_specs=[pl.BlockSpec((MOD_ROWS, d), lambda l, j: (0, 0)),
                  pl.BlockSpec((None, d, tn), lambda l, j: (l, 0, j)),
                  pl.BlockSpec((None, 1, tn), lambda l, j: (l, 0, j))],
        out_specs=pl.BlockSpec((None, MOD_ROWS, tn), lambda l, j: (l, 0, j)),
        out_shape=jax.ShapeDtypeStruct((depth, MOD_ROWS, n), F32),
        compiler_params=_params(2),
        name="adaln",
    )(cond, ada_w, ada_b.reshape(depth, 1, n))


def _rope_pairs16(x, cos, sin):
    lane = lax.broadcasted_iota(jnp.int32, x.shape, 1)
    first = (lane % 32) < 16
    partner = jnp.where(first, pltpu.roll(x, LANES - 16, 1), pltpu.roll(x, 16, 1))
    return x * cos + partner * sin


def _proj_kernel(x_ref, g_ref, mod_ref, w_ref, cos_ref, sin_ref, o_ref, *, kinds, ret_scale):
    h = _norm_mod(x_ref[...], g_ref[...], mod_ref[0:1, :], mod_ref[1:2, :]).astype(BF16)
    cos = cos_ref[...]
    sin = sin_ref[...]
    half = COL_CHUNK // 2
    for c, kind in enumerate(kinds):
        base = c * COL_CHUNK
        a = jnp.concatenate([_dot(h, w_ref[:, base:base + half]),
                             _dot(h, w_ref[:, base + half:base + COL_CHUNK])], axis=1)
        if kind == "plain":
            o_ref[:, base:base + COL_CHUNK] = a.astype(BF16)
        elif kind == "rope16":
            for g in range(COL_CHUNK // LANES):
                sl = slice(g * LANES, (g + 1) * LANES)
                o_ref[:, base + g * LANES:base + (g + 1) * LANES] = _rope_pairs16(a[:, sl], cos, sin).astype(BF16)
        else:
            scale = ret_scale if kind == "ret_k" else 1.0
            for hh in range(COL_CHUNK // RET_KDIM):
                lo = hh * RET_KDIM
                x1 = a[:, lo:lo + LANES]
                x2 = a[:, lo + LANES:lo + RET_KDIM]
                o_ref[:, base + lo:base + lo + LANES] = ((x1 * cos - x2 * sin) * scale).astype(BF16)
                o_ref[:, base + lo + LANES:base + lo + RET_KDIM] = ((x1 * sin + x2 * cos) * scale).astype(BF16)


def _proj(xs, g, mod_l, w, cos_t, sin_t, kinds, geo, ret_scale=1.0):
    nt, d = xs.shape
    n = w.shape[1]
    tm, per, nlt, nb = geo["tm"], geo["per"], geo["nlt"], geo["b"]
    assert n == COL_CHUNK * len(kinds)
    tab = pl.BlockSpec((tm, LANES), lambda i: (jnp.where(i < nlt, i % per, per), 0))
    return pl.pallas_call(
        functools.partial(_proj_kernel, kinds=tuple(kinds), ret_scale=ret_scale),
        grid=(nt // tm,),
        in_specs=[pl.BlockSpec((tm, d), lambda i: (i, 0)),
                  pl.BlockSpec((1, d), lambda i: (0, 0)),
                  pl.BlockSpec((None, 6, d), lambda i: (jnp.minimum(i // per, nb), 0, 0)),
                  pl.BlockSpec((d, n), lambda i: (0, 0)),
                  tab, tab],
        out_specs=pl.BlockSpec((tm, n), lambda i: (i, 0)),
        out_shape=jax.ShapeDtypeStruct((nt, n), BF16),
        compiler_params=_params(1),
        name="proj",
    )(xs, g, mod_l, w, cos_t, sin_t)


def _mla_proj_kernel(x_ref, g_ref, mod_ref, win_ref, qg_ref, kvg_ref, wq_ref, wkv_ref, cos_ref, sin_ref,
                     q_ref, kn_ref, v_ref, kr_ref):
    h = _norm_mod(x_ref[...], g_ref[...], mod_ref[0:1, :], mod_ref[1:2, :]).astype(BF16)
    p = _dot(h, win_ref[...])
    cos = cos_ref[...]
    sin = sin_ref[...]
    ckv = (_rms(p[:, :MLA_KV_RANK]) * kvg_ref[...]).astype(BF16)
    kv = _dot(ckv, wkv_ref[...])
    nk = MLA_HEADS * MLA_NOPE
    kn_ref[...] = kv[:, :nk].astype(BF16)
    v_ref[...] = kv[:, nk:].astype(BF16)
    kr_ref[...] = _rope_pairs16(p[:, MLA_KV_RANK + MLA_Q_RANK:], cos, sin).astype(BF16)
    cq = (_rms(p[:, MLA_KV_RANK:MLA_KV_RANK + MLA_Q_RANK]) * qg_ref[...]).astype(BF16)
    q = _dot(cq, wq_ref[...])
    for hh in range(MLA_HEADS):
        lo = slice(hh * 2 * LANES, hh * 2 * LANES + LANES)
        hi = slice(hh * 2 * LANES + LANES, (hh + 1) * 2 * LANES)
        q_ref[:, lo] = q[:, lo].astype(BF16)
        q_ref[:, hi] = _rope_pairs16(q[:, hi], cos, sin).astype(BF16)


def _mla_proj(xs, g, mod_l, w_in, qg, kvg, wq, wkv, cos_t, sin_t, geo):
    nt, d = xs.shape
    tm, per, nlt, nb = geo["tm"], geo["per"], geo["nlt"], geo["b"]
    full = lambda a: pl.BlockSpec(a.shape, lambda i: (0,) * a.ndim)
    tab = pl.BlockSpec((tm, LANES), lambda i: (jnp.where(i < nlt, i % per, per), 0))
    nq = MLA_HEADS * 2 * LANES
    nk = MLA_HEADS * MLA_NOPE
    return pl.pallas_call(
        _mla_proj_kernel,
        grid=(nt // tm,),
        in_specs=[pl.BlockSpec((tm, d), lambda i: (i, 0)), full(g),
                  pl.BlockSpec((None, 6, d), lambda i: (jnp.minimum(i // per, nb), 0, 0)),
                  full(w_in), full(qg), full(kvg), full(wq), full(wkv), tab, tab],
        out_specs=[pl.BlockSpec((tm, nq), lambda i: (i, 0)),
                   pl.BlockSpec((tm, nk), lambda i: (i, 0)),
                   pl.BlockSpec((tm, nk), lambda i: (i, 0)),
                   pl.BlockSpec((tm, LANES), lambda i: (i, 0))],
        out_shape=[jax.ShapeDtypeStruct((nt, nq), BF16), jax.ShapeDtypeStruct((nt, nk), BF16),
                   jax.ShapeDtypeStruct((nt, nk), BF16), jax.ShapeDtypeStruct((nt, LANES), BF16)],
        compiler_params=_params(1),
        name="mla_proj",
    )(xs, g, mod_l, w_in, qg, kvg, wq, wkv, cos_t, sin_t)


LOG2E = 1.4426950408889634


def _flash_init(m_ref, a_ref):
    m_ref[...] = jnp.full(m_ref.shape, -jnp.inf, F32)
    a_ref[...] = jnp.zeros(a_ref.shape, F32)


def _ones_column(rows):
    lane = lax.broadcasted_iota(jnp.int32, (rows, LANES), 1)
    return jnp.where(lane == 0, 1.0, 0.0).astype(BF16)


def _chunk_rows(j, tk):
    if isinstance(j, int):
        return pl.ds(j * tk, tk)
    return pl.ds(pl.multiple_of(j * tk, tk), tk)


def _softmax_pv(parts, stats):
    for mp, (m_ref, a_ref) in enumerate(stats):
        tq = m_ref.shape[0]
        rb = tq // ATTN_ROW_SPLIT if tq % ATTN_ROW_SPLIT == 0 else tq
        for r0 in range(0, tq, rb):
            rows = slice(r0, r0 + rb)
            tiles = [scores[mp][rows, :] for scores, _ in parts]
            m_prev = m_ref[rows, :]
            m_new = m_prev
            for s2 in tiles:
                m_new = jnp.maximum(m_new, jnp.max(s2, axis=1, keepdims=True))
            acc = jnp.exp2(m_prev - m_new) * a_ref[rows, :]
            for s2, (_, v_aug) in zip(tiles, parts):
                acc = acc + _dot(jnp.exp2(s2 - m_new).astype(BF16), v_aug)
            a_ref[rows, :] = acc
            m_ref[rows, :] = m_new


def _flash_lat(qk_fns, stats, k_ctx, v_ctx, k_at, v_at, s_a, s_b, s_c, n_chunks):
    def qk_into(dst, k):
        for mp, f in enumerate(qk_fns):
            dst[mp] = f(k)

    qk_into(s_c, k_ctx)
    qk_into(s_a, k_at(0))
    qk_into(s_b, k_at(1))
    _softmax_pv([(s_c, v_ctx), (s_a, v_at(0))], stats)

    def body(t, carry):
        j = 2 * t + 1
        qk_into(s_a, k_at(j + 1))
        _softmax_pv([(s_b, v_at(j))], stats)
        qk_into(s_b, k_at(j + 2))
        _softmax_pv([(s_a, v_at(j + 1))], stats)
        return carry

    lax.fori_loop(0, (n_chunks - 2) // 2, body, 0)
    _softmax_pv([(s_b, v_at(n_chunks - 1))], stats)


def _flash_result(a_ref, vdim):
    a = a_ref[...]
    return a[:, :vdim] / a[:, vdim:vdim + 1]


def _da_attn_kernel(*refs, n_chunks, tk, tq, lam_init, scale):
    if n_chunks:
        (lam_ref, sg_ref, q_ref, kc_ref, vc_ref, kl_ref, vl_ref, o_ref, m0, a0, m1, a1, s_a, s_b, s_c) = refs
    else:
        (lam_ref, sg_ref, q_ref, kc_ref, vc_ref, o_ref, m0, a0, m1, a1) = refs
    c2 = scale * LOG2E
    v_ctx = jnp.concatenate([vc_ref[...], _ones_column(vc_ref.shape[0])], axis=1)
    lf = lam_ref[...]
    lam = (jnp.exp(jnp.sum(lf[0:1] * lf[1:2], axis=1, keepdims=True))
           - jnp.exp(jnp.sum(lf[2:3] * lf[3:4], axis=1, keepdims=True)) + lam_init)

    for r0 in range(0, q_ref.shape[0], tq):
        rows = pl.ds(r0, tq)
        q = q_ref[rows, :]
        lane = lax.broadcasted_iota(jnp.int32, q.shape, 1)
        zero = jnp.zeros_like(q)
        q_lo = jnp.where(lane < DA_HEAD_DIM, q, zero)
        q_hi = jnp.where(lane >= DA_HEAD_DIM, q, zero)
        stats = [(m0.at[rows], a0.at[rows]), (m1.at[rows], a1.at[rows])]
        for m_ref, a_ref in stats:
            _flash_init(m_ref, a_ref)
        qk_fns = [lambda k, ql=q_lo: _dot_nt(ql, k) * c2, lambda k, qh=q_hi: _dot_nt(qh, k) * c2]
        if n_chunks:
            ones = _ones_column(tk)

            def k_at(j):
                return kl_ref[_chunk_rows(j, tk), :]

            def v_at(j):
                return jnp.concatenate([vl_ref[_chunk_rows(j, tk), :], ones], axis=1)

            _flash_lat(qk_fns, stats, kc_ref[...], v_ctx, k_at, v_at, s_a, s_b, s_c, n_chunks)
        else:
            _softmax_pv([([f(kc_ref[...]) for f in qk_fns], v_ctx)], stats)
        o = _flash_result(stats[0][1], DA_V_DIM) - lam * _flash_result(stats[1][1], DA_V_DIM)
        o = _rms(o) * sg_ref[...] * (1.0 - lam_init)
        o_ref[rows, :] = o.astype(BF16)


def _attn_scratch(q_rows, vdim, n_maps, tq=0, tk=0, ctx=0):
    s = []
    for _ in range(n_maps):
        s += [pltpu.VMEM((q_rows, 1), F32), pltpu.VMEM((q_rows, vdim + LANES), F32)]
    if tk:
        s += [pltpu.VMEM((n_maps, tq, tk), F32), pltpu.VMEM((n_maps, tq, tk), F32),
              pltpu.VMEM((n_maps, tq, ctx), F32)]
    return s


def _da_attention(p, lam_vecs, subg, lam_init, geo):
    nt = p.shape[0]
    nb, seq, ctx, nlat = geo["b"], geo["seq"], geo["ctx"], geo["nlat"]
    hh = DA_HEADS
    tq = _pick_tile(ATTN_TQ, seq)
    tk = _pick_tile(min(ATTN_TK, seq // 2), seq)
    assert (seq // tk) % 2 == 0
    tqb = tq * ATTN_Q_SUB if seq % (tq * ATTN_Q_SUB) == 0 else tq
    nqt = seq // tqb
    cb = nlat // ctx
    scale = DA_HEAD_DIM ** -0.5
    small = [pl.BlockSpec(lam_vecs.shape, lambda *a: (0, 0)), pl.BlockSpec(subg.shape, lambda *a: (0, 0))]
    o_lat = pl.pallas_call(
        functools.partial(_da_attn_kernel, n_chunks=seq // tk, tk=tk, tq=tq, lam_init=lam_init, scale=scale),
        grid=(nb, hh, nqt),
        in_specs=small + [
            pl.BlockSpec((tqb, LANES), lambda b, h, i: (b * nqt + i, 2 * hh + h)),
            pl.BlockSpec((ctx, LANES), lambda b, h, i: (cb + b, h)),
            pl.BlockSpec((ctx, LANES), lambda b, h, i: (cb + b, hh + h)),
            pl.BlockSpec((seq, LANES), lambda b, h, i: (b, h)),
            pl.BlockSpec((seq, LANES), lambda b, h, i: (b, hh + h))],
        out_specs=pl.BlockSpec((tqb, LANES), lambda b, h, i: (b * nqt + i, h)),
        out_shape=jax.ShapeDtypeStruct((nlat, hh * DA_V_DIM), BF16),
        scratch_shapes=_attn_scratch(tqb, DA_V_DIM, 2, tq, tk, ctx),
        compiler_params=_params(3),
        name="da_attn_lat",
    )(lam_vecs, subg, p, p, p, p, p)
    o_ctx = pl.pallas_call(
        functools.partial(_da_attn_kernel, n_chunks=0, tk=tk, tq=ctx, lam_init=lam_init, scale=scale),
        grid=(nb, hh),
        in_specs=small + [
            pl.BlockSpec((ctx, LANES), lambda b, h: (cb + b, 2 * hh + h)),
            pl.BlockSpec((ctx, LANES), lambda b, h: (cb + b, h)),
            pl.BlockSpec((ctx, LANES), lambda b, h: (cb + b, hh + h))],
        out_specs=pl.BlockSpec((ctx, LANES), lambda b, h: (b, h)),
        out_shape=jax.ShapeDtypeStruct((nt - nlat, hh * DA_V_DIM), BF16),
        scratch_shapes=_attn_scratch(ctx, DA_V_DIM, 2),
        compiler_params=_params(2),
        name="da_attn_ctx",
    )(lam_vecs, subg, p, p, p)
    return o_lat, o_ctx


def _mla_attn_kernel(*refs, n_chunks, tk, tq, scale):
    if n_chunks:
        (q_ref, knc_ref, krc_ref, vc_ref, knl_ref, krl_ref, vl_ref, o_ref, m0, a0, s_a, s_b, s_c) = refs
    else:
        (q_ref, knc_ref, krc_ref, vc_ref, o_ref, m0, a0) = refs
    c2 = scale * LOG2E
    k_ctx = jnp.concatenate([knc_ref[...], krc_ref[...]], axis=1)
    v_ctx = jnp.concatenate([vc_ref[...], _ones_column(vc_ref.shape[0])], axis=1)
    for r0 in range(0, q_ref.shape[0], tq):
        rows = pl.ds(r0, tq)
        q = q_ref[rows, :]
        stats = [(m0.at[rows], a0.at[rows])]
        _flash_init(*stats[0])
        qk_fns = [lambda k, q=q: _dot_nt(q, k) * c2]
        if n_chunks:
            ones = _ones_column(tk)

            def k_at(j):
                kr = _chunk_rows(j, tk)
                return jnp.concatenate([knl_ref[kr, :], krl_ref[kr, :]], axis=1)

            def v_at(j):
                return jnp.concatenate([vl_ref[_chunk_rows(j, tk), :], ones], axis=1)

            _flash_lat(qk_fns, stats, k_ctx, v_ctx, k_at, v_at, s_a, s_b, s_c, n_chunks)
        else:
            _softmax_pv([([qk_fns[0](k_ctx)], v_ctx)], stats)
        o_ref[rows, :] = _flash_result(stats[0][1], MLA_V).astype(BF16)


def _mla_attention(q, kn, v, kr, geo):
    nt = q.shape[0]
    nb, seq, ctx, nlat = geo["b"], geo["seq"], geo["ctx"], geo["nlat"]
    hh = MLA_HEADS
    tq = _pick_tile(ATTN_TQ, seq)
    tk = _pick_tile(min(ATTN_TK, seq // 2), seq)
    assert (seq // tk) % 2 == 0
    tqb = tq * ATTN_Q_SUB if seq % (tq * ATTN_Q_SUB) == 0 else tq
    nqt = seq // tqb
    cb = nlat // ctx
    scale = (MLA_NOPE + MLA_ROPE) ** -0.5
    o_lat = pl.pallas_call(
        functools.partial(_mla_attn_kernel, n_chunks=seq // tk, tk=tk, tq=tq, scale=scale),
        grid=(nb, hh, nqt),
        in_specs=[
            pl.BlockSpec((tqb, 2 * LANES), lambda b, h, i: (b * nqt + i, h)),
            pl.BlockSpec((ctx, LANES), lambda b, h, i: (cb + b, h)),
            pl.BlockSpec((ctx, LANES), lambda b, h, i: (cb + b, 0)),
            pl.BlockSpec((ctx, LANES), lambda b, h, i: (cb + b, h)),
            pl.BlockSpec((seq, LANES), lambda b, h, i: (b, h)),
            pl.BlockSpec((seq, LANES), lambda b, h, i: (b, 0)),
            pl.BlockSpec((seq, LANES), lambda b, h, i: (b, h))],
        out_specs=pl.BlockSpec((tqb, LANES), lambda b, h, i: (b * nqt + i, h)),
        out_shape=jax.ShapeDtypeStruct((nlat, hh * MLA_V), BF16),
        scratch_shapes=_attn_scratch(tqb, MLA_V, 1, tq, tk, ctx),
        compiler_params=_params(3),
        name="mla_attn_lat",
    )(q, kn, kr, v, kn, kr, v)
    o_ctx = pl.pallas_call(
        functools.partial(_mla_attn_kernel, n_chunks=0, tk=tk, tq=ctx, scale=scale),
        grid=(nb, hh),
        in_specs=[
            pl.BlockSpec((ctx, 2 * LANES), lambda b, h: (cb + b, h)),
            pl.BlockSpec((ctx, LANES), lambda b, h: (cb + b, h)),
            pl.BlockSpec((ctx, LANES), lambda b, h: (cb + b, 0)),
            pl.BlockSpec((ctx, LANES), lambda b, h: (cb + b, h))],
        out_specs=pl.BlockSpec((ctx, LANES), lambda b, h: (b, h)),
        out_shape=jax.ShapeDtypeStruct((nt - nlat, hh * MLA_V), BF16),
        scratch_shapes=_attn_scratch(ctx, MLA_V, 1),
        compiler_params=_params(2),
        name="mla_attn_ctx",
    )(q, kn, kr, v)
    return o_lat, o_ctx


def _ret_kernel(*refs, reverse, c):
    if reverse:
        dl_ref, q_ref, k_ref, v0_ref, v1_ref, of_ref, g0_ref, g1_ref, o_ref, st = refs
    else:
        dl_ref, q_ref, k_ref, v0_ref, v1_ref, o_ref, st = refs
    s = pl.program_id(1)

    @pl.when(s == 0)
    def _():
        st[...] = jnp.zeros(st.shape, F32)

    dl = dl_ref[...]
    lsig = jnp.minimum(dl, 0.0) - jnp.log(1.0 + jnp.exp(-jnp.abs(dl)))
    row = lax.broadcasted_iota(jnp.int32, dl.shape, 0)
    col = lax.broadcasted_iota(jnp.int32, dl.shape, 1)
    i2 = lax.broadcasted_iota(jnp.int32, (c, c), 0)
    j2 = lax.broadcasted_iota(jnp.int32, (c, c), 1)
    pos = lax.broadcasted_iota(jnp.int32, (c, 1), 0).astype(F32)
    dist = ((j2 - i2) if reverse else (i2 - j2)).astype(F32)
    keep = dist >= 0.0
    dist = jnp.where(keep, dist, 0.0)
    per_blk = COL_CHUNK // RET_VDIM

    for hd in range(RET_HEADS):
        pick = jnp.logical_and(row == (1 if reverse else 0), col == hd)
        lg = jnp.sum(jnp.sum(jnp.where(pick, lsig, 0.0), axis=1, keepdims=True), axis=0, keepdims=True)
        if reverse:
            q_dec = jnp.exp((c - pos) * lg)
            k_dec = jnp.exp(pos * lg)
        else:
            q_dec = jnp.exp((pos + 1.0) * lg)
            k_dec = jnp.exp((c - 1.0 - pos) * lg)
        intra = jnp.where(keep, jnp.exp(dist * lg), 0.0)
        chunk_dec = jnp.exp(c * lg)

        ksl = slice(hd * RET_KDIM, (hd + 1) * RET_KDIM)
        vsl = slice((hd % per_blk) * RET_VDIM, (hd % per_blk + 1) * RET_VDIM)
        osl = slice(hd * RET_VDIM, (hd + 1) * RET_VDIM)
        qb = q_ref[:, ksl]
        kb = k_ref[:, ksl]
        vb = (v0_ref if hd < per_blk else v1_ref)[:, vsl]
        state = st[hd]
        sc = _dot_nt(qb, kb) * intra
        o = _dot(sc.astype(BF16), vb) + _dot((qb.astype(F32) * q_dec).astype(BF16), state.astype(BF16))
        st[hd] = chunk_dec * state + _dot_tn((kb.astype(F32) * k_dec).astype(BF16), vb)
        if reverse:
            tot = _rms(of_ref[:, osl] + o)
            g = (g0_ref if hd < per_blk else g1_ref)[:, vsl].astype(F32)
            o_ref[:, osl] = (g * (1.0 / (1.0 + jnp.exp(-g))) * tot).astype(BF16)
        else:
            o_ref[:, osl] = o


def _retention(p, decay_logit, geo):
    nt = p.shape[0]
    nb, seq, ctx, nlat = geo["b"], geo["seq"], geo["ctx"], geo["nlat"]
    c = _pick_tile(RET_STEP, ctx, seq)
    ncc, ncl = ctx // c, seq // c
    hh = RET_HEADS
    kq, kv = RET_KDIM, RET_VDIM
    wb = COL_CHUNK
    assert hh * kq == wb and hh * kv == 2 * wb
    dl = jnp.zeros((8, LANES), F32).at[:2, :hh].set(decay_logit.astype(F32))

    def rows_fwd(b, s):
        return jnp.where(s < ncc, nlat // c + b * ncc + s, b * ncl + (s - ncc))

    def rows_bwd(b, s):
        return jnp.where(s < ncc, nlat // c + b * ncc + (ncc - 1 - s), b * ncl + (ncl - 1 - (s - ncc)))

    def blk(rows, col):
        return pl.BlockSpec((c, wb), lambda b, s: (rows(b, s), col))

    def specs(rows):
        return [pl.BlockSpec((8, LANES), lambda b, s: (0, 0)), blk(rows, 3), blk(rows, 0), blk(rows, 1), blk(rows, 2)]

    grid = (nb, ncc + ncl)
    scratch = [pltpu.VMEM((hh, kq, kv), F32)]
    o_f = pl.pallas_call(
        functools.partial(_ret_kernel, reverse=False, c=c),
        grid=grid,
        in_specs=specs(rows_fwd),
        out_specs=pl.BlockSpec((c, hh * kv), lambda b, s: (rows_fwd(b, s), 0)),
        out_shape=jax.ShapeDtypeStruct((nt, hh * kv), F32),
        scratch_shapes=scratch,
        compiler_params=_params(2),
        name="ret_fwd",
    )(dl, p, p, p, p)
    return pl.pallas_call(
        functools.partial(_ret_kernel, reverse=True, c=c),
        grid=grid,
        in_specs=specs(rows_bwd) + [
            pl.BlockSpec((c, hh * kv), lambda b, s: (rows_bwd(b, s), 0)), blk(rows_bwd, 4), blk(rows_bwd, 5)],
        out_specs=pl.BlockSpec((c, hh * kv), lambda b, s: (rows_bwd(b, s), 0)),
        out_shape=jax.ShapeDtypeStruct((nt, hh * kv), BF16),
        scratch_shapes=scratch,
        compiler_params=_params(2),
        name="ret_bwd",
    )(dl, p, p, p, p, o_f, p, p)


def _post_kernel(ol_ref, oc_ref, w_ref, x_ref, mod_ref, g2_ref, wr_ref, br_ref, tri_ref,
                 xo_ref, h2_ref, idx_ref, gate_ref, rank_ref, cnt_ref, cnt_scr, *, nlt):
    t = pl.program_id(0)

    @pl.when(t == 0)
    def _():
        cnt_scr[...] = jnp.zeros(cnt_scr.shape, F32)

    o = jnp.where(t < nlt, ol_ref[...], oc_ref[...])
    y = _dot_split(o, w_ref)
    x = x_ref[...] + mod_ref[2:3, :] * y
    xo_ref[...] = x
    h2 = _norm_mod(x, g2_ref[...], mod_ref[3:4, :], mod_ref[4:5, :])
    _store_rows_as_tiles(h2_ref, h2)
    h_hi = h2.astype(BF16)
    h_lo = (h2 - h_hi.astype(F32)).astype(BF16)
    logits = (_dot(h_hi, wr_ref[0]) + _dot(h_lo, wr_ref[0]) + _dot(h_hi, wr_ref[1])) + br_ref[...]
    lane = lax.broadcasted_iota(jnp.int32, logits.shape, 1).astype(F32)
    work = logits
    vals, idxs = [], []
    for _ in range(TOP_K):
        mx = jnp.max(work, axis=1, keepdims=True)
        ix = jnp.min(jnp.where(work == mx, lane, float(LANES)), axis=1, keepdims=True)
        vals.append(mx)
        idxs.append(ix)
        work = jnp.where(lane == ix, -jnp.inf, work)
    es = [jnp.exp(v - vals[0]) for v in vals]
    den = es[0] + es[1] + es[2] + es[3]
    onehots = [lane == ix for ix in idxs]
    oh = jnp.zeros(logits.shape, F32)
    for m in onehots:
        oh = oh + jnp.where(m, 1.0, 0.0)
    before = _dot(tri_ref[...], oh.astype(BF16)) + cnt_scr[0:1, :]
    idx_out = jnp.zeros(logits.shape, F32)
    gate_out = jnp.zeros(logits.shape, F32)
    rank_out = jnp.zeros(logits.shape, F32)
    for r in range(TOP_K):
        sel = lane == float(r)
        rk = jnp.sum(jnp.where(onehots[r], before, 0.0), axis=1, keepdims=True)
        idx_out = jnp.where(sel, idxs[r], idx_out)
        gate_out = jnp.where(sel, es[r] / den, gate_out)
        rank_out = jnp.where(sel, rk, rank_out)
    idx_ref[...] = idx_out.astype(jnp.int32)
    gate_ref[...] = gate_out
    rank_ref[...] = rank_out.astype(jnp.int32)
    cnt_scr[0:1, :] = cnt_scr[0:1, :] + jnp.sum(oh, axis=0, keepdims=True)
    cnt_ref[...] = cnt_scr[...]


def _post(o_lat, o_ctx, ctx_off, w_out, xs, mod_l, g2, w_router, b_router, geo):
    nt, d = xs.shape
    ko = o_lat.shape[1]
    tm, per, nb, nlt = geo["tm"], geo["per"], geo["b"], geo["nlt"]
    ii = lax.broadcasted_iota(jnp.int32, (tm, tm), 0)
    jj = lax.broadcasted_iota(jnp.int32, (tm, tm), 1)
    tri = (jj < ii).astype(BF16)
    wr32 = jnp.zeros((d, LANES), F32).at[:, :N_EXPERTS].set(w_router.astype(F32))
    wr_hi = wr32.astype(BF16)
    wr = jnp.stack([wr_hi, (wr32 - wr_hi.astype(F32)).astype(BF16)])
    br = jnp.full((1, LANES), NEG_BIG, F32).at[0, :N_EXPERTS].set(b_router.astype(F32))
    row = lambda w: pl.BlockSpec((tm, w), lambda i: (i, 0))
    full = lambda a: pl.BlockSpec(a.shape, lambda i: (0,) * a.ndim)
    return pl.pallas_call(
        functools.partial(_post_kernel, nlt=nlt),
        grid=(nt // tm,),
        in_specs=[pl.BlockSpec((tm, ko), lambda i: (jnp.minimum(i, nlt - 1), 0)),
                  pl.BlockSpec((tm, ko), lambda i: (jnp.maximum(i - nlt, 0) + ctx_off, 0)),
                  full(w_out), row(d),
                  pl.BlockSpec((None, 6, d), lambda i: (jnp.minimum(i // per, nb), 0, 0)),
                  full(g2), full(wr), full(br), full(tri)],
        out_specs=[row(d), pl.BlockSpec((tm * SUBLANES, LANES), lambda i: (i, 0)),
                   row(LANES), row(LANES), row(LANES),
                   pl.BlockSpec((8, LANES), lambda i: (0, 0))],
        out_shape=[jax.ShapeDtypeStruct((nt, d), F32), jax.ShapeDtypeStruct((nt * SUBLANES, LANES), F32),
                   jax.ShapeDtypeStruct((nt, LANES), jnp.int32), jax.ShapeDtypeStruct((nt, LANES), F32),
                   jax.ShapeDtypeStruct((nt, LANES), jnp.int32), jax.ShapeDtypeStruct((8, LANES), F32)],
        scratch_shapes=[pltpu.VMEM((8, LANES), F32)],
        input_output_aliases={3: 0},
        compiler_params=_params(1),
        name="post",
    )(o_lat, o_ctx, w_out, xs, mod_l, g2, wr, br, tri)


SUBLANES = 8
DMA_UNROLL = 4


def _row_copy(src, dst, sem):
    return pltpu.make_async_copy(src, dst, sem)


def _store_rows_as_tiles(ref, val):
    rows = val.shape[0]
    for j in range(SUBLANES):
        ref[pl.ds(j, rows, stride=SUBLANES), :] = val[:, j * LANES:(j + 1) * LANES]


def _load_rows_from_tiles(ref):
    rows = ref.shape[0] // SUBLANES
    return jnp.concatenate([ref[pl.ds(j, rows, stride=SUBLANES), :] for j in range(SUBLANES)], axis=1)


def _tile_of(ref, row):
    if isinstance(row, int):
        return ref.at[pl.ds(row * SUBLANES, SUBLANES)]
    return ref.at[pl.ds(pl.multiple_of(row * SUBLANES, SUBLANES), SUBLANES)]


def _dispatch_kernel(zs_ref, zc_ref, slot_ref, h_ref, xs_ref, zero_scr, sem, *, tm):
    t = pl.program_id(0)

    @pl.when(t == 0)
    def _():
        zero_scr[...] = jnp.zeros(zero_scr.shape, F32)

        def per_expert(e, carry):
            start = zs_ref[e]
            n = zc_ref[e]

            def issue(r, c2):
                _row_copy(zero_scr, _tile_of(xs_ref, start + r), sem).start()
                return c2

            lax.fori_loop(0, n, issue, 0)

            def drain(r, c2):
                _row_copy(zero_scr, _tile_of(xs_ref, 0), sem).wait()
                return c2

            lax.fori_loop(0, n, drain, 0)
            return carry

        lax.fori_loop(0, N_EXPERTS, per_expert, 0)

    def issue(r, carry):
        for k in range(TOP_K):
            s = slot_ref[r * TOP_K + k]
            _row_copy(_tile_of(h_ref, r), _tile_of(xs_ref, s), sem).start(priority=k % 2)
        return carry

    lax.fori_loop(0, tm, issue, 0, unroll=DMA_UNROLL)

    def drain(r, carry):
        for k in range(TOP_K):
            _row_copy(_tile_of(h_ref, 0), _tile_of(xs_ref, 0), sem).wait()
        return carry

    lax.fori_loop(0, tm, drain, 0, unroll=2 * DMA_UNROLL)


def _dispatch(h2, slot_flat, zstart, zcount, n_slots, geo):
    sub, lanes = SUBLANES, h2.shape[1]
    nt = h2.shape[0] // sub
    tm = geo["tm"]
    return pl.pallas_call(
        functools.partial(_dispatch_kernel, tm=tm),
        grid_spec=pltpu.PrefetchScalarGridSpec(
            num_scalar_prefetch=2,
            grid=(nt // tm,),
            in_specs=[pl.BlockSpec((tm * TOP_K,), lambda i, zs, zc: (i,), memory_space=pltpu.SMEM),
                      pl.BlockSpec((tm * sub, lanes), lambda i, zs, zc: (i, 0))],
            out_specs=pl.BlockSpec(memory_space=pl.ANY),
            scratch_shapes=[pltpu.VMEM((sub, lanes), F32), pltpu.SemaphoreType.DMA(())]),
        out_shape=jax.ShapeDtypeStruct((n_slots * sub, lanes), F32),
        compiler_params=_params(1),
        name="moe_dispatch",
    )(zstart, zcount, slot_flat, h2)


def _expert_kernel(be_ref, nu_ref, x_ref, wgu_ref, bgu_ref, wd_ref, bd_ref, y_ref, wgu_bf, wd_bf):
    i = pl.program_id(0)

    @pl.when(jnp.logical_or(i == 0, be_ref[i] != be_ref[jnp.maximum(i - 1, 0)]))
    def _():
        wgu_bf[...] = wgu_ref[...].astype(BF16)
        wd_bf[...] = wd_ref[...].astype(BF16)

    @pl.when(i < nu_ref[0])
    def _():
        x = _load_rows_from_tiles(x_ref).astype(BF16)
        gu = _dot(x, wgu_bf[...]) + bgu_ref[...]
        glu = jnp.minimum(gu[:, :D_EXPERT], SWIGLU_LIMIT)
        lin = jnp.clip(gu[:, D_EXPERT:], -SWIGLU_LIMIT, SWIGLU_LIMIT)
        act = glu * (1.0 / (1.0 + jnp.exp(-SWIGLU_ALPHA * glu))) * (lin + 1.0)
        _store_rows_as_tiles(y_ref, _dot(act.astype(BF16), wd_bf[...]) + bd_ref[...])


def _experts(xs, block_expert, n_used, layer, wgu, bgu, wd, bd):
    sub, lanes = SUBLANES, xs.shape[1]
    ns = xs.shape[0] // sub
    nblk = ns // MOE_BLOCK
    depth, ne, d, f2 = wgu.shape
    f = wd.shape[2]
    blk = lambda i, be, nu: (jnp.minimum(i, nu[0] - 1), 0)
    wsel = lambda i, be, nu: (layer, be[i], 0, 0)
    return pl.pallas_call(
        _expert_kernel,
        grid_spec=pltpu.PrefetchScalarGridSpec(
            num_scalar_prefetch=2,
            grid=(nblk,),
            in_specs=[pl.BlockSpec((MOE_BLOCK * sub, lanes), blk),
                      pl.BlockSpec((None, None, d, f2), wsel),
                      pl.BlockSpec((None, None, 1, f2), wsel),
                      pl.BlockSpec((None, None, f, d), wsel),
                      pl.BlockSpec((None, None, 1, d), wsel)],
            out_specs=pl.BlockSpec((MOE_BLOCK * sub, lanes), blk),
            scratch_shapes=[pltpu.VMEM((d, f2), BF16), pltpu.VMEM((f, d), BF16)]),
        out_shape=jax.ShapeDtypeStruct((ns * sub, lanes), F32),
        compiler_params=_params(1),
        name="moe_experts",
    )(block_expert, n_used, xs, wgu, bgu.reshape(depth, ne, 1, f2), wd, bd.reshape(depth, ne, 1, d))


def _combine_kernel(*refs, tm, final):
    if final:
        slot_ref, next_slot_ref, gate_ref, x_ref, mod_ref, fg_ref, y_ref, xo_ref, ybuf, sems = refs
    else:
        slot_ref, next_slot_ref, gate_ref, x_ref, mod_ref, y_ref, xo_ref, ybuf, sems = refs
    t = pl.program_id(0)
    cur = t % 2

    def gather(slots, buf):
        def issue(r, carry):
            for k in range(TOP_K):
                s = slots[r * TOP_K + k]
                _row_copy(_tile_of(y_ref, s), _tile_of(ybuf.at[buf, k], r), sems.at[buf]).start(priority=k % 2)
            return carry
        lax.fori_loop(0, tm, issue, 0, unroll=DMA_UNROLL)

    @pl.when(t == 0)
    def _():
        gather(slot_ref, 0)

    @pl.when(t + 1 < pl.num_programs(0))
    def _():
        gather(next_slot_ref, 1 - cur)

    def drain(r, carry):
        for k in range(TOP_K):
            _row_copy(_tile_of(y_ref, 0), _tile_of(ybuf.at[cur, k], 0), sems.at[cur]).wait()
        return carry

    lax.fori_loop(0, tm, drain, 0, unroll=2 * DMA_UNROLL)
    gates = gate_ref[...]
    f = gates[:, 0:1] * _load_rows_from_tiles(ybuf.at[cur, 0])
    for k in range(1, TOP_K):
        f = f + gates[:, k:k + 1] * _load_rows_from_tiles(ybuf.at[cur, k])
    x = x_ref[...] + mod_ref[5:6, :] * f
    xo_ref[...] = _rms(x) * fg_ref[...] if final else x


def _combine(y, slot_flat, gates, xs, mod_l, geo, final_g=None):
    nt, d = xs.shape
    tm, per, nb = geo["tm"], geo["per"], geo["b"]
    final = final_g is not None
    n_tiles = geo["nlt"] if final else nt // tm
    last = n_tiles - 1
    extra_specs = [pl.BlockSpec((1, d), lambda i: (0, 0))] if final else []
    extra_args = [final_g] if final else []
    return pl.pallas_call(
        functools.partial(_combine_kernel, tm=tm, final=final),
        grid=(n_tiles,),
        in_specs=[pl.BlockSpec((tm * TOP_K,), lambda i: (i,), memory_space=pltpu.SMEM),
                  pl.BlockSpec((tm * TOP_K,), lambda i: (jnp.minimum(i + 1, last),), memory_space=pltpu.SMEM),
                  pl.BlockSpec((tm, LANES), lambda i: (i, 0)),
                  pl.BlockSpec((tm, d), lambda i: (i, 0)),
                  pl.BlockSpec((None, 6, d), lambda i: (jnp.minimum(i // per, nb), 0, 0))]
                 + extra_specs + [pl.BlockSpec(memory_space=pl.ANY)],
        out_specs=pl.BlockSpec((tm, d), lambda i: (i, 0)),
        out_shape=jax.ShapeDtypeStruct((n_tiles * tm, d), F32),
        scratch_shapes=[pltpu.VMEM((2, TOP_K, tm * SUBLANES, LANES), F32), pltpu.SemaphoreType.DMA((2,))],
        input_output_aliases={} if final else {3: 0},
        compiler_params=_params(1),
        name="moe_combine",
    )(slot_flat, slot_flat, gates, xs, mod_l, *extra_args, y)


def _moe(h2, top_idx, gates, rank, counts, xs, mod_l, layer, wgu, bgu, wd, bd, geo, final_g=None):
    nt = xs.shape[0]
    n_assign = nt * TOP_K
    nblk = n_assign // MOE_BLOCK + N_EXPERTS
    n_slots = nblk * MOE_BLOCK
    counts = counts.astype(jnp.int32)
    padded = (counts + MOE_BLOCK - 1) // MOE_BLOCK * MOE_BLOCK
    pad_end = jnp.cumsum(padded)
    pad_start = pad_end - padded
    experts = jnp.arange(N_EXPERTS, dtype=jnp.int32)
    idx4 = top_idx[:, :TOP_K]
    base = jnp.sum(jnp.where(idx4[:, :, None] == experts[None, None, :], pad_start[None, None, :], 0), axis=-1)
    slot_flat = (base + rank[:, :TOP_K]).reshape(n_assign).astype(jnp.int32)
    n_used = (pad_end[-1] // MOE_BLOCK).astype(jnp.int32)
    blk_id = jnp.minimum(jnp.arange(nblk, dtype=jnp.int32), n_used - 1)
    block_expert = jnp.minimum(
        jnp.sum((pad_end[None, :] <= (blk_id * MOE_BLOCK)[:, None]).astype(jnp.int32), axis=1), N_EXPERTS - 1)
    xs_slots = _dispatch(h2, slot_flat, (pad_start + counts).astype(jnp.int32),
                         (padded - counts).astype(jnp.int32), n_slots, geo)
    y = _experts(xs_slots, block_expert.astype(jnp.int32), n_used.reshape(1), layer, wgu, bgu, wd, bd)
    return _combine(y, slot_flat, gates, xs, mod_l, geo, final_g)


def _with_identity_rows(cos, sin, tm):
    ones = jnp.ones((tm, cos.shape[1]), F32)
    return jnp.concatenate([cos, ones], axis=0), jnp.concatenate([sin, 0.0 * ones], axis=0)


def _axial_tables(seq, tm, pad_to_lanes):
    pos = jnp.arange(seq)
    rows = (pos // GRID_W).astype(F32)
    cols = (pos % GRID_W).astype(F32)
    half = DA_HEAD_DIM // 2
    inv = ROPE_BASE ** (-jnp.arange(0, half, 2, dtype=F32) / half)
    ar = rows[:, None] * inv[None, :]
    ac = cols[:, None] * inv[None, :]
    cos64 = jnp.concatenate([jnp.cos(ar), jnp.cos(ar), jnp.cos(ac), jnp.cos(ac)], axis=1)
    sin64 = jnp.concatenate([-jnp.sin(ar), jnp.sin(ar), -jnp.sin(ac), jnp.sin(ac)], axis=1)
    if pad_to_lanes:
        cos = jnp.concatenate([cos64, jnp.ones_like(cos64)], axis=1)
        sin = jnp.concatenate([sin64, jnp.zeros_like(sin64)], axis=1)
    else:
        cos = jnp.concatenate([cos64, cos64], axis=1)
        sin = jnp.concatenate([sin64, sin64], axis=1)
    return _with_identity_rows(cos, sin, tm)


def _ret_tables(seq, tm):
    inv = ROPE_BASE ** (-jnp.arange(0, RET_KDIM, 2, dtype=F32) / RET_KDIM)
    ang = jnp.arange(seq).astype(F32)[:, None] * inv[None, :]
    return _with_identity_rows(jnp.cos(ang), jnp.sin(ang), tm)


def _mla_weights(w_in, w_q_up, w_kv_up):
    d = w_in.shape[0]
    kvw = MLA_KV_RANK + MLA_ROPE
    w_in_r = jnp.concatenate([w_in[:, :MLA_KV_RANK], w_in[:, kvw:], w_in[:, MLA_KV_RANK:kvw],
                              jnp.zeros((d, LANES - MLA_ROPE), w_in.dtype)], axis=1)
    wq = w_q_up.reshape(MLA_Q_RANK, MLA_HEADS, MLA_NOPE + MLA_ROPE)
    wq = jnp.concatenate([wq, jnp.zeros((MLA_Q_RANK, MLA_HEADS, LANES - MLA_ROPE), wq.dtype)], axis=2)
    wq = wq.reshape(MLA_Q_RANK, MLA_HEADS * 2 * LANES)
    wkv = w_kv_up.reshape(MLA_KV_RANK, MLA_HEADS, MLA_NOPE + MLA_V)
    wkv = jnp.concatenate([wkv[:, :, :MLA_NOPE].reshape(MLA_KV_RANK, -1),
                           wkv[:, :, MLA_NOPE:].reshape(MLA_KV_RANK, -1)], axis=1)
    return w_in_r.astype(BF16), wq.astype(BF16), wkv.astype(BF16)


def kernel(x, c, ctx, c_ctx, ada_w, ada_b, norm_g, final_g, da_w_in, da_w_out, da_lambda, da_subln_g,
           ret_w_in, ret_decay_logit, ret_w_out, mla_w_in, mla_q_norm_g, mla_w_q_up, mla_kv_norm_g,
           mla_w_kv_up, mla_w_out, moe_w_router, moe_b_router, moe_w_gate_up, moe_b_gate_up,
           moe_w_down, moe_b_down):
    nb, seq, d = x.shape
    nctx_per = ctx.shape[1]
    depth = ada_w.shape[0]
    nlat = nb * seq
    tm = _pick_tile(512, seq, nb * nctx_per)
    geo = dict(b=nb, seq=seq, ctx=nctx_per, nlat=nlat, tm=tm, per=seq // tm, nlt=nlat // tm)
    assert nb < MOD_ROWS and nlat % nctx_per == 0 and nctx_per % LANES == 0

    xs = jnp.concatenate([x.reshape(nlat, d), ctx.reshape(nb * nctx_per, d)], axis=0).astype(F32)
    cond = jnp.zeros((MOD_ROWS, d), F32).at[:nb].set(c).at[nb].set(c_ctx)
    mod = _adaln(cond, ada_w, ada_b).reshape(depth, MOD_ROWS, 6, d)

    cos_a, sin_a = _axial_tables(seq, tm, pad_to_lanes=False)
    cos_m, sin_m = _axial_tables(seq, tm, pad_to_lanes=True)
    cos_r, sin_r = _ret_tables(seq, tm)

    for i in range(depth):
        kind = i % N_MIXERS
        j = i // N_MIXERS
        mod_l = mod[i]
        g1 = norm_g[i, 0].reshape(1, d)
        g2 = norm_g[i, 1].reshape(1, d)
        if kind == 0:
            lam_init = 0.8 - 0.6 * math.exp(-0.3 * i)
            p = _proj(xs, g1, mod_l, da_w_in[j].astype(BF16), cos_a, sin_a,
                      ["rope16", "plain", "rope16"], geo)
            o_lat, o_ctx = _da_attention(p, da_lambda[j].astype(F32),
                                         da_subln_g[j].reshape(1, DA_V_DIM).astype(F32), lam_init, geo)
            ctx_off = 0
            w_out = da_w_out[j]
        elif kind == 1:
            p = _proj(xs, g1, mod_l, ret_w_in[j].astype(BF16), cos_r, sin_r,
                      ["ret_k", "plain", "plain", "ret_q", "plain", "plain"], geo,
                      ret_scale=RET_KDIM ** -0.5)
            o_lat = o_ctx = _retention(p, ret_decay_logit[j], geo)
            ctx_off = geo["nlt"]
            w_out = ret_w_out[j]
        else:
            w_in_r, wq, wkv = _mla_weights(mla_w_in[j], mla_w_q_up[j], mla_w_kv_up[j])
            q, kn, v, kr = _mla_proj(xs, g1, mod_l, w_in_r, mla_q_norm_g[j].reshape(1, -1).astype(F32),
                                     mla_kv_norm_g[j].reshape(1, -1).astype(F32), wq, wkv, cos_m, sin_m, geo)
            o_lat, o_ctx = _mla_attention(q, kn, v, kr, geo)
            ctx_off = 0
            w_out = mla_w_out[j]
        xs, h2, top_idx, gates, rank, counts = _post(o_lat, o_ctx, ctx_off, w_out.astype(BF16), xs, mod_l, g2,
                                                     moe_w_router[i], moe_b_router[i], geo)
        fg = final_g.reshape(1, d).astype(F32) if i == depth - 1 else None
        xs = _moe(h2, top_idx, gates, rank, counts[0, :N_EXPERTS], xs, mod_l, i,
                  moe_w_gate_up, moe_b_gate_up, moe_w_down, moe_b_down, geo, fg)
    return xs.reshape(nb, seq, d)
```

```python
import functools
import math

import jax
import jax.numpy as jnp
from jax import lax
from jax.experimental import pallas as pl
from jax.experimental.pallas import tpu as pltpu

F32 = jnp.float32
BF16 = jnp.bfloat16

D_MODEL = 1024
GRID_W = 64
RMS_EPS = 1e-6
ROPE_BASE = 10000.0
N_MIXERS = 3

DA_HEADS = 8
DA_HEAD_DIM = 64
DA_V_DIM = 128

RET_HEADS = 4
RET_KDIM = 256
RET_VDIM = 512
RET_STEP = 256

MLA_HEADS = 8
MLA_NOPE = 128
MLA_ROPE = 64
MLA_V = 128
MLA_Q_RANK = 256
MLA_KV_RANK = 128

N_EXPERTS = 32
TOP_K = 4
D_EXPERT = 1024
SWIGLU_LIMIT = 7.0
SWIGLU_ALPHA = 1.702

LANES = 128
MOD_ROWS = 16
MOE_BLOCK = 512
COL_CHUNK = 1024
ATTN_TQ = 1024
ATTN_TK = 1024
ATTN_Q_SUB = 2
ATTN_ROW_SPLIT = 8
VMEM_LIMIT = 56 * 1024 * 1024
NEG_BIG = -1e30


def _params(n_axes):
    return pltpu.CompilerParams(dimension_semantics=("arbitrary",) * n_axes,
                                vmem_limit_bytes=VMEM_LIMIT)


def _pick_tile(cap, *dims):
    t = cap
    while any(d % t for d in dims):
        t //= 2
    return t


def _rms(x):
    return x * lax.rsqrt(jnp.mean(x * x, axis=-1, keepdims=True) + RMS_EPS)


def _norm_mod(x, g, shift, scale):
    return (_rms(x) * g) * (1.0 + scale) + shift


def _dot(a, b):
    return jnp.dot(a, b, preferred_element_type=F32)


def _dot_split(a, b_ref):
    half = b_ref.shape[1] // 2
    return jnp.concatenate([_dot(a, b_ref[:, :half]), _dot(a, b_ref[:, half:])], axis=1)


def _dot_nt(a, b):
    return lax.dot_general(a, b, (((1,), (1,)), ((), ())), preferred_element_type=F32)


def _dot_tn(a, b):
    return lax.dot_general(a, b, (((0,), (0,)), ((), ())), preferred_element_type=F32)


def _adaln_kernel(c_ref, w_ref, b_ref, o_ref):
    c = c_ref[...]
    a = c * (1.0 / (1.0 + jnp.exp(-c)))
    o_ref[...] = jnp.dot(a, w_ref[...], precision=lax.Precision.HIGHEST,
                         preferred_element_type=F32) + b_ref[...]


def _adaln(cond, ada_w, ada_b):
    depth, d, n = ada_w.shape
    tn = COL_CHUNK
    return pl.pallas_call(
        _adaln_kernel,
        grid=(depth, n // tn),
        in_specs=[pl.BlockSpec((MOD_ROWS, d), lambda l, j: (0, 0)),
                  pl.BlockSpec((None, d, tn), lambda l, j: (l, 0, j)),
                  pl.BlockSpec((None, 1, tn), lambda l, j: (l, 0, j))],
        out_specs=pl.BlockSpec((None, MOD_ROWS, tn), lambda l, j: (l, 0, j)),
        out_shape=jax.ShapeDtypeStruct((depth, MOD_ROWS, n), F32),
        compiler_params=_params(2),
        name="adaln",
    )(cond, ada_w, ada_b.reshape(depth, 1, n))


def _rope_pairs16(x, cos, sin):
    lane = lax.broadcasted_iota(jnp.int32, x.shape, 1)
    first = (lane % 32) < 16
    partner = jnp.where(first, pltpu.roll(x, LANES - 16, 1), pltpu.roll(x, 16, 1))
    return x * cos + partner * sin


def _proj_kernel(x_ref, g_ref, mod_ref, w_ref, cos_ref, sin_ref, o_ref, *, kinds, ret_scale):
    h = _norm_mod(x_ref[...], g_ref[...], mod_ref[0:1, :], mod_ref[1:2, :]).astype(BF16)
    cos = cos_ref[...]
    sin = sin_ref[...]
    half = COL_CHUNK // 2
    for c, kind in enumerate(kinds):
        base = c * COL_CHUNK
        a = jnp.concatenate([_dot(h, w_ref[:, base:base + half]),
                             _dot(h, w_ref[:, base + half:base + COL_CHUNK])], axis=1)
        if kind == "plain":
            o_ref[:, base:base + COL_CHUNK] = a.astype(BF16)
        elif kind == "rope16":
            for g in range(COL_CHUNK // LANES):
                sl = slice(g * LANES, (g + 1) * LANES)
                o_ref[:, base + g * LANES:base + (g + 1) * LANES] = _rope_pairs16(a[:, sl], cos, sin).astype(BF16)
        else:
            scale = ret_scale if kind == "ret_k" else 1.0
            for hh in range(COL_CHUNK // RET_KDIM):
                lo = hh * RET_KDIM
                x1 = a[:, lo:lo + LANES]
                x2 = a[:, lo + LANES:lo + RET_KDIM]
                o_ref[:, base + lo:base + lo + LANES] = ((x1 * cos - x2 * sin) * scale).astype(BF16)
                o_ref[:, base + lo + LANES:base + lo + RET_KDIM] = ((x1 * sin + x2 * cos) * scale).astype(BF16)


def _proj(xs, g, mod_l, w, cos_t, sin_t, kinds, geo, ret_scale=1.0):
    nt, d = xs.shape
    n = w.shape[1]
    tm, per, nlt, nb = geo["tm"], geo["per"], geo["nlt"], geo["b"]
    assert n == COL_CHUNK * len(kinds)
    tab = pl.BlockSpec((tm, LANES), lambda i: (jnp.where(i < nlt, i % per, per), 0))
    return pl.pallas_call(
        functools.partial(_proj_kernel, kinds=tuple(kinds), ret_scale=ret_scale),
        grid=(nt // tm,),
        in_specs=[pl.BlockSpec((tm, d), lambda i: (i, 0)),
                  pl.BlockSpec((1, d), lambda i: (0, 0)),
                  pl.BlockSpec((None, 6, d), lambda i: (jnp.minimum(i // per, nb), 0, 0)),
                  pl.BlockSpec((d, n), lambda i: (0, 0)),
                  tab, tab],
        out_specs=pl.BlockSpec((tm, n), lambda i: (i, 0)),
        out_shape=jax.ShapeDtypeStruct((nt, n), BF16),
        compiler_params=_params(1),
        name="proj",
    )(xs, g, mod_l, w, cos_t, sin_t)


def _mla_proj_kernel(x_ref, g_ref, mod_ref, win_ref, qg_ref, kvg_ref, wq_ref, wkv_ref, cos_ref, sin_ref,
                     q_ref, kn_ref, v_ref, kr_ref):
    h = _norm_mod(x_ref[...], g_ref[...], mod_ref[0:1, :], mod_ref[1:2, :]).astype(BF16)
    p = _dot_split(h, win_ref)
    cos = cos_ref[...]
    sin = sin_ref[...]
    ckv = (_rms(p[:, :MLA_KV_RANK]) * kvg_ref[...]).astype(BF16)
    kv = _dot(ckv, wkv_ref[...])
    nk = MLA_HEADS * MLA_NOPE
    kn_ref[...] = kv[:, :nk].astype(BF16)
    v_ref[...] = kv[:, nk:].astype(BF16)
    kr_ref[...] = _rope_pairs16(p[:, MLA_KV_RANK + MLA_Q_RANK:], cos, sin).astype(BF16)
    cq = (_rms(p[:, MLA_KV_RANK:MLA_KV_RANK + MLA_Q_RANK]) * qg_ref[...]).astype(BF16)
    q = _dot(cq, wq_ref[...])
    for hh in range(MLA_HEADS):
        lo = slice(hh * 2 * LANES, hh * 2 * LANES + LANES)
        hi = slice(hh * 2 * LANES + LANES, (hh + 1) * 2 * LANES)
        q_ref[:, lo] = q[:, lo].astype(BF16)
        q_ref[:, hi] = _rope_pairs16(q[:, hi], cos, sin).astype(BF16)


def _mla_proj(xs, g, mod_l, w_in, qg, kvg, wq, wkv, cos_t, sin_t, geo):
    nt, d = xs.shape
    tm, per, nlt, nb = geo["tm"], geo["per"], geo["nlt"], geo["b"]
    full = lambda a: pl.BlockSpec(a.shape, lambda i: (0,) * a.ndim)
    tab = pl.BlockSpec((tm, LANES), lambda i: (jnp.where(i < nlt, i % per, per), 0))
    nq = MLA_HEADS * 2 * LANES
    nk = MLA_HEADS * MLA_NOPE
    return pl.pallas_call(
        _mla_proj_kernel,
        grid=(nt // tm,),
        in_specs=[pl.BlockSpec((tm, d), lambda i: (i, 0)), full(g),
                  pl.BlockSpec((None, 6, d), lambda i: (jnp.minimum(i // per, nb), 0, 0)),
                  full(w_in), full(qg), full(kvg), full(wq), full(wkv), tab, tab],
        out_specs=[pl.BlockSpec((tm, nq), lambda i: (i, 0)),
                   pl.BlockSpec((tm, nk), lambda i: (i, 0)),
                   pl.BlockSpec((tm, nk), lambda i: (i, 0)),
                   pl.BlockSpec((tm, LANES), lambda i: (i, 0))],
        out_shape=[jax.ShapeDtypeStruct((nt, nq), BF16), jax.ShapeDtypeStruct((nt, nk), BF16),
                   jax.ShapeDtypeStruct((nt, nk), BF16), jax.ShapeDtypeStruct((nt, LANES), BF16)],
        compiler_params=_params(1),
        name="mla_proj",
    )(xs, g, mod_l, w_in, qg, kvg, wq, wkv, cos_t, sin_t)


LOG2E = 1.4426950408889634


def _flash_init(m_ref, a_ref):
    m_ref[...] = jnp.full(m_ref.shape, -jnp.inf, F32)
    a_ref[...] = jnp.zeros(a_ref.shape, F32)


def _ones_column(rows):
    lane = lax.broadcasted_iota(jnp.int32, (rows, LANES), 1)
    return jnp.where(lane == 0, 1.0, 0.0).astype(BF16)


def _chunk_rows(j, tk):
    if isinstance(j, int):
        return pl.ds(j * tk, tk)
    return pl.ds(pl.multiple_of(j * tk, tk), tk)


def _softmax_pv(parts, stats):
    for mp, (m_ref, a_ref) in enumerate(stats):
        tq = m_ref.shape[0]
        rb = tq // ATTN_ROW_SPLIT if tq % ATTN_ROW_SPLIT == 0 else tq
        for r0 in range(0, tq, rb):
            rows = slice(r0, r0 + rb)
            tiles = [scores[mp][rows, :] for scores, _ in parts]
            m_prev = m_ref[rows, :]
            m_new = m_prev
            for s2 in tiles:
                m_new = jnp.maximum(m_new, jnp.max(s2, axis=1, keepdims=True))
            acc = jnp.exp2(m_prev - m_new) * a_ref[rows, :]
            for s2, (_, v_aug) in zip(tiles, parts):
                acc = acc + _dot(jnp.exp2(s2 - m_new).astype(BF16), v_aug)
            a_ref[rows, :] = acc
            m_ref[rows, :] = m_new


def _flash_lat(qk_fns, stats, k_ctx, v_ctx, k_at, v_at, bufs, n_chunks):
    s_a, s_b, s_c = bufs

    def qk_into(dst, k):
        for mp, f in enumerate(qk_fns):
            dst[mp] = f(k)

    qk_into(s_c, k_ctx)
    qk_into(s_a, k_at(0))
    qk_into(s_b, k_at(1))
    _softmax_pv([(s_c, v_ctx), (s_a, v_at(0))], stats)

    def body(t, carry):
        j = 2 * t + 1
        qk_into(s_a, k_at(j + 1))
        _softmax_pv([(s_b, v_at(j))], stats)
        qk_into(s_b, k_at(j + 2))
        _softmax_pv([(s_a, v_at(j + 1))], stats)
        return carry

    lax.fori_loop(0, (n_chunks - 2) // 2, body, 0)
    _softmax_pv([(s_b, v_at(n_chunks - 1))], stats)


def _flash_result(a_ref, vdim):
    a = a_ref[...]
    return a[:, :vdim] / a[:, vdim:vdim + 1]


def _da_attn_kernel(*refs, n_chunks, tk, tq, lam_init, scale):
    if n_chunks:
        (lam_ref, sg_ref, q_ref, kc_ref, vc_ref, kl_ref, vl_ref, o_ref, m0, a0, m1, a1, *bufs) = refs
    else:
        (lam_ref, sg_ref, q_ref, kc_ref, vc_ref, o_ref, m0, a0, m1, a1) = refs
    c2 = scale * LOG2E
    v_ctx = jnp.concatenate([vc_ref[...], _ones_column(vc_ref.shape[0])], axis=1)
    lf = lam_ref[...]
    lam = (jnp.exp(jnp.sum(lf[0:1] * lf[1:2], axis=1, keepdims=True))
           - jnp.exp(jnp.sum(lf[2:3] * lf[3:4], axis=1, keepdims=True)) + lam_init)

    for r0 in range(0, q_ref.shape[0], tq):
        rows = pl.ds(r0, tq)
        q = q_ref[rows, :]
        lane = lax.broadcasted_iota(jnp.int32, q.shape, 1)
        zero = jnp.zeros_like(q)
        q_lo = jnp.where(lane < DA_HEAD_DIM, q, zero)
        q_hi = jnp.where(lane >= DA_HEAD_DIM, q, zero)
        stats = [(m0.at[rows], a0.at[rows]), (m1.at[rows], a1.at[rows])]
        for m_ref, a_ref in stats:
            _flash_init(m_ref, a_ref)
        qk_fns = [lambda k, ql=q_lo: _dot_nt(ql, k) * c2, lambda k, qh=q_hi: _dot_nt(qh, k) * c2]
        if n_chunks:
            ones = _ones_column(tk)

            def k_at(j):
                return kl_ref[_chunk_rows(j, tk), :]

            def v_at(j):
                return jnp.concatenate([vl_ref[_chunk_rows(j, tk), :], ones], axis=1)

            _flash_lat(qk_fns, stats, kc_ref[...], v_ctx, k_at, v_at, bufs, n_chunks)
        else:
            _softmax_pv([([f(kc_ref[...]) for f in qk_fns], v_ctx)], stats)
        o = _flash_result(stats[0][1], DA_V_DIM) - lam * _flash_result(stats[1][1], DA_V_DIM)
        o = _rms(o) * sg_ref[...] * (1.0 - lam_init)
        o_ref[rows, :] = o.astype(BF16)


def _attn_scratch(q_rows, vdim, n_maps, tq=0, tk=0, ctx=0):
    s = []
    for _ in range(n_maps):
        s += [pltpu.VMEM((q_rows, 1), F32), pltpu.VMEM((q_rows, vdim + LANES), F32)]
    if tk:
        s += [pltpu.VMEM((n_maps, tq, tk), F32), pltpu.VMEM((n_maps, tq, tk), F32),
              pltpu.VMEM((n_maps, tq, ctx), F32)]
    return s


def _da_attention(p, lam_vecs, subg, lam_init, geo, need_ctx):
    nt = p.shape[0]
    nb, seq, ctx, nlat = geo["b"], geo["seq"], geo["ctx"], geo["nlat"]
    hh = DA_HEADS
    tq = _pick_tile(ATTN_TQ, seq)
    tk = _pick_tile(min(ATTN_TK, seq // 2), seq)
    assert (seq // tk) % 2 == 0
    tqb = tq * ATTN_Q_SUB if seq % (tq * ATTN_Q_SUB) == 0 else tq
    nqt = seq // tqb
    cb = nlat // ctx
    scale = DA_HEAD_DIM ** -0.5
    small = [pl.BlockSpec(lam_vecs.shape, lambda *a: (0, 0)), pl.BlockSpec(subg.shape, lambda *a: (0, 0))]
    o_lat = pl.pallas_call(
        functools.partial(_da_attn_kernel, n_chunks=seq // tk, tk=tk, tq=tq, lam_init=lam_init, scale=scale),
        grid=(nb, hh, nqt),
        in_specs=small + [
            pl.BlockSpec((tqb, LANES), lambda b, h, i: (b * nqt + i, 2 * hh + h)),
            pl.BlockSpec((ctx, LANES), lambda b, h, i: (cb + b, h)),
            pl.BlockSpec((ctx, LANES), lambda b, h, i: (cb + b, hh + h)),
            pl.BlockSpec((seq, LANES), lambda b, h, i: (b, h)),
            pl.BlockSpec((seq, LANES), lambda b, h, i: (b, hh + h))],
        out_specs=pl.BlockSpec((tqb, LANES), lambda b, h, i: (b * nqt + i, h)),
        out_shape=jax.ShapeDtypeStruct((nlat, hh * DA_V_DIM), BF16),
        scratch_shapes=_attn_scratch(tqb, DA_V_DIM, 2, tq, tk, ctx),
        compiler_params=_params(3),
        name="da_attn_lat",
    )(lam_vecs, subg, p, p, p, p, p)
    if not need_ctx:
        return o_lat, o_lat
    o_ctx = pl.pallas_call(
        functools.partial(_da_attn_kernel, n_chunks=0, tk=tk, tq=ctx, lam_init=lam_init, scale=scale),
        grid=(nb, hh),
        in_specs=small + [
            pl.BlockSpec((ctx, LANES), lambda b, h: (cb + b, 2 * hh + h)),
            pl.BlockSpec((ctx, LANES), lambda b, h: (cb + b, h)),
            pl.BlockSpec((ctx, LANES), lambda b, h: (cb + b, hh + h))],
        out_specs=pl.BlockSpec((ctx, LANES), lambda b, h: (b, h)),
        out_shape=jax.ShapeDtypeStruct((nt - nlat, hh * DA_V_DIM), BF16),
        scratch_shapes=_attn_scratch(ctx, DA_V_DIM, 2),
        compiler_params=_params(2),
        name="da_attn_ctx",
    )(lam_vecs, subg, p, p, p)
    return o_lat, o_ctx


def _mla_attn_kernel(*refs, n_chunks, tk, tq, scale):
    if n_chunks:
        (q_ref, knc_ref, krc_ref, vc_ref, knl_ref, krl_ref, vl_ref, o_ref, m0, a0, *bufs) = refs
    else:
        (q_ref, knc_ref, krc_ref, vc_ref, o_ref, m0, a0) = refs
    c2 = scale * LOG2E
    k_ctx = jnp.concatenate([knc_ref[...], krc_ref[...]], axis=1)
    v_ctx = jnp.concatenate([vc_ref[...], _ones_column(vc_ref.shape[0])], axis=1)
    for r0 in range(0, q_ref.shape[0], tq):
        rows = pl.ds(r0, tq)
        q = q_ref[rows, :]
        stats = [(m0.at[rows], a0.at[rows])]
        _flash_init(*stats[0])
        qk_fns = [lambda k, q=q: _dot_nt(q, k) * c2]
        if n_chunks:
            ones = _ones_column(tk)

            def k_at(j):
                kr = _chunk_rows(j, tk)
                return jnp.concatenate([knl_ref[kr, :], krl_ref[kr, :]], axis=1)

            def v_at(j):
                return jnp.concatenate([vl_ref[_chunk_rows(j, tk), :], ones], axis=1)

            _flash_lat(qk_fns, stats, k_ctx, v_ctx, k_at, v_at, bufs, n_chunks)
        else:
            _softmax_pv([([qk_fns[0](k_ctx)], v_ctx)], stats)
        o_ref[rows, :] = _flash_result(stats[0][1], MLA_V).astype(BF16)


def _mla_attention(q, kn, v, kr, geo, need_ctx):
    nt = q.shape[0]
    nb, seq, ctx, nlat = geo["b"], geo["seq"], geo["ctx"], geo["nlat"]
    hh = MLA_HEADS
    tq = _pick_tile(ATTN_TQ, seq)
    tk = _pick_tile(min(ATTN_TK, seq // 2), seq)
    assert (seq // tk) % 2 == 0
    tqb = tq * ATTN_Q_SUB if seq % (tq * ATTN_Q_SUB) == 0 else tq
    nqt = seq // tqb
    cb = nlat // ctx
    scale = (MLA_NOPE + MLA_ROPE) ** -0.5
    o_lat = pl.pallas_call(
        functools.partial(_mla_attn_kernel, n_chunks=seq // tk, tk=tk, tq=tq, scale=scale),
        grid=(nb, hh, nqt),
        in_specs=[
            pl.BlockSpec((tqb, 2 * LANES), lambda b, h, i: (b * nqt + i, h)),
            pl.BlockSpec((ctx, LANES), lambda b, h, i: (cb + b, h)),
            pl.BlockSpec((ctx, LANES), lambda b, h, i: (cb + b, 0)),
            pl.BlockSpec((ctx, LANES), lambda b, h, i: (cb + b, h)),
            pl.BlockSpec((seq, LANES), lambda b, h, i: (b, h)),
            pl.BlockSpec((seq, LANES), lambda b, h, i: (b, 0)),
            pl.BlockSpec((seq, LANES), lambda b, h, i: (b, h))],
        out_specs=pl.BlockSpec((tqb, LANES), lambda b, h, i: (b * nqt + i, h)),
        out_shape=jax.ShapeDtypeStruct((nlat, hh * MLA_V), BF16),
        scratch_shapes=_attn_scratch(tqb, MLA_V, 1, tq, tk, ctx),
        compiler_params=_params(3),
        name="mla_attn_lat",
    )(q, kn, kr, v, kn, kr, v)
    if not need_ctx:
        return o_lat, o_lat
    o_ctx = pl.pallas_call(
        functools.partial(_mla_attn_kernel, n_chunks=0, tk=tk, tq=ctx, scale=scale),
        grid=(nb, hh),
        in_specs=[
            pl.BlockSpec((ctx, 2 * LANES), lambda b, h: (cb + b, h)),
            pl.BlockSpec((ctx, LANES), lambda b, h: (cb + b, h)),
            pl.BlockSpec((ctx, LANES), lambda b, h: (cb + b, 0)),
            pl.BlockSpec((ctx, LANES), lambda b, h: (cb + b, h))],
        out_specs=pl.BlockSpec((ctx, LANES), lambda b, h: (b, h)),
        out_shape=jax.ShapeDtypeStruct((nt - nlat, hh * MLA_V), BF16),
        scratch_shapes=_attn_scratch(ctx, MLA_V, 1),
        compiler_params=_params(2),
        name="mla_attn_ctx",
    )(q, kn, kr, v)
    return o_lat, o_ctx


def _ret_kernel(*refs, reverse, c):
    if reverse:
        dl_ref, q_ref, k_ref, v0_ref, v1_ref, of_ref, g0_ref, g1_ref, o_ref, st = refs
    else:
        dl_ref, q_ref, k_ref, v0_ref, v1_ref, o_ref, st = refs
    s = pl.program_id(1)

    @pl.when(s == 0)
    def _():
        st[...] = jnp.zeros(st.shape, F32)

    dl = dl_ref[...]
    lsig = jnp.minimum(dl, 0.0) - jnp.log(1.0 + jnp.exp(-jnp.abs(dl)))
    row = lax.broadcasted_iota(jnp.int32, dl.shape, 0)
    col = lax.broadcasted_iota(jnp.int32, dl.shape, 1)
    i2 = lax.broadcasted_iota(jnp.int32, (c, c), 0)
    j2 = lax.broadcasted_iota(jnp.int32, (c, c), 1)
    pos = lax.broadcasted_iota(jnp.int32, (c, 1), 0).astype(F32)
    dist = ((j2 - i2) if reverse else (i2 - j2)).astype(F32)
    keep = dist >= 0.0
    dist = jnp.where(keep, dist, 0.0)
    per_blk = COL_CHUNK // RET_VDIM

    for hd in range(RET_HEADS):
        pick = jnp.logical_and(row == (1 if reverse else 0), col == hd)
        lg = jnp.sum(jnp.sum(jnp.where(pick, lsig, 0.0), axis=1, keepdims=True), axis=0, keepdims=True)
        if reverse:
            q_dec = jnp.exp((c - pos) * lg)
            k_dec = jnp.exp(pos * lg)
        else:
            q_dec = jnp.exp((pos + 1.0) * lg)
            k_dec = jnp.exp((c - 1.0 - pos) * lg)
        intra = jnp.where(keep, jnp.exp(dist * lg), 0.0)
        chunk_dec = jnp.exp(c * lg)

        ksl = slice(hd * RET_KDIM, (hd + 1) * RET_KDIM)
        vsl = slice((hd % per_blk) * RET_VDIM, (hd % per_blk + 1) * RET_VDIM)
        osl = slice(hd * RET_VDIM, (hd + 1) * RET_VDIM)
        qb = q_ref[:, ksl]
        kb = k_ref[:, ksl]
        vb = (v0_ref if hd < per_blk else v1_ref)[:, vsl]
        state = st[hd]
        sc = _dot_nt(qb, kb) * intra
        o = _dot(sc.astype(BF16), vb) + _dot((qb.astype(F32) * q_dec).astype(BF16), state.astype(BF16))
        st[hd] = chunk_dec * state + _dot_tn((kb.astype(F32) * k_dec).astype(BF16), vb)
        if reverse:
            tot = _rms(of_ref[:, osl] + o)
            g = (g0_ref if hd < per_blk else g1_ref)[:, vsl].astype(F32)
            o_ref[:, osl] = (g * (1.0 / (1.0 + jnp.exp(-g))) * tot).astype(BF16)
        else:
            o_ref[:, osl] = o


def _retention(p, decay_logit, geo):
    nt = p.shape[0]
    nb, seq, ctx, nlat = geo["b"], geo["seq"], geo["ctx"], geo["nlat"]
    c = _pick_tile(RET_STEP, ctx, seq)
    ncc, ncl = ctx // c, seq // c
    hh = RET_HEADS
    kq, kv = RET_KDIM, RET_VDIM
    wb = COL_CHUNK
    assert hh * kq == wb and hh * kv == 2 * wb
    dl = jnp.zeros((8, LANES), F32).at[:2, :hh].set(decay_logit.astype(F32))

    def rows_fwd(b, s):
        return jnp.where(s < ncc, nlat // c + b * ncc + s, b * ncl + (s - ncc))

    def rows_bwd(b, s):
        return jnp.where(s < ncc, nlat // c + b * ncc + (ncc - 1 - s), b * ncl + (ncl - 1 - (s - ncc)))

    def blk(rows, col):
        return pl.BlockSpec((c, wb), lambda b, s: (rows(b, s), col))

    def specs(rows):
        return [pl.BlockSpec((8, LANES), lambda b, s: (0, 0)), blk(rows, 3), blk(rows, 0), blk(rows, 1), blk(rows, 2)]

    grid = (nb, ncc + ncl)
    scratch = [pltpu.VMEM((hh, kq, kv), F32)]
    o_f = pl.pallas_call(
        functools.partial(_ret_kernel, reverse=False, c=c),
        grid=grid,
        in_specs=specs(rows_fwd),
        out_specs=pl.BlockSpec((c, hh * kv), lambda b, s: (rows_fwd(b, s), 0)),
        out_shape=jax.ShapeDtypeStruct((nt, hh * kv), F32),
        scratch_shapes=scratch,
        compiler_params=_params(2),
        name="ret_fwd",
    )(dl, p, p, p, p)
    return pl.pallas_call(
        functools.partial(_ret_kernel, reverse=True, c=c),
        grid=grid,
        in_specs=specs(rows_bwd) + [
            pl.BlockSpec((c, hh * kv), lambda b, s: (rows_bwd(b, s), 0)), blk(rows_bwd, 4), blk(rows_bwd, 5)],
        out_specs=pl.BlockSpec((c, hh * kv), lambda b, s: (rows_bwd(b, s), 0)),
        out_shape=jax.ShapeDtypeStruct((nt, hh * kv), BF16),
        scratch_shapes=scratch,
        compiler_params=_params(2),
        name="ret_bwd",
    )(dl, p, p, p, p, o_f, p, p)


def _post_kernel(ol_ref, oc_ref, w_ref, x_ref, mod_ref, g2_ref, wr_ref, br_ref, tri_ref,
                 xo_ref, h2_ref, idx_ref, gate_ref, rank_ref, cnt_ref, cnt_scr, *, nlt):
    t = pl.program_id(0)

    @pl.when(t == 0)
    def _():
        cnt_scr[...] = jnp.zeros(cnt_scr.shape, F32)

    o = jnp.where(t < nlt, ol_ref[...], oc_ref[...])
    y = _dot_split(o, w_ref)
    x = x_ref[...] + mod_ref[2:3, :] * y
    xo_ref[...] = x
    h2 = _norm_mod(x, g2_ref[...], mod_ref[3:4, :], mod_ref[4:5, :])
    _store_rows_as_tiles(h2_ref, h2)
    h_top = pltpu.bitcast(pltpu.bitcast(h2, jnp.uint32) & jnp.uint32(0xFFFF0000), F32)
    h_hi = h_top.astype(BF16)
    h_lo = (h2 - h_top).astype(BF16)
    logits = (_dot(h_hi, wr_ref[0]) + _dot(h_lo, wr_ref[0]) + _dot(h_hi, wr_ref[1])) + br_ref[...]
    lane = lax.broadcasted_iota(jnp.int32, logits.shape, 1).astype(F32)
    work = logits
    vals, idxs = [], []
    for _ in range(TOP_K):
        mx = jnp.max(work, axis=1, keepdims=True)
        ix = jnp.min(jnp.where(work == mx, lane, float(LANES)), axis=1, keepdims=True)
        vals.append(mx)
        idxs.append(ix)
        work = jnp.where(lane == ix, -jnp.inf, work)
    es = [jnp.exp(v - vals[0]) for v in vals]
    den = es[0] + es[1] + es[2] + es[3]
    onehots = [lane == ix for ix in idxs]
    oh = jnp.zeros(logits.shape, F32)
    for m in onehots:
        oh = oh + jnp.where(m, 1.0, 0.0)
    before = _dot(tri_ref[...], oh.astype(BF16)) + cnt_scr[0:1, :]
    idx_out = jnp.zeros(logits.shape, F32)
    gate_out = jnp.zeros(logits.shape, F32)
    rank_out = jnp.zeros(logits.shape, F32)
    for r in range(TOP_K):
        sel = lane == float(r)
        rk = jnp.sum(jnp.where(onehots[r], before, 0.0), axis=1, keepdims=True)
        idx_out = jnp.where(sel, idxs[r], idx_out)
        gate_out = jnp.where(sel, es[r] / den, gate_out)
        rank_out = jnp.where(sel, rk, rank_out)
    idx_ref[...] = idx_out.astype(jnp.int32)
    gate_ref[...] = gate_out
    rank_ref[...] = rank_out.astype(jnp.int32)
    cnt_scr[0:1, :] = cnt_scr[0:1, :] + jnp.sum(oh, axis=0, keepdims=True)
    cnt_ref[...] = cnt_scr[...]


def _post(o_lat, o_ctx, ctx_off, w_out, xs, mod_l, g2, w_router, b_router, geo, lat_only=False):
    n_all, d = xs.shape
    nt = geo["nlat"] if lat_only else n_all
    ko = o_lat.shape[1]
    tm, per, nb, nlt = geo["tm"], geo["per"], geo["b"], geo["nlt"]
    ii = lax.broadcasted_iota(jnp.int32, (tm, tm), 0)
    jj = lax.broadcasted_iota(jnp.int32, (tm, tm), 1)
    tri = (jj < ii).astype(BF16)
    wr32 = jnp.zeros((d, LANES), F32).at[:, :N_EXPERTS].set(w_router.astype(F32))
    wr_top = lax.bitcast_convert_type(lax.bitcast_convert_type(wr32, jnp.uint32) & jnp.uint32(0xFFFF0000), F32)
    wr = jnp.stack([wr_top.astype(BF16), (wr32 - wr_top).astype(BF16)])
    br = jnp.full((1, LANES), NEG_BIG, F32).at[0, :N_EXPERTS].set(b_router.astype(F32))
    row = lambda w: pl.BlockSpec((tm, w), lambda i: (i, 0))
    full = lambda a: pl.BlockSpec(a.shape, lambda i: (0,) * a.ndim)
    return pl.pallas_call(
        functools.partial(_post_kernel, nlt=nlt),
        grid=(nt // tm,),
        in_specs=[pl.BlockSpec((tm, ko), lambda i: (jnp.minimum(i, nlt - 1), 0)),
                  pl.BlockSpec((tm, ko), lambda i: (jnp.maximum(i - nlt, 0) + ctx_off, 0)),
                  full(w_out), row(d),
                  pl.BlockSpec((None, 6, d), lambda i: (jnp.minimum(i // per, nb), 0, 0)),
                  full(g2), full(wr), full(br), full(tri)],
        out_specs=[row(d), pl.BlockSpec((tm * SUBLANES, LANES), lambda i: (i, 0)),
                   row(LANES), row(LANES), row(LANES),
                   pl.BlockSpec((8, LANES), lambda i: (0, 0))],
        out_shape=[jax.ShapeDtypeStruct((n_all, d), F32), jax.ShapeDtypeStruct((nt * SUBLANES, LANES), F32),
                   jax.ShapeDtypeStruct((nt, LANES), jnp.int32), jax.ShapeDtypeStruct((nt, LANES), F32),
                   jax.ShapeDtypeStruct((nt, LANES), jnp.int32), jax.ShapeDtypeStruct((8, LANES), F32)],
        scratch_shapes=[pltpu.VMEM((8, LANES), F32)],
        input_output_aliases={3: 0},
        compiler_params=_params(1),
        name="post",
    )(o_lat, o_ctx, w_out, xs, mod_l, g2, wr, br, tri)


SUBLANES = 8
DMA_UNROLL = 8


def _row_copy(src, dst, sem):
    return pltpu.make_async_copy(src, dst, sem)


def _store_rows_as_tiles(ref, val):
    rows = val.shape[0]
    for j in range(SUBLANES):
        ref[pl.ds(j, rows, stride=SUBLANES), :] = val[:, j * LANES:(j + 1) * LANES]


def _load_rows_from_tiles(ref):
    rows = ref.shape[0] // SUBLANES
    return jnp.concatenate([ref[pl.ds(j, rows, stride=SUBLANES), :] for j in range(SUBLANES)], axis=1)


def _tile_of(ref, row):
    if isinstance(row, int):
        return ref.at[pl.ds(row * SUBLANES, SUBLANES)]
    return ref.at[pl.ds(pl.multiple_of(row * SUBLANES, SUBLANES), SUBLANES)]


def _dispatch_kernel(zs_ref, zc_ref, slot_ref, h_ref, xs_ref, zero_scr, sem, *, tm):
    t = pl.program_id(0)

    @pl.when(t == 0)
    def _():
        zero_scr[...] = jnp.zeros(zero_scr.shape, F32)

        def per_expert(e, carry):
            start = zs_ref[e]
            n = zc_ref[e]

            def issue(r, c2):
                _row_copy(zero_scr, _tile_of(xs_ref, start + r), sem).start()
                return c2

            lax.fori_loop(0, n, issue, 0)

            def drain(r, c2):
                _row_copy(zero_scr, _tile_of(xs_ref, 0), sem).wait()
                return c2

            lax.fori_loop(0, n, drain, 0)
            return carry

        lax.fori_loop(0, N_EXPERTS, per_expert, 0)

    def issue(r, carry):
        for k in range(TOP_K):
            s = slot_ref[r * TOP_K + k]
            _row_copy(_tile_of(h_ref, r), _tile_of(xs_ref, s), sem).start(priority=k % 2)
        return carry

    lax.fori_loop(0, tm, issue, 0, unroll=DMA_UNROLL)

    def drain(r, carry):
        for k in range(TOP_K):
            _row_copy(_tile_of(h_ref, 0), _tile_of(xs_ref, 0), sem).wait()
        return carry

    lax.fori_loop(0, tm, drain, 0, unroll=DMA_UNROLL)


def _dispatch(h2, slot_flat, zstart, zcount, n_slots, geo):
    sub, lanes = SUBLANES, h2.shape[1]
    nt = h2.shape[0] // sub
    tm = geo["tm"]
    return pl.pallas_call(
        functools.partial(_dispatch_kernel, tm=tm),
        grid_spec=pltpu.PrefetchScalarGridSpec(
            num_scalar_prefetch=2,
            grid=(nt // tm,),
            in_specs=[pl.BlockSpec((tm * TOP_K,), lambda i, zs, zc: (i,), memory_space=pltpu.SMEM),
                      pl.BlockSpec((tm * sub, lanes), lambda i, zs, zc: (i, 0))],
            out_specs=pl.BlockSpec(memory_space=pl.ANY),
            scratch_shapes=[pltpu.VMEM((sub, lanes), F32), pltpu.SemaphoreType.DMA(())]),
        out_shape=jax.ShapeDtypeStruct((n_slots * sub, lanes), F32),
        compiler_params=_params(1),
        name="moe_dispatch",
    )(zstart, zcount, slot_flat, h2)


def _expert_kernel(be_ref, nu_ref, x_ref, wgu_ref, bgu_ref, wd_ref, bd_ref, y_ref, wgu_bf, wd_bf):
    i = pl.program_id(0)

    @pl.when(jnp.logical_or(i == 0, be_ref[i] != be_ref[jnp.maximum(i - 1, 0)]))
    def _():
        wgu_bf[...] = wgu_ref[...].astype(BF16)
        wd_bf[...] = wd_ref[...].astype(BF16)

    @pl.when(i < nu_ref[0])
    def _():
        x = _load_rows_from_tiles(x_ref).astype(BF16)
        gu = _dot(x, wgu_bf[...]) + bgu_ref[...]
        glu = jnp.minimum(gu[:, :D_EXPERT], SWIGLU_LIMIT)
        lin = jnp.clip(gu[:, D_EXPERT:], -SWIGLU_LIMIT, SWIGLU_LIMIT)
        act = glu * (1.0 / (1.0 + jnp.exp(-SWIGLU_ALPHA * glu))) * (lin + 1.0)
        _store_rows_as_tiles(y_ref, _dot(act.astype(BF16), wd_bf[...]) + bd_ref[...])


def _experts(xs, block_expert, n_used, layer, wgu, bgu, wd, bd):
    sub, lanes = SUBLANES, xs.shape[1]
    ns = xs.shape[0] // sub
    nblk = ns // MOE_BLOCK
    depth, ne, d, f2 = wgu.shape
    f = wd.shape[2]
    blk = lambda i, be, nu: (jnp.minimum(i, nu[0] - 1), 0)
    wsel = lambda i, be, nu: (layer, be[i], 0, 0)
    return pl.pallas_call(
        _expert_kernel,
        grid_spec=pltpu.PrefetchScalarGridSpec(
            num_scalar_prefetch=2,
            grid=(nblk,),
            in_specs=[pl.BlockSpec((MOE_BLOCK * sub, lanes), blk),
                      pl.BlockSpec((None, None, d, f2), wsel),
                      pl.BlockSpec((None, None, 1, f2), wsel),
                      pl.BlockSpec((None, None, f, d), wsel),
                      pl.BlockSpec((None, None, 1, d), wsel)],
            out_specs=pl.BlockSpec((MOE_BLOCK * sub, lanes), blk),
            scratch_shapes=[pltpu.VMEM((d, f2), BF16), pltpu.VMEM((f, d), BF16)]),
        out_shape=jax.ShapeDtypeStruct((ns * sub, lanes), F32),
        compiler_params=_params(1),
        name="moe_experts",
    )(block_expert, n_used, xs, wgu, bgu.reshape(depth, ne, 1, f2), wd, bd.reshape(depth, ne, 1, d))


def _combine_kernel(*refs, tm, final):
    if final:
        slot_ref, next_slot_ref, gate_ref, x_ref, mod_ref, fg_ref, y_ref, xo_ref, ybuf, sems = refs
    else:
        slot_ref, next_slot_ref, gate_ref, x_ref, mod_ref, y_ref, xo_ref, ybuf, sems = refs
    t = pl.program_id(0)
    cur = t % 2

    def gather(slots, buf):
        def issue(r, carry):
            for k in range(TOP_K):
                s = slots[r * TOP_K + k]
                _row_copy(_tile_of(y_ref, s), _tile_of(ybuf.at[buf, k], r), sems.at[buf]).start(priority=k % 2)
            return carry
        lax.fori_loop(0, tm, issue, 0, unroll=DMA_UNROLL)

    @pl.when(t == 0)
    def _():
        gather(slot_ref, 0)

    @pl.when(t + 1 < pl.num_programs(0))
    def _():
        gather(next_slot_ref, 1 - cur)

    def drain(r, carry):
        for k in range(TOP_K):
            _row_copy(_tile_of(y_ref, 0), _tile_of(ybuf.at[cur, k], 0), sems.at[cur]).wait()
        return carry

    lax.fori_loop(0, tm, drain, 0, unroll=DMA_UNROLL)
    gates = gate_ref[...]
    f = gates[:, 0:1] * _load_rows_from_tiles(ybuf.at[cur, 0])
    for k in range(1, TOP_K):
        f = f + gates[:, k:k + 1] * _load_rows_from_tiles(ybuf.at[cur, k])
    x = x_ref[...] + mod_ref[5:6, :] * f
    xo_ref[...] = _rms(x) * fg_ref[...] if final else x


def _combine(y, slot_flat, gates, xs, mod_l, geo, final_g=None):
    nt, d = xs.shape
    tm, per, nb = geo["tm"], geo["per"], geo["b"]
    final = final_g is not None
    n_tiles = geo["nlt"] if final else nt // tm
    last = n_tiles - 1
    extra_specs = [pl.BlockSpec((1, d), lambda i: (0, 0))] if final else []
    extra_args = [final_g] if final else []
    return pl.pallas_call(
        functools.partial(_combine_kernel, tm=tm, final=final),
        grid=(n_tiles,),
        in_specs=[pl.BlockSpec((tm * TOP_K,), lambda i: (i,), memory_space=pltpu.SMEM),
                  pl.BlockSpec((tm * TOP_K,), lambda i: (jnp.minimum(i + 1, last),), memory_space=pltpu.SMEM),
                  pl.BlockSpec((tm, LANES), lambda i: (i, 0)),
                  pl.BlockSpec((tm, d), lambda i: (i, 0)),
                  pl.BlockSpec((None, 6, d), lambda i: (jnp.minimum(i // per, nb), 0, 0))]
                 + extra_specs + [pl.BlockSpec(memory_space=pl.ANY)],
        out_specs=pl.BlockSpec((tm, d), lambda i: (i, 0)),
        out_shape=jax.ShapeDtypeStruct((n_tiles * tm, d), F32),
        scratch_shapes=[pltpu.VMEM((2, TOP_K, tm * SUBLANES, LANES), F32), pltpu.SemaphoreType.DMA((2,))],
        input_output_aliases={} if final else {3: 0},
        compiler_params=_params(1),
        name="moe_combine",
    )(slot_flat, slot_flat, gates, xs, mod_l, *extra_args, y)


def _moe(h2, top_idx, gates, rank, counts, xs, mod_l, layer, wgu, bgu, wd, bd, geo, final_g=None):
    nt = top_idx.shape[0]
    n_assign = nt * TOP_K
    nblk = n_assign // MOE_BLOCK + N_EXPERTS
    n_slots = nblk * MOE_BLOCK
    counts = counts.astype(jnp.int32)
    padded = (counts + MOE_BLOCK - 1) // MOE_BLOCK * MOE_BLOCK
    pad_end = jnp.cumsum(padded)
    pad_start = pad_end - padded
    experts = jnp.arange(N_EXPERTS, dtype=jnp.int32)
    idx4 = top_idx[:, :TOP_K]
    base = jnp.sum(jnp.where(idx4[:, :, None] == experts[None, None, :], pad_start[None, None, :], 0), axis=-1)
    slot_flat = (base + rank[:, :TOP_K]).reshape(n_assign).astype(jnp.int32)
    n_used = (pad_end[-1] // MOE_BLOCK).astype(jnp.int32)
    blk_id = jnp.minimum(jnp.arange(nblk, dtype=jnp.int32), n_used - 1)
    block_expert = jnp.minimum(
        jnp.sum((pad_end[None, :] <= (blk_id * MOE_BLOCK)[:, None]).astype(jnp.int32), axis=1), N_EXPERTS - 1)
    xs_slots = _dispatch(h2, slot_flat, (pad_start + counts).astype(jnp.int32),
                         (padded - counts).astype(jnp.int32), n_slots, geo)
    y = _experts(xs_slots, block_expert.astype(jnp.int32), n_used.reshape(1), layer, wgu, bgu, wd, bd)
    return _combine(y, slot_flat, gates, xs, mod_l, geo, final_g)


def _with_identity_rows(cos, sin, tm):
    ones = jnp.ones((tm, cos.shape[1]), F32)
    return jnp.concatenate([cos, ones], axis=0), jnp.concatenate([sin, 0.0 * ones], axis=0)


def _axial_tables(seq, tm, pad_to_lanes):
    pos = jnp.arange(seq)
    rows = (pos // GRID_W).astype(F32)
    cols = (pos % GRID_W).astype(F32)
    half = DA_HEAD_DIM // 2
    inv = ROPE_BASE ** (-jnp.arange(0, half, 2, dtype=F32) / half)
    ar = rows[:, None] * inv[None, :]
    ac = cols[:, None] * inv[None, :]
    cos64 = jnp.concatenate([jnp.cos(ar), jnp.cos(ar), jnp.cos(ac), jnp.cos(ac)], axis=1)
    sin64 = jnp.concatenate([-jnp.sin(ar), jnp.sin(ar), -jnp.sin(ac), jnp.sin(ac)], axis=1)
    if pad_to_lanes:
        cos = jnp.concatenate([cos64, jnp.ones_like(cos64)], axis=1)
        sin = jnp.concatenate([sin64, jnp.zeros_like(sin64)], axis=1)
    else:
        cos = jnp.concatenate([cos64, cos64], axis=1)
        sin = jnp.concatenate([sin64, sin64], axis=1)
    return _with_identity_rows(cos, sin, tm)


def _ret_tables(seq, tm):
    inv = ROPE_BASE ** (-jnp.arange(0, RET_KDIM, 2, dtype=F32) / RET_KDIM)
    ang = jnp.arange(seq).astype(F32)[:, None] * inv[None, :]
    return _with_identity_rows(jnp.cos(ang), jnp.sin(ang), tm)


def _mla_weights(w_in, w_q_up, w_kv_up):
    d = w_in.shape[0]
    kvw = MLA_KV_RANK + MLA_ROPE
    w_in_r = jnp.concatenate([w_in[:, :MLA_KV_RANK], w_in[:, kvw:], w_in[:, MLA_KV_RANK:kvw],
                              jnp.zeros((d, LANES - MLA_ROPE), w_in.dtype)], axis=1)
    wq = w_q_up.reshape(MLA_Q_RANK, MLA_HEADS, MLA_NOPE + MLA_ROPE)
    wq = jnp.concatenate([wq, jnp.zeros((MLA_Q_RANK, MLA_HEADS, LANES - MLA_ROPE), wq.dtype)], axis=2)
    wq = wq.reshape(MLA_Q_RANK, MLA_HEADS * 2 * LANES)
    wkv = w_kv_up.reshape(MLA_KV_RANK, MLA_HEADS, MLA_NOPE + MLA_V)
    wkv = jnp.concatenate([wkv[:, :, :MLA_NOPE].reshape(MLA_KV_RANK, -1),
                           wkv[:, :, MLA_NOPE:].reshape(MLA_KV_RANK, -1)], axis=1)
    return w_in_r.astype(BF16), wq.astype(BF16), wkv.astype(BF16)


def kernel(x, c, ctx, c_ctx, ada_w, ada_b, norm_g, final_g, da_w_in, da_w_out, da_lambda, da_subln_g,
           ret_w_in, ret_decay_logit, ret_w_out, mla_w_in, mla_q_norm_g, mla_w_q_up, mla_kv_norm_g,
           mla_w_kv_up, mla_w_out, moe_w_router, moe_b_router, moe_w_gate_up, moe_b_gate_up,
           moe_w_down, moe_b_down):
    nb, seq, d = x.shape
    nctx_per = ctx.shape[1]
    depth = ada_w.shape[0]
    nlat = nb * seq
    tm = _pick_tile(512, seq, nb * nctx_per)
    geo = dict(b=nb, seq=seq, ctx=nctx_per, nlat=nlat, tm=tm, per=seq // tm, nlt=nlat // tm)
    assert nb < MOD_ROWS and nlat % nctx_per == 0 and nctx_per % LANES == 0

    xs = jnp.concatenate([x.reshape(nlat, d), ctx.reshape(nb * nctx_per, d)], axis=0).astype(F32)
    cond = jnp.zeros((MOD_ROWS, d), F32).at[:nb].set(c).at[nb].set(c_ctx)
    mod = _adaln(cond, ada_w, ada_b).reshape(depth, MOD_ROWS, 6, d)

    cos_a, sin_a = _axial_tables(seq, tm, pad_to_lanes=False)
    cos_m, sin_m = _axial_tables(seq, tm, pad_to_lanes=True)
    cos_r, sin_r = _ret_tables(seq, tm)

    for i in range(depth):
        kind = i % N_MIXERS
        j = i // N_MIXERS
        mod_l = mod[i]
        last = i == depth - 1
        g1 = norm_g[i, 0].reshape(1, d)
        g2 = norm_g[i, 1].reshape(1, d)
        if kind == 0:
            lam_init = 0.8 - 0.6 * math.exp(-0.3 * i)
            p = _proj(xs, g1, mod_l, da_w_in[j].astype(BF16), cos_a, sin_a,
                      ["rope16", "plain", "rope16"], geo)
            o_lat, o_ctx = _da_attention(p, da_lambda[j].astype(F32),
                                         da_subln_g[j].reshape(1, DA_V_DIM).astype(F32), lam_init, geo, not last)
            ctx_off = 0
            w_out = da_w_out[j]
        elif kind == 1:
            p = _proj(xs, g1, mod_l, ret_w_in[j].astype(BF16), cos_r, sin_r,
                      ["ret_k", "plain", "plain", "ret_q", "plain", "plain"], geo,
                      ret_scale=RET_KDIM ** -0.5)
            o_lat = o_ctx = _retention(p, ret_decay_logit[j], geo)
            ctx_off = geo["nlt"]
            w_out = ret_w_out[j]
        else:
            w_in_r, wq, wkv = _mla_weights(mla_w_in[j], mla_w_q_up[j], mla_w_kv_up[j])
            q, kn, v, kr = _mla_proj(xs, g1, mod_l, w_in_r, mla_q_norm_g[j].reshape(1, -1).astype(F32),
                                     mla_kv_norm_g[j].reshape(1, -1).astype(F32), wq, wkv, cos_m, sin_m, geo)
            o_lat, o_ctx = _mla_attention(q, kn, v, kr, geo, not last)
            ctx_off = 0
            w_out = mla_w_out[j]
        xs, h2, top_idx, gates, rank, counts = _post(o_lat, o_ctx, ctx_off, w_out.astype(BF16), xs, mod_l, g2,
                                                     moe_w_router[i], moe_b_router[i], geo, lat_only=last)
        fg = final_g.reshape(1, d).astype(F32) if last else None
        xs = _moe(h2, top_idx, gates, rank, counts[0, :N_EXPERTS], xs, mod_l, i,
                  moe_w_gate_up, moe_b_gate_up, moe_w_down, moe_b_down, geo, fg)
    return xs.reshape(nb, seq, d)
```

```python
import functools
import math

import jax
import jax.numpy as jnp
from jax import lax
from jax.experimental import pallas as pl
from jax.experimental.pallas import tpu as pltpu

F32 = jnp.float32
BF16 = jnp.bfloat16

D_MODEL = 1024
GRID_W = 64
RMS_EPS = 1e-6
ROPE_BASE = 10000.0
N_MIXERS = 3

DA_HEADS = 8
DA_HEAD_DIM = 64
DA_V_DIM = 128

RET_HEADS = 4
RET_KDIM = 256
RET_VDIM = 512
RET_STEP = 256

MLA_HEADS = 8
MLA_NOPE = 128
MLA_ROPE = 64
MLA_V = 128
MLA_Q_RANK = 256
MLA_KV_RANK = 128

N_EXPERTS = 32
TOP_K = 4
D_EXPERT = 1024
SWIGLU_LIMIT = 7.0
SWIGLU_ALPHA = 1.702

LANES = 128
MOD_ROWS = 16
MOE_BLOCK = 512
COL_CHUNK = 1024
ATTN_TQ = 1024
ATTN_TK = 1024
ATTN_Q_SUB = 2
ATTN_ROW_SPLIT = 8
VMEM_LIMIT = 56 * 1024 * 1024
NEG_BIG = -1e30


def _params(n_axes):
    return pltpu.CompilerParams(dimension_semantics=("arbitrary",) * n_axes,
                                vmem_limit_bytes=VMEM_LIMIT)


def _pick_tile(cap, *dims):
    t = cap
    while any(d % t for d in dims):
        t //= 2
    return t


def _rms(x):
    return x * lax.rsqrt(jnp.mean(x * x, axis=-1, keepdims=True) + RMS_EPS)


def _norm_mod(x, g, shift, scale):
    return (_rms(x) * g) * (1.0 + scale) + shift


def _dot(a, b):
    return jnp.dot(a, b, preferred_element_type=F32)


def _dot_split(a, b_ref):
    half = b_ref.shape[1] // 2
    return jnp.concatenate([_dot(a, b_ref[:, :half]), _dot(a, b_ref[:, half:])], axis=1)


def _dot_nt(a, b):
    return lax.dot_general(a, b, (((1,), (1,)), ((), ())), preferred_element_type=F32)


def _dot_tn(a, b):
    return lax.dot_general(a, b, (((0,), (0,)), ((), ())), preferred_element_type=F32)


def _adaln_kernel(c_ref, w_ref, b_ref, o_ref):
    c = c_ref[...]
    a = c * (1.0 / (1.0 + jnp.exp(-c)))
    o_ref[...] = jnp.dot(a, w_ref[...], precision=lax.Precision.HIGHEST,
                         preferred_element_type=F32) + b_ref[...]


def _adaln(cond, ada_w, ada_b):
    depth, d, n = ada_w.shape
    tn = COL_CHUNK
    return pl.pallas_call(
        _adaln_kernel,
        grid=(depth, n // tn),
        in_specs=[pl.BlockSpec((MOD_ROWS, d), lambda l, j: (0, 0)),
                  pl.BlockSpec((None, d, tn), lambda l, j: (l, 0, j)),
                  pl.BlockSpec((None, 1, tn), lambda l, j: (l, 0, j))],
        out_specs=pl.BlockSpec((None, MOD_ROWS, tn), lambda l, j: (l, 0, j)),
        out_shape=jax.ShapeDtypeStruct((depth, MOD_ROWS, n), F32),
        compiler_params=_params(2),
        name="adaln",
    )(cond, ada_w, ada_b.reshape(depth, 1, n))


def _rope_pairs16(x, cos, sin):
    lane = lax.broadcasted_iota(jnp.int32, x.shape, 1)
    first = (lane % 32) < 16
    partner = jnp.where(first, pltpu.roll(x, LANES - 16, 1), pltpu.roll(x, 16, 1))
    return x * cos + partner * sin


def _proj_kernel(x_ref, g_ref, mod_ref, w_ref, cos_ref, sin_ref, o_ref, *, kinds, ret_scale):
    h = _norm_mod(x_ref[...], g_ref[...], mod_ref[0:1, :], mod_ref[1:2, :]).astype(BF16)
    cos = cos_ref[...]
    sin = sin_ref[...]
    half = COL_CHUNK // 2
    for c, kind in enumerate(kinds):
        base = c * COL_CHUNK
        a = jnp.concatenate([_dot(h, w_ref[:, base:base + half]),
                             _dot(h, w_ref[:, base + half:base + COL_CHUNK])], axis=1)
        if kind == "plain":
            o_ref[:, base:base + COL_CHUNK] = a.astype(BF16)
        elif kind == "rope16":
            for g in range(COL_CHUNK // LANES):
                sl = slice(g * LANES, (g + 1) * LANES)
                o_ref[:, base + g * LANES:base + (g + 1) * LANES] = _rope_pairs16(a[:, sl], cos, sin).astype(BF16)
        else:
            scale = ret_scale if kind == "ret_k" else 1.0
            for hh in range(COL_CHUNK // RET_KDIM):
                lo = hh * RET_KDIM
                x1 = a[:, lo:lo + LANES]
                x2 = a[:, lo + LANES:lo + RET_KDIM]
                o_ref[:, base + lo:base + lo + LANES] = ((x1 * cos - x2 * sin) * scale).astype(BF16)
                o_ref[:, base + lo + LANES:base + lo + RET_KDIM] = ((x1 * sin + x2 * cos) * scale).astype(BF16)


def _proj(xs, g, mod_l, w, cos_t, sin_t, kinds, geo, ret_scale=1.0):
    nt, d = xs.shape
    n = w.shape[1]
    tm, per, nlt, nb = geo["tm"], geo["per"], geo["nlt"], geo["b"]
    assert n == COL_CHUNK * len(kinds)
    tab = pl.BlockSpec((tm, LANES), lambda i: (jnp.where(i < nlt, i % per, per), 0))
    return pl.pallas_call(
        functools.partial(_proj_kernel, kinds=tuple(kinds), ret_scale=ret_scale),
        grid=(nt // tm,),
        in_specs=[pl.BlockSpec((tm, d), lambda i: (i, 0)),
                  pl.BlockSpec((1, d), lambda i: (0, 0)),
                  pl.BlockSpec((None, 6, d), lambda i: (jnp.minimum(i // per, nb), 0, 0)),
                  pl.BlockSpec((d, n), lambda i: (0, 0)),
                  tab, tab],
        out_specs=pl.BlockSpec((tm, n), lambda i: (i, 0)),
        out_shape=jax.ShapeDtypeStruct((nt, n), BF16),
        compiler_params=_params(1),
        name="proj",
    )(xs, g, mod_l, w, cos_t, sin_t)


def _mla_proj_kernel(x_ref, g_ref, mod_ref, win_ref, qg_ref, kvg_ref, wq_ref, wkv_ref, cos_ref, sin_ref,
                     q_ref, kn_ref, v_ref, kr_ref):
    h = _norm_mod(x_ref[...], g_ref[...], mod_ref[0:1, :], mod_ref[1:2, :]).astype(BF16)
    p = _dot_split(h, win_ref)
    cos = cos_ref[...]
    sin = sin_ref[...]
    ckv = (_rms(p[:, :MLA_KV_RANK]) * kvg_ref[...]).astype(BF16)
    kv = _dot(ckv, wkv_ref[...])
    nk = MLA_HEADS * MLA_NOPE
    kn_ref[...] = kv[:, :nk].astype(BF16)
    v_ref[...] = kv[:, nk:].astype(BF16)
    kr_ref[...] = _rope_pairs16(p[:, MLA_KV_RANK + MLA_Q_RANK:], cos, sin).astype(BF16)
    cq = (_rms(p[:, MLA_KV_RANK:MLA_KV_RANK + MLA_Q_RANK]) * qg_ref[...]).astype(BF16)
    q = _dot(cq, wq_ref[...])
    for hh in range(MLA_HEADS):
        lo = slice(hh * 2 * LANES, hh * 2 * LANES + LANES)
        hi = slice(hh * 2 * LANES + LANES, (hh + 1) * 2 * LANES)
        q_ref[:, lo] = q[:, lo].astype(BF16)
        q_ref[:, hi] = _rope_pairs16(q[:, hi], cos, sin).astype(BF16)


def _mla_proj(xs, g, mod_l, w_in, qg, kvg, wq, wkv, cos_t, sin_t, geo):
    nt, d = xs.shape
    tm, per, nlt, nb = geo["tm"], geo["per"], geo["nlt"], geo["b"]
    full = lambda a: pl.BlockSpec(a.shape, lambda i: (0,) * a.ndim)
    tab = pl.BlockSpec((tm, LANES), lambda i: (jnp.where(i < nlt, i % per, per), 0))
    nq = MLA_HEADS * 2 * LANES
    nk = MLA_HEADS * MLA_NOPE
    return pl.pallas_call(
        _mla_proj_kernel,
        grid=(nt // tm,),
        in_specs=[pl.BlockSpec((tm, d), lambda i: (i, 0)), full(g),
                  pl.BlockSpec((None, 6, d), lambda i: (jnp.minimum(i // per, nb), 0, 0)),
                  full(w_in), full(qg), full(kvg), full(wq), full(wkv), tab, tab],
        out_specs=[pl.BlockSpec((tm, nq), lambda i: (i, 0)),
                   pl.BlockSpec((tm, nk), lambda i: (i, 0)),
                   pl.BlockSpec((tm, nk), lambda i: (i, 0)),
                   pl.BlockSpec((tm, LANES), lambda i: (i, 0))],
        out_shape=[jax.ShapeDtypeStruct((nt, nq), BF16), jax.ShapeDtypeStruct((nt, nk), BF16),
                   jax.ShapeDtypeStruct((nt, nk), BF16), jax.ShapeDtypeStruct((nt, LANES), BF16)],
        compiler_params=_params(1),
        name="mla_proj",
    )(xs, g, mod_l, w_in, qg, kvg, wq, wkv, cos_t, sin_t)


LOG2E = 1.4426950408889634


def _flash_init(m_ref, a_ref):
    m_ref[...] = jnp.full(m_ref.shape, -jnp.inf, F32)
    a_ref[...] = jnp.zeros(a_ref.shape, F32)


def _ones_column(rows):
    lane = lax.broadcasted_iota(jnp.int32, (rows, LANES), 1)
    return jnp.where(lane == 0, 1.0, 0.0).astype(BF16)


def _chunk_rows(j, tk):
    if isinstance(j, int):
        return pl.ds(j * tk, tk)
    return pl.ds(pl.multiple_of(j * tk, tk), tk)


def _softmax_pv(parts, stats):
    for mp, (m_ref, a_ref) in enumerate(stats):
        tq = m_ref.shape[0]
        rb = tq // ATTN_ROW_SPLIT if tq % ATTN_ROW_SPLIT == 0 else tq
        for r0 in range(0, tq, rb):
            rows = slice(r0, r0 + rb)
            tiles = [scores[mp][rows, :] for scores, _ in parts]
            m_prev = m_ref[rows, :]
            m_new = m_prev
            for s2 in tiles:
                m_new = jnp.maximum(m_new, jnp.max(s2, axis=1, keepdims=True))
            acc = jnp.exp2(m_prev - m_new) * a_ref[rows, :]
            for s2, (_, v_aug) in zip(tiles, parts):
                acc = acc + _dot(jnp.exp2(s2 - m_new).astype(BF16), v_aug)
            a_ref[rows, :] = acc
            m_ref[rows, :] = m_new


def _flash_lat(qk_fns, stats, k_ctx, v_ctx, k_at, v_at, bufs, n_chunks):
    s_a, s_b, s_c = bufs

    def qk_into(dst, k):
        for mp, f in enumerate(qk_fns):
            dst[mp] = f(k)

    qk_into(s_c, k_ctx)
    qk_into(s_a, k_at(0))
    qk_into(s_b, k_at(1))
    _softmax_pv([(s_c, v_ctx), (s_a, v_at(0))], stats)

    def body(t, carry):
        j = 2 * t + 1
        qk_into(s_a, k_at(j + 1))
        _softmax_pv([(s_b, v_at(j))], stats)
        qk_into(s_b, k_at(j + 2))
        _softmax_pv([(s_a, v_at(j + 1))], stats)
        return carry

    lax.fori_loop(0, (n_chunks - 2) // 2, body, 0)
    _softmax_pv([(s_b, v_at(n_chunks - 1))], stats)


def _flash_result(a_ref, vdim):
    a = a_ref[...]
    return a[:, :vdim] / a[:, vdim:vdim + 1]


def _da_attn_kernel(*refs, n_chunks, tk, tq, lam_init, scale):
    if n_chunks:
        (lam_ref, sg_ref, q_ref, kc_ref, vc_ref, kl_ref, vl_ref, o_ref, m0, a0, m1, a1, *bufs) = refs
    else:
        (lam_ref, sg_ref, q_ref, kc_ref, vc_ref, o_ref, m0, a0, m1, a1) = refs
    c2 = scale * LOG2E
    v_ctx = jnp.concatenate([vc_ref[...], _ones_column(vc_ref.shape[0])], axis=1)
    lf = lam_ref[...]
    lam = (jnp.exp(jnp.sum(lf[0:1] * lf[1:2], axis=1, keepdims=True))
           - jnp.exp(jnp.sum(lf[2:3] * lf[3:4], axis=1, keepdims=True)) + lam_init)

    for r0 in range(0, q_ref.shape[0], tq):
        rows = pl.ds(r0, tq)
        q = q_ref[rows, :]
        lane = lax.broadcasted_iota(jnp.int32, q.shape, 1)
        zero = jnp.zeros_like(q)
        q_lo = jnp.where(lane < DA_HEAD_DIM, q, zero)
        q_hi = jnp.where(lane >= DA_HEAD_DIM, q, zero)
        stats = [(m0.at[rows], a0.at[rows]), (m1.at[rows], a1.at[rows])]
        for m_ref, a_ref in stats:
            _flash_init(m_ref, a_ref)
        qk_fns = [lambda k, ql=q_lo: _dot_nt(ql, k) * c2, lambda k, qh=q_hi: _dot_nt(qh, k) * c2]
        if n_chunks:
            ones = _ones_column(tk)

            def k_at(j):
                return kl_ref[_chunk_rows(j, tk), :]

            def v_at(j):
                return jnp.concatenate([vl_ref[_chunk_rows(j, tk), :], ones], axis=1)

            _flash_lat(qk_fns, stats, kc_ref[...], v_ctx, k_at, v_at, bufs, n_chunks)
        else:
            _softmax_pv([([f(kc_ref[...]) for f in qk_fns], v_ctx)], stats)
        o = _flash_result(stats[0][1], DA_V_DIM) - lam * _flash_result(stats[1][1], DA_V_DIM)
        o = _rms(o) * sg_ref[...] * (1.0 - lam_init)
        o_ref[rows, :] = o.astype(BF16)


def _attn_scratch(q_rows, vdim, n_maps, tq=0, tk=0, ctx=0):
    s = []
    for _ in range(n_maps):
        s += [pltpu.VMEM((q_rows, 1), F32), pltpu.VMEM((q_rows, vdim + LANES), F32)]
    if tk:
        s += [pltpu.VMEM((n_maps, tq, tk), F32), pltpu.VMEM((n_maps, tq, tk), F32),
              pltpu.VMEM((n_maps, tq, ctx), F32)]
    return s


def _da_attention(p, lam_vecs, subg, lam_init, geo, need_ctx):
    nt = p.shape[0]
    nb, seq, ctx, nlat = geo["b"], geo["seq"], geo["ctx"], geo["nlat"]
    hh = DA_HEADS
    tq = _pick_tile(ATTN_TQ, seq)
    tk = _pick_tile(min(ATTN_TK, seq // 2), seq)
    assert (seq // tk) % 2 == 0
    tqb = tq * ATTN_Q_SUB if seq % (tq * ATTN_Q_SUB) == 0 else tq
    nqt = seq // tqb
    cb = nlat // ctx
    scale = DA_HEAD_DIM ** -0.5
    small = [pl.BlockSpec(lam_vecs.shape, lambda *a: (0, 0)), pl.BlockSpec(subg.shape, lambda *a: (0, 0))]
    o_lat = pl.pallas_call(
        functools.partial(_da_attn_kernel, n_chunks=seq // tk, tk=tk, tq=tq, lam_init=lam_init, scale=scale),
        grid=(nb, hh, nqt),
        in_specs=small + [
            pl.BlockSpec((tqb, LANES), lambda b, h, i: (b * nqt + i, 2 * hh + h)),
            pl.BlockSpec((ctx, LANES), lambda b, h, i: (cb + b, h)),
            pl.BlockSpec((ctx, LANES), lambda b, h, i: (cb + b, hh + h)),
            pl.BlockSpec((seq, LANES), lambda b, h, i: (b, h)),
            pl.BlockSpec((seq, LANES), lambda b, h, i: (b, hh + h))],
        out_specs=pl.BlockSpec((tqb, LANES), lambda b, h, i: (b * nqt + i, h)),
        out_shape=jax.ShapeDtypeStruct((nlat, hh * DA_V_DIM), BF16),
        scratch_shapes=_attn_scratch(tqb, DA_V_DIM, 2, tq, tk, ctx),
        compiler_params=_params(3),
        name="da_attn_lat",
    )(lam_vecs, subg, p, p, p, p, p)
    if not need_ctx:
        return o_lat, o_lat
    o_ctx = pl.pallas_call(
        functools.partial(_da_attn_kernel, n_chunks=0, tk=tk, tq=ctx, lam_init=lam_init, scale=scale),
        grid=(nb, hh),
        in_specs=small + [
            pl.BlockSpec((ctx, LANES), lambda b, h: (cb + b, 2 * hh + h)),
            pl.BlockSpec((ctx, LANES), lambda b, h: (cb + b, h)),
            pl.BlockSpec((ctx, LANES), lambda b, h: (cb + b, hh + h))],
        out_specs=pl.BlockSpec((ctx, LANES), lambda b, h: (b, h)),
        out_shape=jax.ShapeDtypeStruct((nt - nlat, hh * DA_V_DIM), BF16),
        scratch_shapes=_attn_scratch(ctx, DA_V_DIM, 2),
        compiler_params=_params(2),
        name="da_attn_ctx",
    )(lam_vecs, subg, p, p, p)
    return o_lat, o_ctx


def _mla_attn_kernel(*refs, n_chunks, tk, tq, scale):
    if n_chunks:
        (q_ref, knc_ref, krc_ref, vc_ref, knl_ref, krl_ref, vl_ref, o_ref, m0, a0, *bufs) = refs
    else:
        (q_ref, knc_ref, krc_ref, vc_ref, o_ref, m0, a0) = refs
    c2 = scale * LOG2E
    k_ctx = jnp.concatenate([knc_ref[...], krc_ref[...]], axis=1)
    v_ctx = jnp.concatenate([vc_ref[...], _ones_column(vc_ref.shape[0])], axis=1)
    for r0 in range(0, q_ref.shape[0], tq):
        rows = pl.ds(r0, tq)
        q = q_ref[rows, :]
        stats = [(m0.at[rows], a0.at[rows])]
        _flash_init(*stats[0])
        qk_fns = [lambda k, q=q: _dot_nt(q, k) * c2]
        if n_chunks:
            ones = _ones_column(tk)

            def k_at(j):
                kr = _chunk_rows(j, tk)
                return jnp.concatenate([knl_ref[kr, :], krl_ref[kr, :]], axis=1)

            def v_at(j):
                return jnp.concatenate([vl_ref[_chunk_rows(j, tk), :], ones], axis=1)

            _flash_lat(qk_fns, stats, k_ctx, v_ctx, k_at, v_at, bufs, n_chunks)
        else:
            _softmax_pv([([qk_fns[0](k_ctx)], v_ctx)], stats)
        o_ref[rows, :] = _flash_result(stats[0][1], MLA_V).astype(BF16)


def _mla_attention(q, kn, v, kr, geo, need_ctx):
    nt = q.shape[0]
    nb, seq, ctx, nlat = geo["b"], geo["seq"], geo["ctx"], geo["nlat"]
    hh = MLA_HEADS
    tq = _pick_tile(ATTN_TQ, seq)
    tk = _pick_tile(min(ATTN_TK, seq // 2), seq)
    assert (seq // tk) % 2 == 0
    tqb = tq * ATTN_Q_SUB if seq % (tq * ATTN_Q_SUB) == 0 else tq
    nqt = seq // tqb
    cb = nlat // ctx
    scale = (MLA_NOPE + MLA_ROPE) ** -0.5
    o_lat = pl.pallas_call(
        functools.partial(_mla_attn_kernel, n_chunks=seq // tk, tk=tk, tq=tq, scale=scale),
        grid=(nb, hh, nqt),
        in_specs=[
            pl.BlockSpec((tqb, 2 * LANES), lambda b, h, i: (b * nqt + i, h)),
            pl.BlockSpec((ctx, LANES), lambda b, h, i: (cb + b, h)),
            pl.BlockSpec((ctx, LANES), lambda b, h, i: (cb + b, 0)),
            pl.BlockSpec((ctx, LANES), lambda b, h, i: (cb + b, h)),
            pl.BlockSpec((seq, LANES), lambda b, h, i: (b, h)),
            pl.BlockSpec((seq, LANES), lambda b, h, i: (b, 0)),
            pl.BlockSpec((seq, LANES), lambda b, h, i: (b, h))],
        out_specs=pl.BlockSpec((tqb, LANES), lambda b, h, i: (b * nqt + i, h)),
        out_shape=jax.ShapeDtypeStruct((nlat, hh * MLA_V), BF16),
        scratch_shapes=_attn_scratch(tqb, MLA_V, 1, tq, tk, ctx),
        compiler_params=_params(3),
        name="mla_attn_lat",
    )(q, kn, kr, v, kn, kr, v)
    if not need_ctx:
        return o_lat, o_lat
    o_ctx = pl.pallas_call(
        functools.partial(_mla_attn_kernel, n_chunks=0, tk=tk, tq=ctx, scale=scale),
        grid=(nb, hh),
        in_specs=[
            pl.BlockSpec((ctx, 2 * LANES), lambda b, h: (cb + b, h)),
            pl.BlockSpec((ctx, LANES), lambda b, h: (cb + b, h)),
            pl.BlockSpec((ctx, LANES), lambda b, h: (cb + b, 0)),
            pl.BlockSpec((ctx, LANES), lambda b, h: (cb + b, h))],
        out_specs=pl.BlockSpec((ctx, LANES), lambda b, h: (b, h)),
        out_shape=jax.ShapeDtypeStruct((nt - nlat, hh * MLA_V), BF16),
        scratch_shapes=_attn_scratch(ctx, MLA_V, 1),
        compiler_params=_params(2),
        name="mla_attn_ctx",
    )(q, kn, kr, v)
    return o_lat, o_ctx


def _ret_kernel(*refs, reverse, c):
    if reverse:
        dl_ref, q_ref, k_ref, v0_ref, v1_ref, of_ref, g0_ref, g1_ref, o_ref, st = refs
    else:
        dl_ref, q_ref, k_ref, v0_ref, v1_ref, o_ref, st = refs
    s = pl.program_id(1)

    @pl.when(s == 0)
    def _():
        st[...] = jnp.zeros(st.shape, F32)

    dl = dl_ref[...]
    lsig = jnp.minimum(dl, 0.0) - jnp.log(1.0 + jnp.exp(-jnp.abs(dl)))
    row = lax.broadcasted_iota(jnp.int32, dl.shape, 0)
    col = lax.broadcasted_iota(jnp.int32, dl.shape, 1)
    i2 = lax.broadcasted_iota(jnp.int32, (c, c), 0)
    j2 = lax.broadcasted_iota(jnp.int32, (c, c), 1)
    pos = lax.broadcasted_iota(jnp.int32, (c, 1), 0).astype(F32)
    dist = ((j2 - i2) if reverse else (i2 - j2)).astype(F32)
    keep = dist >= 0.0
    dist = jnp.where(keep, dist, 0.0)
    per_blk = COL_CHUNK // RET_VDIM

    for hd in range(RET_HEADS):
        pick = jnp.logical_and(row == (1 if reverse else 0), col == hd)
        lg = jnp.sum(jnp.sum(jnp.where(pick, lsig, 0.0), axis=1, keepdims=True), axis=0, keepdims=True)
        if reverse:
            q_dec = jnp.exp((c - pos) * lg)
            k_dec = jnp.exp(pos * lg)
        else:
            q_dec = jnp.exp((pos + 1.0) * lg)
            k_dec = jnp.exp((c - 1.0 - pos) * lg)
        intra = jnp.where(keep, jnp.exp(dist * lg), 0.0)
        chunk_dec = jnp.exp(c * lg)

        ksl = slice(hd * RET_KDIM, (hd + 1) * RET_KDIM)
        vsl = slice((hd % per_blk) * RET_VDIM, (hd % per_blk + 1) * RET_VDIM)
        osl = slice(hd * RET_VDIM, (hd + 1) * RET_VDIM)
        qb = q_ref[:, ksl]
        kb = k_ref[:, ksl]
        vb = (v0_ref if hd < per_blk else v1_ref)[:, vsl]
        state = st[hd]
        sc = _dot_nt(qb, kb) * intra
        o = _dot(sc.astype(BF16), vb) + _dot((qb.astype(F32) * q_dec).astype(BF16), state.astype(BF16))
        st[hd] = chunk_dec * state + _dot_tn((kb.astype(F32) * k_dec).astype(BF16), vb)
        if reverse:
            tot = _rms(of_ref[:, osl] + o)
            g = (g0_ref if hd < per_blk else g1_ref)[:, vsl].astype(F32)
            o_ref[:, osl] = (g * (1.0 / (1.0 + jnp.exp(-g))) * tot).astype(BF16)
        else:
            o_ref[:, osl] = o


def _retention(p, decay_logit, geo):
    nt = p.shape[0]
    nb, seq, ctx, nlat = geo["b"], geo["seq"], geo["ctx"], geo["nlat"]
    c = _pick_tile(RET_STEP, ctx, seq)
    ncc, ncl = ctx // c, seq // c
    hh = RET_HEADS
    kq, kv = RET_KDIM, RET_VDIM
    wb = COL_CHUNK
    assert hh * kq == wb and hh * kv == 2 * wb
    dl = jnp.zeros((8, LANES), F32).at[:2, :hh].set(decay_logit.astype(F32))

    def rows_fwd(b, s):
        return jnp.where(s < ncc, nlat // c + b * ncc + s, b * ncl + (s - ncc))

    def rows_bwd(b, s):
        return jnp.where(s < ncc, nlat // c + b * ncc + (ncc - 1 - s), b * ncl + (ncl - 1 - (s - ncc)))

    def blk(rows, col):
        return pl.BlockSpec((c, wb), lambda b, s: (rows(b, s), col))

    def specs(rows):
        return [pl.BlockSpec((8, LANES), lambda b, s: (0, 0)), blk(rows, 3), blk(rows, 0), blk(rows, 1), blk(rows, 2)]

    grid = (nb, ncc + ncl)
    scratch = [pltpu.VMEM((hh, kq, kv), F32)]
    o_f = pl.pallas_call(
        functools.partial(_ret_kernel, reverse=False, c=c),
        grid=grid,
        in_specs=specs(rows_fwd),
        out_specs=pl.BlockSpec((c, hh * kv), lambda b, s: (rows_fwd(b, s), 0)),
        out_shape=jax.ShapeDtypeStruct((nt, hh * kv), F32),
        scratch_shapes=scratch,
        compiler_params=_params(2),
        name="ret_fwd",
    )(dl, p, p, p, p)
    return pl.pallas_call(
        functools.partial(_ret_kernel, reverse=True, c=c),
        grid=grid,
        in_specs=specs(rows_bwd) + [
            pl.BlockSpec((c, hh * kv), lambda b, s: (rows_bwd(b, s), 0)), blk(rows_bwd, 4), blk(rows_bwd, 5)],
        out_specs=pl.BlockSpec((c, hh * kv), lambda b, s: (rows_bwd(b, s), 0)),
        out_shape=jax.ShapeDtypeStruct((nt, hh * kv), BF16),
        scratch_shapes=scratch,
        compiler_params=_params(2),
        name="ret_bwd",
    )(dl, p, p, p, p, o_f, p, p)


def _post_kernel(ol_ref, oc_ref, w_ref, x_ref, mod_ref, g2_ref, wr_ref, br_ref, tri_ref,
                 xo_ref, h2_ref, idx_ref, gate_ref, rank_ref, cnt_ref, cnt_scr, *, nlt):
    t = pl.program_id(0)

    @pl.when(t == 0)
    def _():
        cnt_scr[...] = jnp.zeros(cnt_scr.shape, F32)

    o = jnp.where(t < nlt, ol_ref[...], oc_ref[...])
    y = _dot_split(o, w_ref)
    x = x_ref[...] + mod_ref[2:3, :] * y
    xo_ref[...] = x
    h2 = _norm_mod(x, g2_ref[...], mod_ref[3:4, :], mod_ref[4:5, :])
    _store_rows_as_tiles(h2_ref, h2)
    h_top = pltpu.bitcast(pltpu.bitcast(h2, jnp.uint32) & jnp.uint32(0xFFFF0000), F32)
    h_hi = h_top.astype(BF16)
    h_lo = (h2 - h_top).astype(BF16)
    logits = (_dot(h_hi, wr_ref[0]) + _dot(h_lo, wr_ref[0]) + _dot(h_hi, wr_ref[1])) + br_ref[...]
    lane = lax.broadcasted_iota(jnp.int32, logits.shape, 1).astype(F32)
    work = logits
    vals, idxs = [], []
    for _ in range(TOP_K):
        mx = jnp.max(work, axis=1, keepdims=True)
        ix = jnp.min(jnp.where(work == mx, lane, float(LANES)), axis=1, keepdims=True)
        vals.append(mx)
        idxs.append(ix)
        work = jnp.where(lane == ix, -jnp.inf, work)
    es = [jnp.exp(v - vals[0]) for v in vals]
    den = es[0] + es[1] + es[2] + es[3]
    onehots = [lane == ix for ix in idxs]
    oh = jnp.zeros(logits.shape, F32)
    for m in onehots:
        oh = oh + jnp.where(m, 1.0, 0.0)
    before = _dot(tri_ref[...], oh.astype(BF16)) + cnt_scr[0:1, :]
    idx_out = jnp.zeros(logits.shape, F32)
    gate_out = jnp.zeros(logits.shape, F32)
    rank_out = jnp.zeros(logits.shape, F32)
    for r in range(TOP_K):
        sel = lane == float(r)
        rk = jnp.sum(jnp.where(onehots[r], before, 0.0), axis=1, keepdims=True)
        idx_out = jnp.where(sel, idxs[r], idx_out)
        gate_out = jnp.where(sel, es[r] / den, gate_out)
        rank_out = jnp.where(sel, rk, rank_out)
    idx_ref[...] = idx_out.astype(jnp.int32)
    gate_ref[...] = gate_out
    rank_ref[...] = rank_out.astype(jnp.int32)
    cnt_scr[0:1, :] = cnt_scr[0:1, :] + jnp.sum(oh, axis=0, keepdims=True)
    cnt_ref[...] = cnt_scr[...]


def _post(o_lat, o_ctx, ctx_off, w_out, xs, mod_l, g2, w_router, b_router, geo, lat_only=False):
    n_all, d = xs.shape
    nt = geo["nlat"] if lat_only else n_all
    ko = o_lat.shape[1]
    tm, per, nb, nlt = geo["tm"], geo["per"], geo["b"], geo["nlt"]
    ii = lax.broadcasted_iota(jnp.int32, (tm, tm), 0)
    jj = lax.broadcasted_iota(jnp.int32, (tm, tm), 1)
    tri = (jj < ii).astype(BF16)
    wr32 = jnp.zeros((d, LANES), F32).at[:, :N_EXPERTS].set(w_router.astype(F32))
    wr_top = lax.bitcast_convert_type(lax.bitcast_convert_type(wr32, jnp.uint32) & jnp.uint32(0xFFFF0000), F32)
    wr = jnp.stack([wr_top.astype(BF16), (wr32 - wr_top).astype(BF16)])
    br = jnp.full((1, LANES), NEG_BIG, F32).at[0, :N_EXPERTS].set(b_router.astype(F32))
    row = lambda w: pl.BlockSpec((tm, w), lambda i: (i, 0))
    full = lambda a: pl.BlockSpec(a.shape, lambda i: (0,) * a.ndim)
    return pl.pallas_call(
        functools.partial(_post_kernel, nlt=nlt),
        grid=(nt // tm,),
        in_specs=[pl.BlockSpec((tm, ko), lambda i: (jnp.minimum(i, nlt - 1), 0)),
                  pl.BlockSpec((tm, ko), lambda i: (jnp.maximum(i - nlt, 0) + ctx_off, 0)),
                  full(w_out), row(d),
                  pl.BlockSpec((None, 6, d), lambda i: (jnp.minimum(i // per, nb), 0, 0)),
                  full(g2), full(wr), full(br), full(tri)],
        out_specs=[row(d), pl.BlockSpec((tm * SUBLANES, LANES), lambda i: (i, 0)),
                   row(LANES), row(LANES), row(LANES),
                   pl.BlockSpec((8, LANES), lambda i: (0, 0))],
        out_shape=[jax.ShapeDtypeStruct((n_all, d), F32), jax.ShapeDtypeStruct((nt * SUBLANES, LANES), F32),
                   jax.ShapeDtypeStruct((nt, LANES), jnp.int32), jax.ShapeDtypeStruct((nt, LANES), F32),
                   jax.ShapeDtypeStruct((nt, LANES), jnp.int32), jax.ShapeDtypeStruct((8, LANES), F32)],
        scratch_shapes=[pltpu.VMEM((8, LANES), F32)],
        input_output_aliases={3: 0},
        compiler_params=_params(1),
        name="post",
    )(o_lat, o_ctx, w_out, xs, mod_l, g2, wr, br, tri)


SUBLANES = 8
DMA_UNROLL = 8


def _row_copy(src, dst, sem):
    return pltpu.make_async_copy(src, dst, sem)


def _store_rows_as_tiles(ref, val):
    rows = val.shape[0]
    for j in range(SUBLANES):
        ref[pl.ds(j, rows, stride=SUBLANES), :] = val[:, j * LANES:(j + 1) * LANES]


def _load_rows_from_tiles(ref):
    rows = ref.shape[0] // SUBLANES
    return jnp.concatenate([ref[pl.ds(j, rows, stride=SUBLANES), :] for j in range(SUBLANES)], axis=1)


def _tile_of(ref, row):
    if isinstance(row, int):
        return ref.at[pl.ds(row * SUBLANES, SUBLANES)]
    return ref.at[pl.ds(pl.multiple_of(row * SUBLANES, SUBLANES), SUBLANES)]


def _dispatch_kernel(zs_ref, zc_ref, slot_ref, h_ref, xs_ref, zero_scr, ring, sems, *, tm):
    t = pl.program_id(0)
    cur = t % 2
    ring[cur] = h_ref[...]

    @pl.when(t == 0)
    def _():
        zero_scr[...] = jnp.zeros(zero_scr.shape, F32)

        def per_expert(e, carry):
            start = zs_ref[e]
            n = zc_ref[e]

            def issue(r, c2):
                _row_copy(zero_scr, _tile_of(xs_ref, start + r), sems.at[1]).start()
                return c2

            lax.fori_loop(0, n, issue, 0)

            def drain(r, c2):
                _row_copy(zero_scr, _tile_of(xs_ref, 0), sems.at[1]).wait()
                return c2

            lax.fori_loop(0, n, drain, 0)
            return carry

        lax.fori_loop(0, N_EXPERTS, per_expert, 0)

    def issue(r, carry):
        for k in range(TOP_K):
            s = slot_ref[r * TOP_K + k]
            _row_copy(_tile_of(ring.at[cur], r), _tile_of(xs_ref, s), sems.at[cur]).start(priority=k % 2)
        return carry

    lax.fori_loop(0, tm, issue, 0, unroll=DMA_UNROLL)

    def drain_of(which):
        def drain(r, carry):
            for k in range(TOP_K):
                _row_copy(_tile_of(ring.at[which], 0), _tile_of(xs_ref, 0), sems.at[which]).wait()
            return carry
        lax.fori_loop(0, tm, drain, 0, unroll=DMA_UNROLL)

    @pl.when(t > 0)
    def _():
        drain_of(1 - cur)

    @pl.when(t == pl.num_programs(0) - 1)
    def _():
        drain_of(cur)


def _dispatch(h2, slot_flat, zstart, zcount, n_slots, geo):
    sub, lanes = SUBLANES, h2.shape[1]
    nt = h2.shape[0] // sub
    tm = geo["tm"]
    return pl.pallas_call(
        functools.partial(_dispatch_kernel, tm=tm),
        grid_spec=pltpu.PrefetchScalarGridSpec(
            num_scalar_prefetch=2,
            grid=(nt // tm,),
            in_specs=[pl.BlockSpec((tm * TOP_K,), lambda i, zs, zc: (i,), memory_space=pltpu.SMEM),
                      pl.BlockSpec((tm * sub, lanes), lambda i, zs, zc: (i, 0))],
            out_specs=pl.BlockSpec(memory_space=pl.ANY),
            scratch_shapes=[pltpu.VMEM((sub, lanes), F32), pltpu.VMEM((2, tm * sub, lanes), F32),
                            pltpu.SemaphoreType.DMA((2,))]),
        out_shape=jax.ShapeDtypeStruct((n_slots * sub, lanes), F32),
        compiler_params=_params(1),
        name="moe_dispatch",
    )(zstart, zcount, slot_flat, h2)


def _expert_kernel(be_ref, nu_ref, x_ref, wgu_ref, bgu_ref, wd_ref, bd_ref, y_ref, wgu_bf, wd_bf):
    i = pl.program_id(0)

    @pl.when(jnp.logical_or(i == 0, be_ref[i] != be_ref[jnp.maximum(i - 1, 0)]))
    def _():
        wgu_bf[...] = wgu_ref[...].astype(BF16)
        wd_bf[...] = wd_ref[...].astype(BF16)

    @pl.when(i < nu_ref[0])
    def _():
        x = _load_rows_from_tiles(x_ref).astype(BF16)
        gu = _dot(x, wgu_bf[...]) + bgu_ref[...]
        glu = jnp.minimum(gu[:, :D_EXPERT], SWIGLU_LIMIT)
        lin = jnp.clip(gu[:, D_EXPERT:], -SWIGLU_LIMIT, SWIGLU_LIMIT)
        act = glu * (1.0 / (1.0 + jnp.exp(-SWIGLU_ALPHA * glu))) * (lin + 1.0)
        _store_rows_as_tiles(y_ref, _dot(act.astype(BF16), wd_bf[...]) + bd_ref[...])


def _experts(xs, block_expert, n_used, layer, wgu, bgu, wd, bd):
    sub, lanes = SUBLANES, xs.shape[1]
    ns = xs.shape[0] // sub
    nblk = ns // MOE_BLOCK
    depth, ne, d, f2 = wgu.shape
    f = wd.shape[2]
    blk = lambda i, be, nu: (jnp.minimum(i, nu[0] - 1), 0)
    wsel = lambda i, be, nu: (layer, be[i], 0, 0)
    return pl.pallas_call(
        _expert_kernel,
        grid_spec=pltpu.PrefetchScalarGridSpec(
            num_scalar_prefetch=2,
            grid=(nblk,),
            in_specs=[pl.BlockSpec((MOE_BLOCK * sub, lanes), blk),
                      pl.BlockSpec((None, None, d, f2), wsel),
                      pl.BlockSpec((None, None, 1, f2), wsel),
                      pl.BlockSpec((None, None, f, d), wsel),
                      pl.BlockSpec((None, None, 1, d), wsel)],
            out_specs=pl.BlockSpec((MOE_BLOCK * sub, lanes), blk),
            scratch_shapes=[pltpu.VMEM((d, f2), BF16), pltpu.VMEM((f, d), BF16)]),
        out_shape=jax.ShapeDtypeStruct((ns * sub, lanes), F32),
        compiler_params=_params(1),
        name="moe_experts",
    )(block_expert, n_used, xs, wgu, bgu.reshape(depth, ne, 1, f2), wd, bd.reshape(depth, ne, 1, d))


def _combine_kernel(*refs, tm, final):
    if final:
        slot_ref, next_slot_ref, gate_ref, x_ref, mod_ref, fg_ref, y_ref, xo_ref, ybuf, sems = refs
    else:
        slot_ref, next_slot_ref, gate_ref, x_ref, mod_ref, y_ref, xo_ref, ybuf, sems = refs
    t = pl.program_id(0)
    cur = t % 2

    def gather(slots, buf):
        def issue(r, carry):
            for k in range(TOP_K):
                s = slots[r * TOP_K + k]
                _row_copy(_tile_of(y_ref, s), _tile_of(ybuf.at[buf, k], r), sems.at[buf]).start(priority=k % 2)
            return carry
        lax.fori_loop(0, tm, issue, 0, unroll=DMA_UNROLL)

    @pl.when(t == 0)
    def _():
        gather(slot_ref, 0)

    @pl.when(t + 1 < pl.num_programs(0))
    def _():
        gather(next_slot_ref, 1 - cur)

    def drain(r, carry):
        for k in range(TOP_K):
            _row_copy(_tile_of(y_ref, 0), _tile_of(ybuf.at[cur, k], 0), sems.at[cur]).wait()
        return carry

    lax.fori_loop(0, tm, drain, 0, unroll=DMA_UNROLL)
    gates = gate_ref[...]
    f = gates[:, 0:1] * _load_rows_from_tiles(ybuf.at[cur, 0])
    for k in range(1, TOP_K):
        f = f + gates[:, k:k + 1] * _load_rows_from_tiles(ybuf.at[cur, k])
    x = x_ref[...] + mod_ref[5:6, :] * f
    xo_ref[...] = _rms(x) * fg_ref[...] if final else x


def _combine(y, slot_flat, gates, xs, mod_l, geo, final_g=None):
    nt, d = xs.shape
    tm, per, nb = geo["tm"], geo["per"], geo["b"]
    final = final_g is not None
    n_tiles = geo["nlt"] if final else nt // tm
    last = n_tiles - 1
    extra_specs = [pl.BlockSpec((1, d), lambda i: (0, 0))] if final else []
    extra_args = [final_g] if final else []
    return pl.pallas_call(
        functools.partial(_combine_kernel, tm=tm, final=final),
        grid=(n_tiles,),
        in_specs=[pl.BlockSpec((tm * TOP_K,), lambda i: (i,), memory_space=pltpu.SMEM),
                  pl.BlockSpec((tm * TOP_K,), lambda i: (jnp.minimum(i + 1, last),), memory_space=pltpu.SMEM),
                  pl.BlockSpec((tm, LANES), lambda i: (i, 0)),
                  pl.BlockSpec((tm, d), lambda i: (i, 0)),
                  pl.BlockSpec((None, 6, d), lambda i: (jnp.minimum(i // per, nb), 0, 0))]
                 + extra_specs + [pl.BlockSpec(memory_space=pl.ANY)],
        out_specs=pl.BlockSpec((tm, d), lambda i: (i, 0)),
        out_shape=jax.ShapeDtypeStruct((n_tiles * tm, d), F32),
        scratch_shapes=[pltpu.VMEM((2, TOP_K, tm * SUBLANES, LANES), F32), pltpu.SemaphoreType.DMA((2,))],
        input_output_aliases={} if final else {3: 0},
        compiler_params=_params(1),
        name="moe_combine",
    )(slot_flat, slot_flat, gates, xs, mod_l, *extra_args, y)


def _moe(h2, top_idx, gates, rank, counts, xs, mod_l, layer, wgu, bgu, wd, bd, geo, final_g=None):
    nt = top_idx.shape[0]
    n_assign = nt * TOP_K
    nblk = n_assign // MOE_BLOCK + N_EXPERTS
    n_slots = nblk * MOE_BLOCK
    counts = counts.astype(jnp.int32)
    padded = (counts + MOE_BLOCK - 1) // MOE_BLOCK * MOE_BLOCK
    pad_end = jnp.cumsum(padded)
    pad_start = pad_end - padded
    experts = jnp.arange(N_EXPERTS, dtype=jnp.int32)
    idx4 = top_idx[:, :TOP_K]
    base = jnp.sum(jnp.where(idx4[:, :, None] == experts[None, None, :], pad_start[None, None, :], 0), axis=-1)
    slot_flat = (base + rank[:, :TOP_K]).reshape(n_assign).astype(jnp.int32)
    n_used = (pad_end[-1] // MOE_BLOCK).astype(jnp.int32)
    blk_id = jnp.minimum(jnp.arange(nblk, dtype=jnp.int32), n_used - 1)
    block_expert = jnp.minimum(
        jnp.sum((pad_end[None, :] <= (blk_id * MOE_BLOCK)[:, None]).astype(jnp.int32), axis=1), N_EXPERTS - 1)
    xs_slots = _dispatch(h2, slot_flat, (pad_start + counts).astype(jnp.int32),
                         (padded - counts).astype(jnp.int32), n_slots, geo)
    y = _experts(xs_slots, block_expert.astype(jnp.int32), n_used.reshape(1), layer, wgu, bgu, wd, bd)
    return _combine(y, slot_flat, gates, xs, mod_l, geo, final_g)


def _with_identity_rows(cos, sin, tm):
    ones = jnp.ones((tm, cos.shape[1]), F32)
    return jnp.concatenate([cos, ones], axis=0), jnp.concatenate([sin, 0.0 * ones], axis=0)


def _axial_tables(seq, tm, pad_to_lanes):
    pos = jnp.arange(seq)
    rows = (pos // GRID_W).astype(F32)
    cols = (pos % GRID_W).astype(F32)
    half = DA_HEAD_DIM // 2
    inv = ROPE_BASE ** (-jnp.arange(0, half, 2, dtype=F32) / half)
    ar = rows[:, None] * inv[None, :]
    ac = cols[:, None] * inv[None, :]
    cos64 = jnp.concatenate([jnp.cos(ar), jnp.cos(ar), jnp.cos(ac), jnp.cos(ac)], axis=1)
    sin64 = jnp.concatenate([-jnp.sin(ar), jnp.sin(ar), -jnp.sin(ac), jnp.sin(ac)], axis=1)
    if pad_to_lanes:
        cos = jnp.concatenate([cos64, jnp.ones_like(cos64)], axis=1)
        sin = jnp.concatenate([sin64, jnp.zeros_like(sin64)], axis=1)
    else:
        cos = jnp.concatenate([cos64, cos64], axis=1)
        sin = jnp.concatenate([sin64, sin64], axis=1)
    return _with_identity_rows(cos, sin, tm)


def _ret_tables(seq, tm):
    inv = ROPE_BASE ** (-jnp.arange(0, RET_KDIM, 2, dtype=F32) / RET_KDIM)
    ang = jnp.arange(seq).astype(F32)[:, None] * inv[None, :]
    return _with_identity_rows(jnp.cos(ang), jnp.sin(ang), tm)


def _mla_weights(w_in, w_q_up, w_kv_up):
    d = w_in.shape[0]
    kvw = MLA_KV_RANK + MLA_ROPE
    w_in_r = jnp.concatenate([w_in[:, :MLA_KV_RANK], w_in[:, kvw:], w_in[:, MLA_KV_RANK:kvw],
                              jnp.zeros((d, LANES - MLA_ROPE), w_in.dtype)], axis=1)
    wq = w_q_up.reshape(MLA_Q_RANK, MLA_HEADS, MLA_NOPE + MLA_ROPE)
    wq = jnp.concatenate([wq, jnp.zeros((MLA_Q_RANK, MLA_HEADS, LANES - MLA_ROPE), wq.dtype)], axis=2)
    wq = wq.reshape(MLA_Q_RANK, MLA_HEADS * 2 * LANES)
    wkv = w_kv_up.reshape(MLA_KV_RANK, MLA_HEADS, MLA_NOPE + MLA_V)
    wkv = jnp.concatenate([wkv[:, :, :MLA_NOPE].reshape(MLA_KV_RANK, -1),
                           wkv[:, :, MLA_NOPE:].reshape(MLA_KV_RANK, -1)], axis=1)
    return w_in_r.astype(BF16), wq.astype(BF16), wkv.astype(BF16)


def kernel(x, c, ctx, c_ctx, ada_w, ada_b, norm_g, final_g, da_w_in, da_w_out, da_lambda, da_subln_g,
           ret_w_in, ret_decay_logit, ret_w_out, mla_w_in, mla_q_norm_g, mla_w_q_up, mla_kv_norm_g,
           mla_w_kv_up, mla_w_out, moe_w_router, moe_b_router, moe_w_gate_up, moe_b_gate_up,
           moe_w_down, moe_b_down):
    nb, seq, d = x.shape
    nctx_per = ctx.shape[1]
    depth = ada_w.shape[0]
    nlat = nb * seq
    tm = _pick_tile(512, seq, nb * nctx_per)
    geo = dict(b=nb, seq=seq, ctx=nctx_per, nlat=nlat, tm=tm, per=seq // tm, nlt=nlat // tm)
    assert nb < MOD_ROWS and nlat % nctx_per == 0 and nctx_per % LANES == 0

    xs = jnp.concatenate([x.reshape(nlat, d), ctx.reshape(nb * nctx_per, d)], axis=0).astype(F32)
    cond = jnp.zeros((MOD_ROWS, d), F32).at[:nb].set(c).at[nb].set(c_ctx)
    mod = _adaln(cond, ada_w, ada_b).reshape(depth, MOD_ROWS, 6, d)

    cos_a, sin_a = _axial_tables(seq, tm, pad_to_lanes=False)
    cos_m, sin_m = _axial_tables(seq, tm, pad_to_lanes=True)
    cos_r, sin_r = _ret_tables(seq, tm)

    for i in range(depth):
        kind = i % N_MIXERS
        j = i // N_MIXERS
        mod_l = mod[i]
        last = i == depth - 1
        g1 = norm_g[i, 0].reshape(1, d)
        g2 = norm_g[i, 1].reshape(1, d)
        if kind == 0:
            lam_init = 0.8 - 0.6 * math.exp(-0.3 * i)
            p = _proj(xs, g1, mod_l, da_w_in[j].astype(BF16), cos_a, sin_a,
                      ["rope16", "plain", "rope16"], geo)
            o_lat, o_ctx = _da_attention(p, da_lambda[j].astype(F32),
                                         da_subln_g[j].reshape(1, DA_V_DIM).astype(F32), lam_init, geo, not last)
            ctx_off = 0
            w_out = da_w_out[j]
        elif kind == 1:
            p = _proj(xs, g1, mod_l, ret_w_in[j].astype(BF16), cos_r, sin_r,
                      ["ret_k", "plain", "plain", "ret_q", "plain", "plain"], geo,
                      ret_scale=RET_KDIM ** -0.5)
            o_lat = o_ctx = _retention(p, ret_decay_logit[j], geo)
            ctx_off = geo["nlt"]
            w_out = ret_w_out[j]
        else:
            w_in_r, wq, wkv = _mla_weights(mla_w_in[j], mla_w_q_up[j], mla_w_kv_up[j])
            q, kn, v, kr = _mla_proj(xs, g1, mod_l, w_in_r, mla_q_norm_g[j].reshape(1, -1).astype(F32),
                                     mla_kv_norm_g[j].reshape(1, -1).astype(F32), wq, wkv, cos_m, sin_m, geo)
            o_lat, o_ctx = _mla_attention(q, kn, v, kr, geo, not last)
            ctx_off = 0
            w_out = mla_w_out[j]
        xs, h2, top_idx, gates, rank, counts = _post(o_lat, o_ctx, ctx_off, w_out.astype(BF16), xs, mod_l, g2,
                                                     moe_w_router[i], moe_b_router[i], geo, lat_only=last)
        fg = final_g.reshape(1, d).astype(F32) if last else None
        xs = _moe(h2, top_idx, gates, rank, counts[0, :N_EXPERTS], xs, mod_l, i,
                  moe_w_gate_up, moe_b_gate_up, moe_w_down, moe_b_down, geo, fg)
    return xs.reshape(nb, seq, d)
```
